```python
import jax, jax.numpy as jnp
from jax import lax
import numpy as np

D_MODEL = 1024
BATCH = 2
SEQ = 8192
DEPTH = 2
DEC_BATCH = 32
DEC_SEQ = 1
PAST_LEN = 8192
PAGE_SIZE = 128

MIX_WIDTH = D_MODEL
HEAD_DIM = 64
CONV_A_WIDTH = MIX_WIDTH // 4
CONV_A_KERNEL = 31
H_DELTA = (3 * MIX_WIDTH) // (8 * HEAD_DIM)
H_FOX = (3 * MIX_WIDTH) // (8 * HEAD_DIM)
DELTA_WIDTH = H_DELTA * HEAD_DIM
FOX_WIDTH = H_FOX * HEAD_DIM
SHORT_CONV = 4
DELTA_CHUNK = 64
Q_BLOCK = 128
FORGET_BIAS_INIT = 2.0
D_FF = ((-(-8 * D_MODEL // 3) + 255) // 256) * 256
SPLITS = [
    CONV_A_WIDTH,
    2 * CONV_A_WIDTH,
    2 * CONV_A_WIDTH + 3 * DELTA_WIDTH,
    2 * CONV_A_WIDTH + 4 * DELTA_WIDTH,
    2 * CONV_A_WIDTH + 4 * DELTA_WIDTH + H_DELTA,
    2 * CONV_A_WIDTH + 4 * DELTA_WIDTH + 2 * H_DELTA,
    2 * CONV_A_WIDTH + 4 * DELTA_WIDTH + 2 * H_DELTA + 3 * FOX_WIDTH,
]
N_IN = SPLITS[-1] + H_FOX

kernel_name = 'hymba_conformer_gdn_fox_step'


def rmsnorm(x, g, eps=1e-6):
    xf = x.astype(jnp.float32)
    y = xf * lax.rsqrt(jnp.mean(xf * xf, axis=-1, keepdims=True) + eps)
    return (y * g.astype(jnp.float32)).astype(x.dtype)


def layernorm(x, g, b, eps=1e-5):
    xf = x.astype(jnp.float32)
    mu = jnp.mean(xf, axis=-1, keepdims=True)
    xc = xf - mu
    var = jnp.mean(xc * xc, axis=-1, keepdims=True)
    return (xc * lax.rsqrt(var + eps) * g.astype(jnp.float32) + b.astype(jnp.float32)).astype(x.dtype)


def l2norm(x, eps=1e-6):
    xf = x.astype(jnp.float32)
    return xf * lax.rsqrt(jnp.sum(xf * xf, axis=-1, keepdims=True) + eps)


def causal_dwconv(x, buf, w):
    width = w.shape[0]
    x_ext = jnp.concatenate([buf.astype(x.dtype), x], axis=1)
    y = lax.conv_general_dilated(
        x_ext, w.astype(x.dtype)[:, None, :], window_strides=(1,), padding='VALID',
        dimension_numbers=('NWC', 'WIO', 'NWC'), feature_group_count=x.shape[-1])
    return y, x_ext[:, x_ext.shape[1] - (width - 1):]


def gated_delta_rule(q, k, v, g, beta, s0):
    B, T, H, Dk = q.shape
    Dv = v.shape[-1]
    C = DELTA_CHUNK
    n = -(-T // C)
    pad = n * C - T

    def chunks(a):
        a = a.astype(jnp.float32)
        a = jnp.pad(a, [(0, 0), (0, pad)] + [(0, 0)] * (a.ndim - 2))
        a = a.reshape((B, n, C) + a.shape[2:])
        return jnp.moveaxis(a, (1, 3), (0, 2))

    qc = chunks(q) * (Dk ** -0.5)
    kc, vc, gc, bc = chunks(k), chunks(v), chunks(g), chunks(beta)
    gcum = jnp.cumsum(gc, axis=-1)
    idx = jnp.arange(C)
    causal = idx[:, None] >= idx[None, :]
    strict = idx[:, None] > idx[None, :]
    decay = jnp.exp(jnp.where(causal, gcum[..., :, None] - gcum[..., None, :], -jnp.inf))
    kb = kc * bc[..., None]
    lower = jnp.where(strict, jnp.einsum('nbhid,nbhjd->nbhij', kb, kc) * decay, 0.0)
    eye = jnp.eye(C, dtype=jnp.float32)
    tmat = lax.linalg.triangular_solve(lower + eye, jnp.broadcast_to(eye, lower.shape),
                                       left_side=True, lower=True, unit_diagonal=True)
    u = tmat @ (vc * bc[..., None])
    w = tmat @ (kb * jnp.exp(gcum)[..., None])
    qk = jnp.where(causal, jnp.einsum('nbhid,nbhjd->nbhij', qc, kc) * decay, 0.0)

    def step(S, xs):
        q_i, k_i, u_i, w_i, qk_i, g_i = xs
        v_new = u_i - jnp.einsum('bhck,bhkv->bhcv', w_i, S)
        o_i = (jnp.einsum('bhck,bhkv->bhcv', q_i * jnp.exp(g_i)[..., None], S)
               + jnp.einsum('bhij,bhjv->bhiv', qk_i, v_new))
        g_last = g_i[..., -1]
        S = (S * jnp.exp(g_last)[..., None, None]
             + jnp.einsum('bhck,bhcv->bhkv', k_i * jnp.exp(g_last[..., None] - g_i)[..., None], v_new))
        return S, o_i

    S, o = lax.scan(step, s0.astype(jnp.float32), (qc, kc, u, w, qk, gcum))
    o = jnp.moveaxis(o, (0, 2), (1, 3)).reshape(B, n * C, H, Dv)[:, :T]
    return o, S


def forgetting_attention(q, k_all, v_all, logf_all, past_len):
    B, T, H, D = q.shape
    S = k_all.shape[1]
    ck = jnp.moveaxis(jnp.cumsum(logf_all.astype(jnp.float32), axis=1), 2, 1)
    qb = Q_BLOCK if T % Q_BLOCK == 0 else T
    nb = T // qb
    q_blocks = jnp.moveaxis(q.reshape(B, nb, qb, H, D), 1, 0)
    cq_blocks = jnp.moveaxis(ck[:, :, past_len:].reshape(B, H, nb, qb), 2, 0)
    kpos = jnp.arange(S)
    scale = D ** -0.5

    def block(args):
        qi, cqi, bi = args
        s = jnp.einsum('bqhd,bkhd->bhqk', qi, k_all).astype(jnp.float32) * scale
        s = s + cqi[..., :, None] - ck[:, :, None, :]
        qpos = past_len + bi * qb + jnp.arange(qb)
        s = jnp.where(kpos[None, :] <= qpos[:, None], s, -jnp.inf)
        p = jax.nn.softmax(s, axis=-1)
        return jnp.einsum('bhqk,bkhd->bqhd', p.astype(v_all.dtype), v_all)

    out = lax.map(block, (q_blocks, cq_blocks, jnp.arange(nb)))
    return jnp.moveaxis(out, 0, 1).reshape(B, T, H, D)


def trunk_layer(x, p, l, buf_a, buf_b, s_delta, past_k, past_v, past_logf):
    B, T, _ = x.shape
    h = rmsnorm(x, p['g_pre_mix'][l])
    proj = h @ p['w_in'][l]
    a_val, a_gate, qkv_b, z_b, beta_raw, alpha_raw, qkv_c, f_raw = jnp.split(proj, SPLITS, axis=-1)

    u = a_val * jax.nn.sigmoid(a_gate)
    ca, new_buf_a = causal_dwconv(u, buf_a, p['conv_a_w'][l])
    y_a = jax.nn.silu(layernorm(ca + p['conv_a_b'][l], p['ln_a_g'][l], p['ln_a_b'][l]))

    cb, new_buf_b = causal_dwconv(qkv_b, buf_b, p['conv_b_w'][l])
    cb = jax.nn.silu(cb).reshape(B, T, 3, H_DELTA, HEAD_DIM)
    beta = jax.nn.sigmoid(beta_raw.astype(jnp.float32))
    g = -jnp.exp(p['a_log'][l].astype(jnp.float32)) * jax.nn.softplus(
        alpha_raw.astype(jnp.float32) + p['dt_bias'][l].astype(jnp.float32))
    o_d, new_s = gated_delta_rule(l2norm(cb[:, :, 0]), l2norm(cb[:, :, 1]), cb[:, :, 2], g, beta, s_delta)
    y_b = rmsnorm(o_d, p['norm_b_g'][l]) * jax.nn.silu(
        z_b.reshape(B, T, H_DELTA, HEAD_DIM).astype(jnp.float32))
    y_b = y_b.reshape(B, T, DELTA_WIDTH).astype(x.dtype)

    qkv = qkv_c.reshape(B, T, 3, H_FOX, HEAD_DIM)
    qf, kf, vf = qkv[:, :, 0], qkv[:, :, 1], qkv[:, :, 2]
    logf = jax.nn.log_sigmoid((f_raw + p['f_bias'][l]).astype(jnp.float32))
    k_all = jnp.concatenate([past_k.astype(kf.dtype), kf], axis=1)
    v_all = jnp.concatenate([past_v.astype(vf.dtype), vf], axis=1)
    logf_all = jnp.concatenate([past_logf.astype(jnp.float32), logf], axis=1)
    o_f = forgetting_attention(qf, k_all, v_all, logf_all, past_k.shape[1])
    y_c = o_f.reshape(B, T, FOX_WIDTH).astype(x.dtype)

    mix = jnp.concatenate([y_a, y_b, y_c], axis=-1) @ p['w_out'][l]
    x = x + rmsnorm(mix, p['g_post_mix'][l])
    h = rmsnorm(x, p['g_pre_ffn'][l])
    f = (jax.nn.silu(h @ p['w_gate'][l]) * (h @ p['w_up'][l])) @ p['w_down'][l]
    x = x + rmsnorm(f, p['g_post_ffn'][l])
    return x, (kf, vf, logf, new_buf_a, new_buf_b, new_s)


def setup_inputs(seed: int = 0) -> dict:
    key = jax.random.key(seed)
    ks = jax.random.split(key, 32)
    n_pages = PAST_LEN // PAGE_SIZE
    n_phys = (DEC_BATCH * n_pages * 5) // 4

    def nrm(k, shape, s):
        return jax.random.normal(k, shape, jnp.float32) * s

    x_prompt = nrm(ks[0], (BATCH, SEQ, D_MODEL), 1.0)
    x_sample = nrm(ks[1], (DEC_BATCH, DEC_SEQ, D_MODEL), 1.0)
    cache_k = nrm(ks[2], (DEPTH, n_phys, PAGE_SIZE, H_FOX, HEAD_DIM), 1.0)
    cache_v = nrm(ks[3], (DEPTH, n_phys, PAGE_SIZE, H_FOX, HEAD_DIM), 1.0)
    cache_logf = jax.nn.log_sigmoid(FORGET_BIAS_INIT + nrm(ks[4], (DEPTH, n_phys, PAGE_SIZE, H_FOX), 1.0))
    page_table = jax.random.permutation(ks[5], n_phys)[: DEC_BATCH * n_pages].reshape(
        DEC_BATCH, n_pages).astype(jnp.int32)
    state_conv_a = nrm(ks[6], (DEPTH, DEC_BATCH, CONV_A_KERNEL - 1, CONV_A_WIDTH), 0.5)
    state_conv_b = nrm(ks[7], (DEPTH, DEC_BATCH, SHORT_CONV - 1, 3 * DELTA_WIDTH), 1.0)
    state_delta = nrm(ks[8], (DEPTH, DEC_BATCH, H_DELTA, HEAD_DIM, HEAD_DIM), 0.1)

    w_in = nrm(ks[9], (DEPTH, D_MODEL, N_IN), D_MODEL ** -0.5)
    conv_a_w = nrm(ks[10], (DEPTH, CONV_A_KERNEL, CONV_A_WIDTH), CONV_A_KERNEL ** -0.5)
    conv_a_b = nrm(ks[11], (DEPTH, CONV_A_WIDTH), 0.02)
    ln_a_g = 1.0 + nrm(ks[12], (DEPTH, CONV_A_WIDTH), 0.05)
    ln_a_b = nrm(ks[13], (DEPTH, CONV_A_WIDTH), 0.02)
    conv_b_w = nrm(ks[14], (DEPTH, SHORT_CONV, 3 * DELTA_WIDTH), SHORT_CONV ** -0.5)
    a_log = jnp.log(jax.random.uniform(ks[15], (DEPTH, H_DELTA), jnp.float32, 1.0, 16.0))
    dt = jax.random.uniform(ks[16], (DEPTH, H_DELTA), jnp.float32, 0.001, 0.1)
    dt_bias = jnp.log(jnp.expm1(dt))
    norm_b_g = 1.0 + nrm(ks[17], (DEPTH, HEAD_DIM), 0.05)
    f_bias = FORGET_BIAS_INIT + nrm(ks[18], (DEPTH, H_FOX), 0.1)
    w_out = nrm(ks[19], (DEPTH, MIX_WIDTH, D_MODEL), MIX_WIDTH ** -0.5)
    g_pre_mix = 1.0 + nrm(ks[20], (DEPTH, D_MODEL), 0.05)
    g_post_mix = 1.0 + nrm(ks[21], (DEPTH, D_MODEL), 0.05)
    g_pre_ffn = 1.0 + nrm(ks[22], (DEPTH, D_MODEL), 0.05)
    g_post_ffn = 1.0 + nrm(ks[23], (DEPTH, D_MODEL), 0.05)
    w_gate = nrm(ks[24], (DEPTH, D_MODEL, D_FF), D_MODEL ** -0.5)
    w_up = nrm(ks[25], (DEPTH, D_MODEL, D_FF), D_MODEL ** -0.5)
    w_down = nrm(ks[26], (DEPTH, D_FF, D_MODEL), D_FF ** -0.5)
    return {
        'x_prompt': x_prompt, 'x_sample': x_sample,
        'cache_k': cache_k, 'cache_v': cache_v, 'cache_logf': cache_logf, 'page_table': page_table,
        'state_conv_a': state_conv_a, 'state_conv_b': state_conv_b, 'state_delta': state_delta,
        'w_in': w_in, 'conv_a_w': conv_a_w, 'conv_a_b': conv_a_b, 'ln_a_g': ln_a_g, 'ln_a_b': ln_a_b,
        'conv_b_w': conv_b_w, 'a_log': a_log, 'dt_bias': dt_bias, 'norm_b_g': norm_b_g, 'f_bias': f_bias,
        'w_out': w_out, 'g_pre_mix': g_pre_mix, 'g_post_mix': g_post_mix, 'g_pre_ffn': g_pre_ffn,
        'g_post_ffn': g_post_ffn, 'w_gate': w_gate, 'w_up': w_up, 'w_down': w_down,
    }


def reference(x_prompt, x_sample, cache_k, cache_v, cache_logf, page_table, state_conv_a, state_conv_b,
              state_delta, w_in, conv_a_w, conv_a_b, ln_a_g, ln_a_b, conv_b_w, a_log, dt_bias, norm_b_g,
              f_bias, w_out, g_pre_mix, g_post_mix, g_pre_ffn, g_post_ffn, w_gate, w_up, w_down):
    p = dict(w_in=w_in, conv_a_w=conv_a_w, conv_a_b=conv_a_b, ln_a_g=ln_a_g, ln_a_b=ln_a_b,
             conv_b_w=conv_b_w, a_log=a_log, dt_bias=dt_bias, norm_b_g=norm_b_g, f_bias=f_bias,
             w_out=w_out, g_pre_mix=g_pre_mix, g_post_mix=g_post_mix, g_pre_ffn=g_pre_ffn,
             g_post_ffn=g_post_ffn, w_gate=w_gate, w_up=w_up, w_down=w_down)
    B = x_prompt.shape[0]
    Bd = x_sample.shape[0]
    n_pages = page_table.shape[1]
    past_len = n_pages * PAGE_SIZE
    dt = x_prompt.dtype
    zero_buf_a = jnp.zeros((B, CONV_A_KERNEL - 1, CONV_A_WIDTH), dt)
    zero_buf_b = jnp.zeros((B, SHORT_CONV - 1, 3 * DELTA_WIDTH), dt)
    zero_s = jnp.zeros((B, H_DELTA, HEAD_DIM, HEAD_DIM), jnp.float32)
    empty_kv = jnp.zeros((B, 0, H_FOX, HEAD_DIM), dt)
    empty_logf = jnp.zeros((B, 0, H_FOX), jnp.float32)

    xp, xs = x_prompt, x_sample
    prompt_states, sample_states = [], []
    for l in range(DEPTH):
        xp, sp = trunk_layer(xp, p, l, zero_buf_a, zero_buf_b, zero_s, empty_kv, empty_kv, empty_logf)
        prompt_states.append(sp)
        past_k = cache_k[l][page_table].reshape(Bd, past_len, H_FOX, HEAD_DIM)
        past_v = cache_v[l][page_table].reshape(Bd, past_len, H_FOX, HEAD_DIM)
        past_logf = cache_logf[l][page_table].reshape(Bd, past_len, H_FOX)
        xs, ss = trunk_layer(xs, p, l, state_conv_a[l], state_conv_b[l], state_delta[l],
                             past_k, past_v, past_logf)
        sample_states.append(ss)

    k_p, v_p, logf_p, conv_a_p, conv_b_p, delta_p = [jnp.stack(t) for t in zip(*prompt_states)]
    k_s, v_s, logf_s, conv_a_s, conv_b_s, delta_s = [jnp.stack(t) for t in zip(*sample_states)]
    return (xp, xs, k_p, v_p, logf_p, conv_a_p, conv_b_p, delta_p,
            k_s, v_s, logf_s, conv_a_s, conv_b_s, delta_s)
```

```python
import functools

import jax
import jax.numpy as jnp
from jax import lax
from jax.experimental import pallas as pl
from jax.experimental.pallas import tpu as pltpu

F32 = jnp.float32
BF16 = jnp.bfloat16

HEAD_DIM = 64
N_HEADS = 6
N_PAIRS = N_HEADS // 2
HW = N_HEADS * HEAD_DIM
CONV_A_WIDTH = 256
CONV_A_KERNEL = 31
SHORT_CONV = 4
DELTA_CHUNK = 64
PAGE_SIZE = 128
LANES = 128
RMS_EPS = 1e-6
LN_EPS = 1e-5
VMEM_LIMIT_BYTES = 56 * 1024 * 1024
NEG_INF = float("-inf")

_NT = (((1,), (1,)), ((), ()))


def _dot(a, b):
    return jnp.dot(a, b, preferred_element_type=F32)


def _dot_nt(a, b):
    return lax.dot_general(a, b, _NT, preferred_element_type=F32)


def _dot_exact(a, b):
    return jnp.dot(a, b, preferred_element_type=F32, precision=lax.Precision.HIGHEST)


def _sigmoid(x):
    return 1.0 / (1.0 + jnp.exp(-x))


def _softplus(x):
    return jnp.maximum(x, 0.0) + jnp.log(1.0 + jnp.exp(-jnp.abs(x)))


def _silu(x):
    return x * _sigmoid(x)


def _rms(x, g):
    return x * lax.rsqrt(jnp.mean(x * x, axis=-1, keepdims=True) + RMS_EPS) * g


def _params(sem):
    return pltpu.CompilerParams(dimension_semantics=sem, vmem_limit_bytes=VMEM_LIMIT_BYTES)


def _iota(shape, dim):
    return lax.broadcasted_iota(jnp.int32, shape, dim)


def _split_hi_lo(x):
    hi = x.astype(BF16)
    lo = (x - hi.astype(F32)).astype(BF16)
    return hi, lo


def _group_sum(x2, gmat):
    hi, lo = _split_hi_lo(x2)
    return _dot(hi, gmat) + _dot(lo, gmat)


_MAIN_SPLITS = (0, 256, 512, 1664, 2048, 2432)


def _gate_fns(raw, add, a_log):
    x = raw + add
    logf = -_softplus(-x)
    beta = _sigmoid(x)
    g = -jnp.exp(a_log) * _softplus(x)
    return logf, beta, g


def _inproj_kernel(x_ref, g_ref, wmain_ref, wkv_ref, wsm_ref, addc_ref, alogc_ref, addr_ref, alogr_ref,
                   u_ref, qkvb_ref, zb_ref, qx_ref, kt_ref, vt_ref, ktb_ref, vtb_ref, smt_ref, smr_ref):
    hb = _rms(x_ref[...], g_ref[...]).astype(BF16)

    def proj(i):
        return _dot_nt(hb, wmain_ref[_MAIN_SPLITS[i]:_MAIN_SPLITS[i + 1], :])

    u_ref[...] = proj(0) * _sigmoid(proj(1))
    qkvb_ref[...] = proj(2)
    zb_ref[...] = proj(3)
    q = proj(4) * (HEAD_DIM ** -0.5)
    lane = _iota((1, LANES), 1)
    for h in range(N_HEADS):
        p = h // 2
        own = (lane < HEAD_DIM) if h % 2 == 0 else (lane >= HEAD_DIM)
        qx_ref[:, h * LANES:(h + 1) * LANES] = jnp.where(own, q[:, p * LANES:(p + 1) * LANES], 0.0).astype(BF16)

    kv = _dot_nt(wkv_ref[...], hb)
    kt_ref[0] = kv[:HW]
    vt_ref[0] = kv[HW:]
    ktb_ref[0] = kv[:HW].astype(BF16)
    vtb_ref[0] = kv[HW:].astype(BF16)

    rawt = _dot_nt(wsm_ref[0:24, :], hb)
    logf, beta, g = _gate_fns(rawt, addc_ref[...], alogc_ref[...])
    smt_ref[0, 0:8] = logf[0:8]
    smt_ref[0, 8:16] = beta[8:16]
    smt_ref[0, 16:24] = g[16:24]

    rawr = _dot_nt(hb, wsm_ref[...])
    logf, beta, g = _gate_fns(rawr, addr_ref[...], alogr_ref[...])
    col = _iota((1, LANES), 1)
    smr_ref[...] = jnp.where(col < 8, logf, jnp.where(col < 16, beta, g))


def _inproj(x, g, wmain, wkv, wsm, addc, alogc, addr, alogr, *, nb, tm):
    m, d = x.shape
    t = m // nb
    nt = t // tm
    row = lambda i: (i, 0)
    const = lambda i: (0, 0)
    tr = lambda i: (i // nt, 0, i % nt)
    outs = [
        jax.ShapeDtypeStruct((m, CONV_A_WIDTH), F32), jax.ShapeDtypeStruct((m, 3 * HW), F32),
        jax.ShapeDtypeStruct((m, HW), F32), jax.ShapeDtypeStruct((m, N_HEADS * LANES), BF16),
        jax.ShapeDtypeStruct((nb, HW, t), F32), jax.ShapeDtypeStruct((nb, HW, t), F32),
        jax.ShapeDtypeStruct((nb, HW, t), BF16), jax.ShapeDtypeStruct((nb, HW, t), BF16),
        jax.ShapeDtypeStruct((nb, 24, t), F32), jax.ShapeDtypeStruct((m, LANES), F32),
    ]
    out_specs = [
        pl.BlockSpec((tm, CONV_A_WIDTH), row), pl.BlockSpec((tm, 3 * HW), row), pl.BlockSpec((tm, HW), row),
        pl.BlockSpec((tm, N_HEADS * LANES), row),
        pl.BlockSpec((1, HW, tm), tr), pl.BlockSpec((1, HW, tm), tr),
        pl.BlockSpec((1, HW, tm), tr), pl.BlockSpec((1, HW, tm), tr),
        pl.BlockSpec((1, 24, tm), tr), pl.BlockSpec((tm, LANES), row),
    ]
    in_specs = [
        pl.BlockSpec((tm, d), row), pl.BlockSpec((1, d), const),
        pl.BlockSpec(wmain.shape, const), pl.BlockSpec(wkv.shape, const), pl.BlockSpec(wsm.shape, const),
        pl.BlockSpec((24, 1), const), pl.BlockSpec((24, 1), const),
        pl.BlockSpec((1, LANES), const), pl.BlockSpec((1, LANES), const),
    ]
    return pl.pallas_call(
        _inproj_kernel, grid=(m // tm,), in_specs=in_specs, out_specs=out_specs, out_shape=outs,
        compiler_params=_params(("arbitrary",)), name="inproj",
    )(x, g, wmain, wkv, wsm, addc, alogc, addr, alogr)


_CA_PAD = 32
_CA_ROWS = 64


def _conv_a_kernel(u_ref, buf_ref, w_ref, b_ref, lg_ref, lb_ref, y_ref, nbuf_ref, ext_ref, *, tt):
    ti = pl.program_id(1)

    @pl.when(ti == 0)
    def _():
        ext_ref[0:_CA_PAD] = buf_ref[0]

    ext_ref[_CA_PAD:_CA_PAD + tt] = u_ref[0]
    off = _CA_PAD - (CONV_A_KERNEL - 1)
    rows = min(_CA_ROWS, tt)
    for c in range(tt // rows):
        acc = jnp.zeros((rows, CONV_A_WIDTH), F32)
        for j in range(CONV_A_KERNEL):
            acc = acc + w_ref[j:j + 1, :] * ext_ref[off + j + c * rows:off + j + c * rows + rows, :]
        ca = acc + b_ref[...]
        mu = jnp.mean(ca, axis=-1, keepdims=True)
        xc = ca - mu
        var = jnp.mean(xc * xc, axis=-1, keepdims=True)
        y_ref[0, c * rows:(c + 1) * rows, :] = _silu(xc * lax.rsqrt(var + LN_EPS) * lg_ref[...] + lb_ref[...])
    tail = ext_ref[tt:tt + _CA_PAD]
    nbuf_ref[0] = tail
    ext_ref[0:_CA_PAD] = tail


def _conv_a(u, buf, w, b, lg, lb, *, tt):
    nb, t, c = u.shape
    const = lambda bi, ti: (0, 0)
    return pl.pallas_call(
        functools.partial(_conv_a_kernel, tt=tt),
        grid=(nb, t // tt),
        in_specs=[pl.BlockSpec((1, tt, c), lambda bi, ti: (bi, ti, 0)),
                  pl.BlockSpec((1, _CA_PAD, c), lambda bi, ti: (bi, 0, 0)),
                  pl.BlockSpec((_CA_PAD, c), const), pl.BlockSpec((1, c), const),
                  pl.BlockSpec((1, c), const), pl.BlockSpec((1, c), const)],
        out_specs=[pl.BlockSpec((1, tt, c), lambda bi, ti: (bi, ti, 0)),
                   pl.BlockSpec((1, _CA_PAD, c), lambda bi, ti: (bi, 0, 0))],
        out_shape=[jax.ShapeDtypeStruct((nb, t, c), F32), jax.ShapeDtypeStruct((nb, _CA_PAD, c), F32)],
        scratch_shapes=[pltpu.VMEM((_CA_PAD + tt, c), F32)],
        compiler_params=_params(("arbitrary", "arbitrary")), name="conv_a",
    )(u, buf, w, b, lg, lb)


_CB_PAD = 8


def _pair_cols(x, p):
    return x[:, p * LANES:(p + 1) * LANES]


def _head_cols(sm, base, p, lane_lo):
    c0 = sm[:, base + 2 * p:base + 2 * p + 1]
    c1 = sm[:, base + 2 * p + 1:base + 2 * p + 2]
    return jnp.where(lane_lo, c0, c1)


def _block_diag(x, lane_lo):
    return jnp.concatenate([jnp.where(lane_lo, x, 0.0), jnp.where(lane_lo, 0.0, x)], axis=0)


def _delta_kernel(x_ref, buf_ref, w_ref, smr_ref, smt_ref, z_ref, gb_ref, s0_ref, gmat_ref, tril_ref, triu_ref,
                  y_ref, nbuf_ref, sout_ref, ext_ref, s_ref, *, tt):
    ti = pl.program_id(1)
    nchunk = tt // DELTA_CHUNK
    c = DELTA_CHUNK

    @pl.when(ti == 0)
    def _():
        ext_ref[0:_CB_PAD] = buf_ref[0]
        s_ref[...] = s0_ref[0]

    ext_ref[_CB_PAD:_CB_PAD + tt] = x_ref[0]
    off = _CB_PAD - (SHORT_CONV - 1)
    acc = w_ref[0:1, :] * ext_ref[off:off + tt, :]
    for j in range(1, SHORT_CONV):
        acc = acc + w_ref[j:j + 1, :] * ext_ref[off + j:off + j + tt, :]
    tail = ext_ref[tt:tt + _CB_PAD]
    nbuf_ref[0] = tail
    ext_ref[0:_CB_PAD] = tail
    cb = _silu(acc)

    gmat = gmat_ref[...]
    q = cb[:, 0:HW]
    k = cb[:, HW:2 * HW]
    v = cb[:, 2 * HW:3 * HW]
    qs = q * lax.rsqrt(_group_sum(q * q, gmat) + 1e-6) * (HEAD_DIM ** -0.5)
    kn = k * lax.rsqrt(_group_sum(k * k, gmat) + 1e-6)

    smr = smr_ref[0]
    smt = smt_ref[0]
    gc_col = _dot_exact(tril_ref[...], smr)
    gc_row = _dot_exact(smt[16:24], triu_ref[...])
    if tt < LANES:
        gc_row = jnp.concatenate([gc_row, jnp.zeros((8, LANES - tt), F32)], axis=1)

    lane = _iota((1, LANES), 1)
    lane_lo = lane < HEAD_DIM
    ri = _iota((c, LANES), 0)
    ci_ = _iota((c, LANES), 1)
    cj = jnp.where(ci_ < HEAD_DIM, ci_, ci_ - HEAD_DIM)
    causal = ri >= cj
    strict = ri > cj
    eye2 = jnp.where(ri == cj, 1.0, 0.0)
    r128 = _iota((LANES, LANES), 0)
    c128 = _iota((LANES, LANES), 1)
    bdmask = (r128 < HEAD_DIM) == (c128 < HEAD_DIM)

    o_chunks = []
    for ci in range(nchunk):
        r0, r1 = ci * c, (ci + 1) * c
        o_pairs = []
        for p in range(N_PAIRS):
            qs_p = _pair_cols(qs, p)[r0:r1]
            kn_p = _pair_cols(kn, p)[r0:r1]
            v_p = _pair_cols(v, p)[r0:r1]
            beta = _head_cols(smr[r0:r1], 8, p, lane_lo)
            gi = _head_cols(gc_col[r0:r1], 16, p, lane_lo)
            blk = gc_row[:, (ci // 2) * LANES:(ci // 2 + 1) * LANES]
            rot = pltpu.roll(blk, HEAD_DIM, 1)
            if ci % 2 == 0:
                gj = jnp.where(lane_lo, blk[2 * p:2 * p + 1], rot[2 * p + 1:2 * p + 2])
            else:
                gj = jnp.where(lane_lo, rot[2 * p:2 * p + 1], blk[2 * p + 1:2 * p + 2])
            glast = gi[c - 1:c, :]
            eg = jnp.exp(gi)
            decay = jnp.exp(jnp.where(causal, gi - gj, NEG_INF))
            kb = kn_p * beta
            kd = _block_diag(kn_p, lane_lo)
            lmat = jnp.where(strict, _dot_nt(kb, kd) * decay, 0.0)
            x_inv = eye2 - lmat
            pw = lmat
            for _ in range(5):
                pw = _dot_exact(pw, _block_diag(pw, lane_lo))
                x_inv = x_inv + _dot_exact(x_inv, _block_diag(pw, lane_lo))
            u = _dot(x_inv, _block_diag(v_p * beta, lane_lo))
            w = _dot(x_inv, _block_diag(kb * eg, lane_lo))
            qk = jnp.where(causal, _dot_nt(qs_p, kd) * decay, 0.0)
            kdec_t = (kn_p * jnp.exp(glast - gi)).T
            kw_bd = jnp.where(bdmask, _dot(kdec_t, w), 0.0)
            ku_bd = jnp.where(bdmask, _dot(kdec_t, u), 0.0)
            q2 = qs_p * eg - _dot(qk, _block_diag(w, lane_lo))
            o2 = _dot(qk, _block_diag(u, lane_lo))
            s_bd = s_ref[p]
            o_pairs.append(_dot(q2, s_bd) + o2)
            s_ref[p] = s_bd * jnp.exp(glast) - _dot(kw_bd, s_bd) + ku_bd
        o_chunks.append(jnp.concatenate(o_pairs, axis=1))
    o = jnp.concatenate(o_chunks, axis=0) if nchunk > 1 else o_chunks[0]
    ms = _group_sum(o * o, gmat) * (1.0 / HEAD_DIM)
    y_ref[0] = o * lax.rsqrt(ms + RMS_EPS) * gb_ref[...] * _silu(z_ref[0])
    sout_ref[0] = s_ref[...]


def _delta(x, buf, w, smr, smt, z, gb, s0, *, tt):
    nb, t, cw = x.shape
    const = lambda bi, ti: (0, 0)
    head = jnp.arange(HW) // HEAD_DIM
    gmat = (head[:, None] == head[None, :]).astype(BF16)
    chunk = jnp.arange(tt) // DELTA_CHUNK
    same = chunk[:, None] == chunk[None, :]
    pos = jnp.arange(tt)
    tril = (same & (pos[:, None] >= pos[None, :])).astype(F32)
    triu = tril.T
    return pl.pallas_call(
        functools.partial(_delta_kernel, tt=tt),
        grid=(nb, t // tt),
        in_specs=[pl.BlockSpec((1, tt, cw), lambda bi, ti: (bi, ti, 0)),
                  pl.BlockSpec((1, _CB_PAD, cw), lambda bi, ti: (bi, 0, 0)),
                  pl.BlockSpec((SHORT_CONV, cw), const),
                  pl.BlockSpec((1, tt, LANES), lambda bi, ti: (bi, ti, 0)),
                  pl.BlockSpec((1, 24, tt), lambda bi, ti: (bi, 0, ti)),
                  pl.BlockSpec((1, tt, HW), lambda bi, ti: (bi, ti, 0)),
                  pl.BlockSpec((1, HW), const),
                  pl.BlockSpec((1, N_PAIRS, LANES, LANES), lambda bi, ti: (bi, 0, 0, 0)),
                  pl.BlockSpec((HW, HW), const), pl.BlockSpec((tt, tt), const), pl.BlockSpec((tt, tt), const)],
        out_specs=[pl.BlockSpec((1, tt, HW), lambda bi, ti: (bi, ti, 0)),
                   pl.BlockSpec((1, _CB_PAD, cw), lambda bi, ti: (bi, 0, 0)),
                   pl.BlockSpec((1, N_PAIRS, LANES, LANES), lambda bi, ti: (bi, 0, 0, 0))],
        out_shape=[jax.ShapeDtypeStruct((nb, t, HW), F32), jax.ShapeDtypeStruct((nb, _CB_PAD, cw), F32),
                   jax.ShapeDtypeStruct((nb, N_PAIRS, LANES, LANES), F32)],
        scratch_shapes=[pltpu.VMEM((_CB_PAD + tt, cw), F32), pltpu.VMEM((N_PAIRS, LANES, LANES), F32)],
        compiler_params=_params(("arbitrary", "arbitrary")), name="delta",
    )(x, buf, w, smr, smt, z, gb, s0, gmat, tril, triu)


def _cumsum_kernel(x_ref, c_ref, carry_ref, *, tl):
    ti = pl.program_id(1)

    @pl.when(ti == 0)
    def _():
        carry_ref[...] = jnp.zeros_like(carry_ref)

    rr = _iota((tl, tl), 0)
    cc = _iota((tl, tl), 1)
    triu = jnp.where(rr <= cc, 1.0, 0.0)
    cs = _dot_exact(x_ref[0, 0:8], triu) + carry_ref[:, 0:1]
    c_ref[0] = cs
    carry_ref[...] = jnp.broadcast_to(cs[:, tl - 1:tl], carry_ref.shape)


def _logf_cumsum(smt, *, tl):
    nb, _, t = smt.shape
    return pl.pallas_call(
        functools.partial(_cumsum_kernel, tl=tl),
        grid=(nb, t // tl),
        in_specs=[pl.BlockSpec((1, 24, tl), lambda bi, ti: (bi, 0, ti))],
        out_specs=pl.BlockSpec((1, 8, tl), lambda bi, ti: (bi, 0, ti)),
        out_shape=jax.ShapeDtypeStruct((nb, 8, t), F32),
        scratch_shapes=[pltpu.VMEM((8, LANES), F32)],
        compiler_params=_params(("arbitrary", "arbitrary")), name="logf_cumsum",
    )(smt)


def _fox_kernel(qi_ref, ki_ref, qx_ref, kt_ref, vt_ref, ck_ref, cq_ref, o_ref, m_ref, l_ref, acc_ref, *, tq, tk):
    step = pl.program_id(1)
    qi = qi_ref[step]
    ki = ki_ref[step]

    @pl.when(ki == 0)
    def _():
        m_ref[...] = jnp.full_like(m_ref, NEG_INF)
        l_ref[...] = jnp.zeros_like(l_ref)
        acc_ref[...] = jnp.zeros_like(acc_ref)

    def update(masked):
        if masked:
            keep = _iota((tq, tk), 1) <= _iota((tq, tk), 0)
        for h in range(N_HEADS):
            p = h // 2
            s = _dot(qx_ref[0, :, h * LANES:(h + 1) * LANES], kt_ref[0, p * LANES:(p + 1) * LANES, :])
            s = s + (cq_ref[0, h:h + 1, 0:1] - ck_ref[0, h:h + 1, :])
            if masked:
                s = jnp.where(keep, s, NEG_INF)
            m_prev = m_ref[h]
            m_new = jnp.maximum(m_prev, jnp.max(s, axis=-1, keepdims=True))
            alpha = jnp.exp(m_prev - m_new)
            pr = jnp.exp(s - m_new)
            l_ref[h] = alpha * l_ref[h] + jnp.sum(pr, axis=-1, keepdims=True)
            acc_ref[h] = alpha * acc_ref[h] + _dot_nt(pr.astype(BF16), vt_ref[0, p * LANES:(p + 1) * LANES, :])
            m_ref[h] = m_new

    @pl.when(ki < qi)
    def _():
        update(False)

    @pl.when(ki == qi)
    def _():
        update(True)
        lane_lo = _iota((1, LANES), 1) < HEAD_DIM
        for p in range(N_PAIRS):
            lo = acc_ref[2 * p] / l_ref[2 * p]
            hi = acc_ref[2 * p + 1] / l_ref[2 * p + 1]
            o_ref[0, :, p * LANES:(p + 1) * LANES] = jnp.where(lane_lo, lo, hi)


def _fox_prompt(qx, ktb, vtb, cs, *, tq):
    nb, t, _ = qx.shape
    nq = t // tq
    qi_tab = jnp.asarray([qi for qi in range(nq) for _ in range(qi + 1)], jnp.int32)
    ki_tab = jnp.asarray([ki for qi in range(nq) for ki in range(qi + 1)], jnp.int32)
    grid_spec = pltpu.PrefetchScalarGridSpec(
        num_scalar_prefetch=2, grid=(nb, int(qi_tab.shape[0])),
        in_specs=[pl.BlockSpec((1, tq, N_HEADS * LANES), lambda b, s, qt, kt: (b, qt[s], 0)),
                  pl.BlockSpec((1, HW, tq), lambda b, s, qt, kt: (b, 0, kt[s])),
                  pl.BlockSpec((1, HW, tq), lambda b, s, qt, kt: (b, 0, kt[s])),
                  pl.BlockSpec((1, 8, tq), lambda b, s, qt, kt: (b, 0, kt[s])),
                  pl.BlockSpec((1, 8, tq), lambda b, s, qt, kt: (b, 0, qt[s]))],
        out_specs=pl.BlockSpec((1, tq, HW), lambda b, s, qt, kt: (b, qt[s], 0)),
        scratch_shapes=[pltpu.VMEM((N_HEADS, tq, 1), F32), pltpu.VMEM((N_HEADS, tq, 1), F32),
                        pltpu.VMEM((N_HEADS, tq, LANES), F32)],
    )
    return pl.pallas_call(
        functools.partial(_fox_kernel, tq=tq, tk=tq), grid_spec=grid_spec,
        out_shape=jax.ShapeDtypeStruct((nb, t, HW), F32),
        compiler_params=_params(("arbitrary", "arbitrary")), name="fox_prompt",
    )(qi_tab, ki_tab, qx, ktb, vtb, cs, cs)


def _fox_decode_kernel(pt_ref, qx_ref, kt_ref, vt_ref, lf_ref, kn_ref, vn_ref, lfn_ref, triu_ref, o_ref,
                       qbd_ref, m_ref, l_ref, acc_ref, carry_ref, *, n_pages):
    j = pl.program_id(1)

    @pl.when(j == 0)
    def _():
        row = _iota((8, LANES), 0)
        qx = qx_ref[0].astype(F32)
        for p in range(N_PAIRS):
            q0 = qx[:, (2 * p) * LANES:(2 * p + 1) * LANES]
            q1 = qx[:, (2 * p + 1) * LANES:(2 * p + 2) * LANES]
            qbd_ref[:, p * LANES:(p + 1) * LANES] = jnp.where(row == 2 * p, q0, jnp.where(row == 2 * p + 1, q1, 0.0))
        m_ref[...] = jnp.full_like(m_ref, NEG_INF)
        l_ref[...] = jnp.zeros_like(l_ref)
        acc_ref[...] = jnp.zeros_like(acc_ref)
        carry_ref[...] = jnp.zeros_like(carry_ref)

    cs = _dot_exact(lf_ref[0], triu_ref[...]) + carry_ref[:, 0:1]
    s = _dot(qbd_ref[...], kt_ref[0]) - cs
    m_prev = m_ref[:, 0:1]
    m_new = jnp.maximum(m_prev, jnp.max(s, axis=-1, keepdims=True))
    alpha = jnp.exp(m_prev - m_new)
    pr = jnp.exp(s - m_new)
    l_new = alpha * l_ref[:, 0:1] + jnp.sum(pr, axis=-1, keepdims=True)
    acc_new = alpha * acc_ref[...] + _dot_nt(pr, vt_ref[0])
    c_end = cs[:, PAGE_SIZE - 1:PAGE_SIZE]
    m_ref[...] = jnp.broadcast_to(m_new, m_ref.shape)
    l_ref[...] = jnp.broadcast_to(l_new, l_ref.shape)
    acc_ref[...] = acc_new
    carry_ref[...] = jnp.broadcast_to(c_end, carry_ref.shape)

    @pl.when(j == n_pages - 1)
    def _():
        s_n = jnp.sum(qbd_ref[...] * kn_ref[0], axis=-1, keepdims=True) - (c_end + lfn_ref[0][:, 0:1])
        m_f = jnp.maximum(m_new, s_n)
        a_f = jnp.exp(m_new - m_f)
        p_n = jnp.exp(s_n - m_f)
        l_f = a_f * l_new + p_n
        o8 = (a_f * acc_new + p_n * vn_ref[0]) / l_f
        row = _iota((8, HW), 0)
        col = _iota((8, HW), 1)
        own = (col >= row * HEAD_DIM) & (col < (row + 1) * HEAD_DIM)
        o_ref[0] = jnp.sum(jnp.where(own, o8, 0.0), axis=0, keepdims=True)


def _fox_decode(page_table, qx, cache_kt, cache_vt, cache_lf, k_new, v_new, lf_new):
    bd = qx.shape[0]
    n_pages = page_table.shape[1]
    pos = jnp.arange(PAGE_SIZE)
    triu = (pos[:, None] <= pos[None, :]).astype(F32)
    grid_spec = pltpu.PrefetchScalarGridSpec(
        num_scalar_prefetch=1, grid=(bd, n_pages),
        in_specs=[pl.BlockSpec((1, 1, N_HEADS * LANES), lambda b, j, pt: (b, 0, 0)),
                  pl.BlockSpec((1, HW, PAGE_SIZE), lambda b, j, pt: (pt[b, j], 0, 0)),
                  pl.BlockSpec((1, HW, PAGE_SIZE), lambda b, j, pt: (pt[b, j], 0, 0)),
                  pl.BlockSpec((1, 8, PAGE_SIZE), lambda b, j, pt: (pt[b, j], 0, 0)),
                  pl.BlockSpec((1, 1, HW), lambda b, j, pt: (b, 0, 0)),
                  pl.BlockSpec((1, 1, HW), lambda b, j, pt: (b, 0, 0)),
                  pl.BlockSpec((1, 8, LANES), lambda b, j, pt: (b, 0, 0)),
                  pl.BlockSpec((PAGE_SIZE, PAGE_SIZE), lambda b, j, pt: (0, 0))],
        out_specs=pl.BlockSpec((1, 1, HW), lambda b, j, pt: (b, 0, 0)),
        scratch_shapes=[pltpu.VMEM((8, HW), F32), pltpu.VMEM((8, LANES), F32), pltpu.VMEM((8, LANES), F32),
                        pltpu.VMEM((8, HW), F32), pltpu.VMEM((8, LANES), F32)],
    )
    return pl.pallas_call(
        functools.partial(_fox_decode_kernel, n_pages=n_pages), grid_spec=grid_spec,
        out_shape=jax.ShapeDtypeStruct((bd, 1, HW), F32),
        compiler_params=_params(("arbitrary", "arbitrary")), name="fox_decode",
    )(page_table, qx, cache_kt, cache_vt, cache_lf, k_new, v_new, lf_new, triu)


_FF_CHUNK = 704


def _outffn_kernel(x_ref, ya_ref, yb_ref, yc_ref, wo_ref, wg_ref, wu_ref, wd_ref, gpm_ref, gpf_ref, gqf_ref, o_ref):
    a0 = CONV_A_WIDTH
    a1 = a0 + HW
    mix = (_dot(ya_ref[...].astype(BF16), wo_ref[0:a0, :]) + _dot(yb_ref[...].astype(BF16), wo_ref[a0:a1, :])
           + _dot(yc_ref[...].astype(BF16), wo_ref[a1:a1 + HW, :]))
    x1 = x_ref[...] + _rms(mix, gpm_ref[...])
    hb = _rms(x1, gpf_ref[...]).astype(BF16)
    d_ff = wg_ref.shape[1]
    f = jnp.zeros_like(x1)
    for c0 in range(0, d_ff, _FF_CHUNK):
        gate = _dot(hb, wg_ref[:, c0:c0 + _FF_CHUNK])
        up = _dot(hb, wu_ref[:, c0:c0 + _FF_CHUNK])
        f = f + _dot((_silu(gate) * up).astype(BF16), wd_ref[c0:c0 + _FF_CHUNK, :])
    o_ref[...] = x1 + _rms(f, gqf_ref[...])


def _outffn(x, ya, yb, yc, wo, wg, wu, wd, gpm, gpf, gqf, *, tm):
    m, d = x.shape
    row = lambda i: (i, 0)
    const = lambda i: (0, 0)
    whole = lambda a: pl.BlockSpec(a.shape, const, pipeline_mode=pl.Buffered(1))
    return pl.pallas_call(
        _outffn_kernel, grid=(m // tm,),
        in_specs=[pl.BlockSpec((tm, d), row), pl.BlockSpec((tm, CONV_A_WIDTH), row),
                  pl.BlockSpec((tm, HW), row), pl.BlockSpec((tm, HW), row),
                  whole(wo), whole(wg), whole(wu), whole(wd),
                  pl.BlockSpec((1, d), const), pl.BlockSpec((1, d), const), pl.BlockSpec((1, d), const)],
        out_specs=pl.BlockSpec((tm, d), row),
        out_shape=jax.ShapeDtypeStruct((m, d), F32),
        compiler_params=_params(("arbitrary",)), name="outffn",
    )(x, ya, yb, yc, wo, wg, wu, wd, gpm, gpf, gqf)


def _layer_weights(l, w_in, conv_a_w, conv_a_b, ln_a_g, ln_a_b, conv_b_w, a_log, dt_bias, norm_b_g, f_bias,
                   w_out, g_pre_mix, g_post_mix, g_pre_ffn, g_post_ffn, w_gate, w_up, w_down):
    wt = w_in[l].T
    wmain = jnp.concatenate([wt[0:2048], wt[2060:2444]], axis=0).astype(BF16)
    wkv = wt[2444:3212].astype(BF16)
    z2 = jnp.zeros((2, wt.shape[1]), F32)
    wsm = jnp.concatenate([wt[3212:3218], z2, wt[2048:2054], z2, wt[2054:2060], z2,
                           jnp.zeros((LANES - 24, wt.shape[1]), F32)], axis=0).astype(BF16)
    z2v = jnp.zeros((2,), F32)
    z8v = jnp.zeros((8,), F32)
    add24 = jnp.concatenate([f_bias[l], z2v, z8v, dt_bias[l], z2v])
    alog24 = jnp.concatenate([z8v, z8v, a_log[l], z2v])
    pad_row = lambda v: jnp.concatenate([v, jnp.zeros((LANES - 24,), F32)])[None, :]
    return dict(
        g_pre_mix=g_pre_mix[l][None, :], wmain=wmain, wkv=wkv, wsm=wsm,
        addc=add24[:, None], alogc=alog24[:, None], addr=pad_row(add24), alogr=pad_row(alog24),
        conv_a_w=jnp.concatenate([conv_a_w[l], jnp.zeros((1, CONV_A_WIDTH), F32)], axis=0),
        conv_a_b=conv_a_b[l][None, :], ln_a_g=ln_a_g[l][None, :], ln_a_b=ln_a_b[l][None, :],
        conv_b_w=conv_b_w[l], norm_b_g=jnp.tile(norm_b_g[l], N_HEADS)[None, :],
        w_out=w_out[l].astype(BF16), w_gate=w_gate[l].astype(BF16), w_up=w_up[l].astype(BF16),
        w_down=w_down[l].astype(BF16),
        g_post_mix=g_post_mix[l][None, :], g_pre_ffn=g_pre_ffn[l][None, :], g_post_ffn=g_post_ffn[l][None, :],
    )


def _state_to_pairs(s):
    nb = s.shape[0]
    s = s.reshape(nb, N_PAIRS, 2, HEAD_DIM, HEAD_DIM)
    z = jnp.zeros_like(s[:, :, 0])
    top = jnp.concatenate([s[:, :, 0], z], axis=-1)
    bot = jnp.concatenate([z, s[:, :, 1]], axis=-1)
    return jnp.concatenate([top, bot], axis=-2)


def _pairs_to_state(sp):
    nb = sp.shape[0]
    s0 = sp[:, :, :HEAD_DIM, :HEAD_DIM]
    s1 = sp[:, :, HEAD_DIM:, HEAD_DIM:]
    return jnp.stack([s0, s1], axis=2).reshape(nb, N_HEADS, HEAD_DIM, HEAD_DIM)


def _pad_rows_front(buf, rows):
    nb, r, c = buf.shape
    return jnp.concatenate([jnp.zeros((nb, rows - r, c), buf.dtype), buf], axis=1)


def _prompt_layer(x, p, *, tm, tq, tt_a, tt_b):
    nb, t, d = x.shape
    xf = x.reshape(nb * t, d)
    u, qkvb, zb, qx, kt, vt, ktb, vtb, smt, smr = _inproj(
        xf, p["g_pre_mix"], p["wmain"], p["wkv"], p["wsm"], p["addc"], p["alogc"], p["addr"], p["alogr"],
        nb=nb, tm=tm)
    ya, nbuf_a = _conv_a(u.reshape(nb, t, -1), jnp.zeros((nb, _CA_PAD, CONV_A_WIDTH), F32), p["conv_a_w"],
                         p["conv_a_b"], p["ln_a_g"], p["ln_a_b"], tt=tt_a)
    yb, nbuf_b, s_new = _delta(qkvb.reshape(nb, t, -1), jnp.zeros((nb, _CB_PAD, 3 * HW), F32), p["conv_b_w"],
                               smr.reshape(nb, t, LANES), smt, zb.reshape(nb, t, HW), p["norm_b_g"],
                               jnp.zeros((nb, N_PAIRS, LANES, LANES), F32), tt=tt_b)
    cs = _logf_cumsum(smt, tl=tq)
    yc = _fox_prompt(qx.reshape(nb, t, -1), ktb, vtb, cs, tq=tq)
    y = _outffn(xf, ya.reshape(nb * t, -1), yb.reshape(nb * t, -1), yc.reshape(nb * t, -1), p["w_out"],
                p["w_gate"], p["w_up"], p["w_down"], p["g_post_mix"], p["g_pre_ffn"], p["g_post_ffn"], tm=tm)
    k_out = kt.reshape(nb, N_HEADS, HEAD_DIM, t).transpose(0, 3, 1, 2)
    v_out = vt.reshape(nb, N_HEADS, HEAD_DIM, t).transpose(0, 3, 1, 2)
    logf_out = smt[:, 0:N_HEADS, :].transpose(0, 2, 1)
    states = (k_out, v_out, logf_out, nbuf_a[:, _CA_PAD - (CONV_A_KERNEL - 1):],
              nbuf_b[:, _CB_PAD - (SHORT_CONV - 1):], _pairs_to_state(s_new))
    return y.reshape(nb, t, d), states


def _sample_layer(x, p, buf_a, buf_b, s_delta, cache_kt, cache_vt, cache_lf, page_table):
    bd, _, d = x.shape
    xf = x.reshape(bd, d)
    u, qkvb, zb, qx, kt, vt, _, _, smt, smr = _inproj(
        xf, p["g_pre_mix"], p["wmain"], p["wkv"], p["wsm"], p["addc"], p["alogc"], p["addr"], p["alogr"],
        nb=1, tm=bd)
    ya, nbuf_a = _conv_a(u.reshape(bd, 1, -1), _pad_rows_front(buf_a, _CA_PAD), p["conv_a_w"], p["conv_a_b"],
                         p["ln_a_g"], p["ln_a_b"], tt=1)

    c = DELTA_CHUNK
    pad_t = lambda a: jnp.concatenate([a[:, None, :], jnp.zeros((bd, c - 1, a.shape[-1]), a.dtype)], axis=1)
    smr_b = pad_t(smr)
    smt_b = jnp.transpose(smr_b[:, :, 0:24], (0, 2, 1))
    yb, nbuf_b, s_new = _delta(pad_t(qkvb), _pad_rows_front(buf_b, _CB_PAD), p["conv_b_w"], smr_b, smt_b,
                               pad_t(zb), p["norm_b_g"], _state_to_pairs(s_delta), tt=c)
    yb = yb[:, 0, :]
    new_buf_b = jnp.concatenate([buf_b[:, 1:], qkvb[:, None, :]], axis=1)

    k_new = kt[0].T
    v_new = vt[0].T
    lf_new = smt[0, 0:8, :].T
    yc = _fox_decode(page_table, qx[:, None, :], cache_kt, cache_vt, cache_lf, k_new[:, None, :],
                     v_new[:, None, :], jnp.broadcast_to(lf_new[:, :, None], (bd, 8, LANES)))
    y = _outffn(xf, ya.reshape(bd, -1), yb, yc.reshape(bd, -1), p["w_out"], p["w_gate"], p["w_up"], p["w_down"],
                p["g_post_mix"], p["g_pre_ffn"], p["g_post_ffn"], tm=bd)
    states = (k_new.reshape(bd, 1, N_HEADS, HEAD_DIM), v_new.reshape(bd, 1, N_HEADS, HEAD_DIM),
              lf_new[:, None, 0:N_HEADS], nbuf_a[:, _CA_PAD - (CONV_A_KERNEL - 1):], new_buf_b,
              _pairs_to_state(s_new))
    return y.reshape(bd, 1, d), states


def _forward(x_prompt, x_sample, cache_k, cache_v, cache_logf, page_table, state_conv_a, state_conv_b,
             state_delta, weights, *, tm, tq, tt_a, tt_b):
    depth = cache_k.shape[0]
    n_phys = cache_k.shape[1]
    xp, xs = x_prompt, x_sample
    prompt_states, sample_states = [], []
    for l in range(depth):
        p = _layer_weights(l, *weights)
        xp, sp = _prompt_layer(xp, p, tm=tm, tq=tq, tt_a=tt_a, tt_b=tt_b)
        prompt_states.append(sp)
        ckt = jnp.transpose(cache_k[l], (0, 2, 3, 1)).reshape(n_phys, HW, PAGE_SIZE)
        cvt = jnp.transpose(cache_v[l], (0, 2, 3, 1)).reshape(n_phys, HW, PAGE_SIZE)
        clf = jnp.transpose(cache_logf[l], (0, 2, 1))
        clf = jnp.concatenate([clf, jnp.zeros((n_phys, 8 - N_HEADS, PAGE_SIZE), F32)], axis=1)
        xs, ss = _sample_layer(xs, p, state_conv_a[l], state_conv_b[l], state_delta[l], ckt, cvt, clf, page_table)
        sample_states.append(ss)
    ps = [jnp.stack(t) for t in zip(*prompt_states)]
    ss = [jnp.stack(t) for t in zip(*sample_states)]
    return (xp, xs, *ps, *ss)


def kernel(x_prompt, x_sample, cache_k, cache_v, cache_logf, page_table, state_conv_a, state_conv_b, state_delta,
           w_in, conv_a_w, conv_a_b, ln_a_g, ln_a_b, conv_b_w, a_log, dt_bias, norm_b_g, f_bias, w_out, g_pre_mix,
           g_post_mix, g_pre_ffn, g_post_ffn, w_gate, w_up, w_down):
    weights = (w_in, conv_a_w, conv_a_b, ln_a_g, ln_a_b, conv_b_w, a_log, dt_bias, norm_b_g, f_bias, w_out,
               g_pre_mix, g_post_mix, g_pre_ffn, g_post_ffn, w_gate, w_up, w_down)
    return _forward(x_prompt, x_sample, cache_k, cache_v, cache_logf, page_table, state_conv_a, state_conv_b,
                    state_delta, weights, tm=512, tq=512, tt_a=512, tt_b=256)
```

```python
import functools

import jax
import jax.numpy as jnp
from jax import lax
from jax.experimental import pallas as pl
from jax.experimental.pallas import tpu as pltpu

F32 = jnp.float32
BF16 = jnp.bfloat16

HEAD_DIM = 64
N_HEADS = 6
N_PAIRS = N_HEADS // 2
HW = N_HEADS * HEAD_DIM
CONV_A_WIDTH = 256
CONV_A_KERNEL = 31
SHORT_CONV = 4
DELTA_CHUNK = 64
PAGE_SIZE = 128
LANES = 128
RMS_EPS = 1e-6
LN_EPS = 1e-5
VMEM_LIMIT_BYTES = 56 * 1024 * 1024
NEG_INF = float("-inf")

_NT = (((1,), (1,)), ((), ()))


def _dot(a, b):
    return jnp.dot(a, b, preferred_element_type=F32)


def _dot_nt(a, b):
    return lax.dot_general(a, b, _NT, preferred_element_type=F32)


def _dot_exact(a, b):
    return jnp.dot(a, b, preferred_element_type=F32, precision=lax.Precision.HIGHEST)


def _sigmoid(x):
    return 1.0 / (1.0 + jnp.exp(-x))


def _softplus(x):
    return jnp.maximum(x, 0.0) + jnp.log(1.0 + jnp.exp(-jnp.abs(x)))


def _silu(x):
    return x * _sigmoid(x)


def _rms(x, g):
    return x * lax.rsqrt(jnp.mean(x * x, axis=-1, keepdims=True) + RMS_EPS) * g


def _params(sem):
    return pltpu.CompilerParams(dimension_semantics=sem, vmem_limit_bytes=VMEM_LIMIT_BYTES)


def _iota(shape, dim):
    return lax.broadcasted_iota(jnp.int32, shape, dim)


def _split_hi_lo(x):
    hi = x.astype(BF16)
    lo = (x - hi.astype(F32)).astype(BF16)
    return hi, lo


def _group_sum(x2, gmat):
    hi, lo = _split_hi_lo(x2)
    return _dot(hi, gmat) + _dot(lo, gmat)


_MAIN_SPLITS = (0, 256, 512, 1664, 2048, 2432)


def _gate_fns(raw, add, a_log):
    x = raw + add
    logf = -_softplus(-x)
    beta = _sigmoid(x)
    g = -jnp.exp(a_log) * _softplus(x)
    return logf, beta, g


_VX_ROWS = 80
_BIAS_ROWS = 3


def _inproj_kernel(x_ref, g_ref, wmain_ref, wq_ref, wkv_ref, wsm_ref, addc_ref, alogc_ref, addr_ref, alogr_ref,
                   u_ref, qkvb_ref, zb_ref, *out_refs, decode):
    hb = _rms(x_ref[...], g_ref[...]).astype(BF16)
    tm = hb.shape[0]

    def proj(i):
        return _dot_nt(hb, wmain_ref[_MAIN_SPLITS[i]:_MAIN_SPLITS[i + 1], :])

    u_ref[...] = proj(0) * _sigmoid(proj(1))
    qkvb_ref[...] = proj(2)
    zb_ref[...] = proj(3)
    kv = _dot_nt(wkv_ref[...], hb)
    if decode:
        q_ref, kt_ref, vt_ref, smt_ref, smr_ref = out_refs
        q_ref[...] = _dot_nt(hb, wq_ref[...]) * (HEAD_DIM ** -0.5)
    else:
        kr_ref, qxt_ref, vx_ref, kt_ref, vt_ref, smt_ref, smr_ref = out_refs
        kr_ref[...] = proj(4).astype(BF16)
        qt = _dot_nt(wq_ref[...], hb) * (HEAD_DIM ** -0.5)
        ones_rows = jnp.where(_iota((LANES - HEAD_DIM, tm), 0) < _BIAS_ROWS, 1.0, 0.0).astype(BF16)
        for h in range(N_HEADS):
            qxt_ref[0, h * LANES:h * LANES + HEAD_DIM, :] = qt[h * HEAD_DIM:(h + 1) * HEAD_DIM].astype(BF16)
            qxt_ref[0, h * LANES + HEAD_DIM:(h + 1) * LANES, :] = ones_rows
            vx_ref[0, h * _VX_ROWS:h * _VX_ROWS + HEAD_DIM, :] = kv[HW + h * HEAD_DIM:HW + (h + 1) * HEAD_DIM].astype(BF16)
            vx_ref[0, h * _VX_ROWS + HEAD_DIM:(h + 1) * _VX_ROWS, :] = jnp.ones((_VX_ROWS - HEAD_DIM, tm), BF16)
    kt_ref[0] = kv[:HW]
    vt_ref[0] = kv[HW:]

    rawt = _dot_nt(wsm_ref[0:24, :], hb)
    logf, beta, g = _gate_fns(rawt, addc_ref[...], alogc_ref[...])
    smt_ref[0, 0:8] = logf[0:8]
    smt_ref[0, 8:16] = beta[8:16]
    smt_ref[0, 16:24] = g[16:24]

    rawr = _dot_nt(hb, wsm_ref[...])
    logf, beta, g = _gate_fns(rawr, addr_ref[...], alogr_ref[...])
    col = _iota((1, LANES), 1)
    smr_ref[...] = jnp.where(col < 8, logf, jnp.where(col < 16, beta, g))


def _inproj(x, g, wmain, wq, wkv, wsm, addc, alogc, addr, alogr, *, nb, tm, decode):
    m, d = x.shape
    t = m // nb
    nt = t // tm
    row = lambda i: (i, 0)
    const = lambda i: (0, 0)
    tr = lambda i: (i // nt, 0, i % nt)
    outs = [jax.ShapeDtypeStruct((m, CONV_A_WIDTH), F32), jax.ShapeDtypeStruct((m, 3 * HW), F32),
            jax.ShapeDtypeStruct((m, HW), F32)]
    out_specs = [pl.BlockSpec((tm, CONV_A_WIDTH), row), pl.BlockSpec((tm, 3 * HW), row), pl.BlockSpec((tm, HW), row)]
    if decode:
        outs += [jax.ShapeDtypeStruct((m, HW), F32)]
        out_specs += [pl.BlockSpec((tm, HW), row)]
    else:
        outs += [jax.ShapeDtypeStruct((m, HW), BF16), jax.ShapeDtypeStruct((nb, N_HEADS * LANES, t), BF16),
                 jax.ShapeDtypeStruct((nb, N_HEADS * _VX_ROWS, t), BF16)]
        out_specs += [pl.BlockSpec((tm, HW), row), pl.BlockSpec((1, N_HEADS * LANES, tm), tr),
                      pl.BlockSpec((1, N_HEADS * _VX_ROWS, tm), tr)]
    outs += [jax.ShapeDtypeStruct((nb, HW, t), F32), jax.ShapeDtypeStruct((nb, HW, t), F32),
             jax.ShapeDtypeStruct((nb, 24, t), F32), jax.ShapeDtypeStruct((m, LANES), F32)]
    out_specs += [pl.BlockSpec((1, HW, tm), tr), pl.BlockSpec((1, HW, tm), tr),
                  pl.BlockSpec((1, 24, tm), tr), pl.BlockSpec((tm, LANES), row)]
    in_specs = [
        pl.BlockSpec((tm, d), row), pl.BlockSpec((1, d), const),
        pl.BlockSpec(wmain.shape, const), pl.BlockSpec(wq.shape, const), pl.BlockSpec(wkv.shape, const),
        pl.BlockSpec(wsm.shape, const),
        pl.BlockSpec((24, 1), const), pl.BlockSpec((24, 1), const),
        pl.BlockSpec((1, LANES), const), pl.BlockSpec((1, LANES), const),
    ]
    return pl.pallas_call(
        functools.partial(_inproj_kernel, decode=decode), grid=(m // tm,), in_specs=in_specs, out_specs=out_specs,
        out_shape=outs, compiler_params=_params(("arbitrary",)), name="inproj",
    )(x, g, wmain, wq, wkv, wsm, addc, alogc, addr, alogr)


_CA_PAD = 32
_CA_ROWS = 64


def _conv_a_kernel(u_ref, buf_ref, w_ref, b_ref, lg_ref, lb_ref, y_ref, nbuf_ref, ext_ref, *, tt):
    ti = pl.program_id(1)

    @pl.when(ti == 0)
    def _():
        ext_ref[0:_CA_PAD] = buf_ref[0]

    ext_ref[_CA_PAD:_CA_PAD + tt] = u_ref[0]
    off = _CA_PAD - (CONV_A_KERNEL - 1)
    rows = min(_CA_ROWS, tt)
    for c in range(tt // rows):
        acc = jnp.zeros((rows, CONV_A_WIDTH), F32)
        for j in range(CONV_A_KERNEL):
            acc = acc + w_ref[j:j + 1, :] * ext_ref[off + j + c * rows:off + j + c * rows + rows, :]
        ca = acc + b_ref[...]
        mu = jnp.mean(ca, axis=-1, keepdims=True)
        xc = ca - mu
        var = jnp.mean(xc * xc, axis=-1, keepdims=True)
        y_ref[0, c * rows:(c + 1) * rows, :] = _silu(xc * lax.rsqrt(var + LN_EPS) * lg_ref[...] + lb_ref[...])
    tail = ext_ref[tt:tt + _CA_PAD]
    nbuf_ref[0] = tail
    ext_ref[0:_CA_PAD] = tail


def _conv_a(u, buf, w, b, lg, lb, *, tt):
    nb, t, c = u.shape
    const = lambda bi, ti: (0, 0)
    return pl.pallas_call(
        functools.partial(_conv_a_kernel, tt=tt),
        grid=(nb, t // tt),
        in_specs=[pl.BlockSpec((1, tt, c), lambda bi, ti: (bi, ti, 0)),
                  pl.BlockSpec((1, _CA_PAD, c), lambda bi, ti: (bi, 0, 0)),
                  pl.BlockSpec((_CA_PAD, c), const), pl.BlockSpec((1, c), const),
                  pl.BlockSpec((1, c), const), pl.BlockSpec((1, c), const)],
        out_specs=[pl.BlockSpec((1, tt, c), lambda bi, ti: (bi, ti, 0)),
                   pl.BlockSpec((1, _CA_PAD, c), lambda bi, ti: (bi, 0, 0))],
        out_shape=[jax.ShapeDtypeStruct((nb, t, c), F32), jax.ShapeDtypeStruct((nb, _CA_PAD, c), F32)],
        scratch_shapes=[pltpu.VMEM((_CA_PAD + tt, c), F32)],
        compiler_params=_params(("arbitrary", "arbitrary")), name="conv_a",
    )(u, buf, w, b, lg, lb)


_CB_PAD = 8


def _pair_cols(x, p):
    return x[:, p * LANES:(p + 1) * LANES]


def _head_cols(sm, base, p, lane_lo):
    c0 = sm[:, base + 2 * p:base + 2 * p + 1]
    c1 = sm[:, base + 2 * p + 1:base + 2 * p + 2]
    return jnp.where(lane_lo, c0, c1)


def _block_diag(x, lane_lo):
    return jnp.concatenate([jnp.where(lane_lo, x, 0.0), jnp.where(lane_lo, 0.0, x)], axis=0)


def _delta_kernel(x_ref, buf_ref, w_ref, smr_ref, smt_ref, z_ref, gb_ref, s0_ref, gmat_ref, tril_ref, triu_ref,
                  y_ref, nbuf_ref, sout_ref, ext_ref, s_ref, *, tt):
    ti = pl.program_id(1)
    nchunk = tt // DELTA_CHUNK
    c = DELTA_CHUNK

    @pl.when(ti == 0)
    def _():
        ext_ref[0:_CB_PAD] = buf_ref[0]
        s_ref[...] = s0_ref[0]

    ext_ref[_CB_PAD:_CB_PAD + tt] = x_ref[0]
    off = _CB_PAD - (SHORT_CONV - 1)
    acc = w_ref[0:1, :] * ext_ref[off:off + tt, :]
    for j in range(1, SHORT_CONV):
        acc = acc + w_ref[j:j + 1, :] * ext_ref[off + j:off + j + tt, :]
    tail = ext_ref[tt:tt + _CB_PAD]
    nbuf_ref[0] = tail
    ext_ref[0:_CB_PAD] = tail
    cb = _silu(acc)

    gmat = gmat_ref[...]
    q = cb[:, 0:HW]
    k = cb[:, HW:2 * HW]
    v = cb[:, 2 * HW:3 * HW]
    qs = q * lax.rsqrt(_group_sum(q * q, gmat) + 1e-6) * (HEAD_DIM ** -0.5)
    kn = k * lax.rsqrt(_group_sum(k * k, gmat) + 1e-6)

    smr = smr_ref[0]
    smt = smt_ref[0]
    gc_col = _dot_exact(tril_ref[...], smr)
    gc_row = _dot_exact(smt[16:24], triu_ref[...])
    if tt < LANES:
        gc_row = jnp.concatenate([gc_row, jnp.zeros((8, LANES - tt), F32)], axis=1)

    lane = _iota((1, LANES), 1)
    lane_lo = lane < HEAD_DIM
    ri = _iota((c, LANES), 0)
    ci_ = _iota((c, LANES), 1)
    cj = jnp.where(ci_ < HEAD_DIM, ci_, ci_ - HEAD_DIM)
    causal = ri >= cj
    strict = ri > cj
    eye2 = jnp.where(ri == cj, 1.0, 0.0)
    r128 = _iota((LANES, LANES), 0)
    c128 = _iota((LANES, LANES), 1)
    bdmask = (r128 < HEAD_DIM) == (c128 < HEAD_DIM)

    o_chunks = []
    for ci in range(nchunk):
        r0, r1 = ci * c, (ci + 1) * c
        o_pairs = []
        for p in range(N_PAIRS):
            qs_p = _pair_cols(qs, p)[r0:r1]
            kn_p = _pair_cols(kn, p)[r0:r1]
            v_p = _pair_cols(v, p)[r0:r1]
            beta = _head_cols(smr[r0:r1], 8, p, lane_lo)
            gi = _head_cols(gc_col[r0:r1], 16, p, lane_lo)
            blk = gc_row[:, (ci // 2) * LANES:(ci // 2 + 1) * LANES]
            rot = pltpu.roll(blk, HEAD_DIM, 1)
            if ci % 2 == 0:
                gj = jnp.where(lane_lo, blk[2 * p:2 * p + 1], rot[2 * p + 1:2 * p + 2])
            else:
                gj = jnp.where(lane_lo, rot[2 * p:2 * p + 1], blk[2 * p + 1:2 * p + 2])
            glast = gi[c - 1:c, :]
            eg = jnp.exp(gi)
            decay = jnp.exp(jnp.where(causal, gi - gj, NEG_INF))
            kb = kn_p * beta
            kd = _block_diag(kn_p, lane_lo)
            lmat = jnp.where(strict, _dot_nt(kb, kd) * decay, 0.0)
            x_inv = eye2 - lmat
            pw = lmat
            for _ in range(5):
                pw = _dot_exact(pw, _block_diag(pw, lane_lo))
                x_inv = x_inv + _dot_exact(x_inv, _block_diag(pw, lane_lo))
            u = _dot(x_inv, _block_diag(v_p * beta, lane_lo))
            w = _dot(x_inv, _block_diag(kb * eg, lane_lo))
            qk = jnp.where(causal, _dot_nt(qs_p, kd) * decay, 0.0)
            kdec_t = (kn_p * jnp.exp(glast - gi)).T
            kw_bd = jnp.where(bdmask, _dot(kdec_t, w), 0.0)
            ku_bd = jnp.where(bdmask, _dot(kdec_t, u), 0.0)
            q2 = qs_p * eg - _dot(qk, _block_diag(w, lane_lo))
            o2 = _dot(qk, _block_diag(u, lane_lo))
            s_bd = s_ref[p]
            o_pairs.append(_dot(q2, s_bd) + o2)
            s_ref[p] = s_bd * jnp.exp(glast) - _dot(kw_bd, s_bd) + ku_bd
        o_chunks.append(jnp.concatenate(o_pairs, axis=1))
    o = jnp.concatenate(o_chunks, axis=0) if nchunk > 1 else o_chunks[0]
    ms = _group_sum(o * o, gmat) * (1.0 / HEAD_DIM)
    y_ref[0] = o * lax.rsqrt(ms + RMS_EPS) * gb_ref[...] * _silu(z_ref[0])
    sout_ref[0] = s_ref[...]


def _delta(x, buf, w, smr, smt, z, gb, s0, *, tt):
    nb, t, cw = x.shape
    const = lambda bi, ti: (0, 0)
    head = jnp.arange(HW) // HEAD_DIM
    gmat = (head[:, None] == head[None, :]).astype(BF16)
    chunk = jnp.arange(tt) // DELTA_CHUNK
    same = chunk[:, None] == chunk[None, :]
    pos = jnp.arange(tt)
    tril = (same & (pos[:, None] >= pos[None, :])).astype(F32)
    triu = tril.T
    return pl.pallas_call(
        functools.partial(_delta_kernel, tt=tt),
        grid=(nb, t // tt),
        in_specs=[pl.BlockSpec((1, tt, cw), lambda bi, ti: (bi, ti, 0)),
                  pl.BlockSpec((1, _CB_PAD, cw), lambda bi, ti: (bi, 0, 0)),
                  pl.BlockSpec((SHORT_CONV, cw), const),
                  pl.BlockSpec((1, tt, LANES), lambda bi, ti: (bi, ti, 0)),
                  pl.BlockSpec((1, 24, tt), lambda bi, ti: (bi, 0, ti)),
                  pl.BlockSpec((1, tt, HW), lambda bi, ti: (bi, ti, 0)),
                  pl.BlockSpec((1, HW), const),
                  pl.BlockSpec((1, N_PAIRS, LANES, LANES), lambda bi, ti: (bi, 0, 0, 0)),
                  pl.BlockSpec((HW, HW), const), pl.BlockSpec((tt, tt), const), pl.BlockSpec((tt, tt), const)],
        out_specs=[pl.BlockSpec((1, tt, HW), lambda bi, ti: (bi, ti, 0)),
                   pl.BlockSpec((1, _CB_PAD, cw), lambda bi, ti: (bi, 0, 0)),
                   pl.BlockSpec((1, N_PAIRS, LANES, LANES), lambda bi, ti: (bi, 0, 0, 0))],
        out_shape=[jax.ShapeDtypeStruct((nb, t, HW), F32), jax.ShapeDtypeStruct((nb, _CB_PAD, cw), F32),
                   jax.ShapeDtypeStruct((nb, N_PAIRS, LANES, LANES), F32)],
        scratch_shapes=[pltpu.VMEM((_CB_PAD + tt, cw), F32), pltpu.VMEM((N_PAIRS, LANES, LANES), F32)],
        compiler_params=_params(("arbitrary", "arbitrary")), name="delta",
    )(x, buf, w, smr, smt, z, gb, s0, gmat, tril, triu)


def _fox_keys_kernel(smr_ref, kr_ref, tril_ref, pk_ref, pb_ref, kx_ref, carry_ref):
    ti = pl.program_id(1)

    @pl.when(ti == 0)
    def _():
        carry_ref[...] = jnp.zeros_like(carry_ref)

    cs = _dot_exact(tril_ref[...], smr_ref[0]) + carry_ref[0:1, :]
    tl = cs.shape[0]
    carry_ref[...] = jnp.broadcast_to(cs[tl - 1:tl, :], carry_ref.shape)
    neg = jnp.where(_iota((1, LANES), 1) < N_HEADS, -cs, 0.0)
    hi = neg.astype(BF16)
    r1 = neg - hi.astype(F32)
    mid = r1.astype(BF16)
    lo = (r1 - mid.astype(F32)).astype(BF16)
    kx = (_dot(kr_ref[0], pk_ref[...]) + _dot(hi, pb_ref[0]) + _dot(mid, pb_ref[1]) + _dot(lo, pb_ref[2]))
    kx_ref[0] = kx.astype(BF16)


def _fox_keys(smr, kr, *, tl):
    nb, t, _ = smr.shape
    pos = jnp.arange(tl)
    tril = (pos[:, None] >= pos[None, :]).astype(F32)
    src = jnp.arange(HW)
    dst = (src // HEAD_DIM) * LANES + src % HEAD_DIM
    pk = (dst[:, None] == jnp.arange(N_HEADS * LANES)[None, :]).astype(BF16)
    col = jnp.arange(LANES)
    pb = jnp.stack([((col[:, None] < N_HEADS)
                     & (col[:, None] * LANES + HEAD_DIM + piece == jnp.arange(N_HEADS * LANES)[None, :])).astype(BF16)
                    for piece in range(_BIAS_ROWS)])
    const2 = lambda bi, ti: (0, 0)
    return pl.pallas_call(
        _fox_keys_kernel, grid=(nb, t // tl),
        in_specs=[pl.BlockSpec((1, tl, LANES), lambda bi, ti: (bi, ti, 0)),
                  pl.BlockSpec((1, tl, HW), lambda bi, ti: (bi, ti, 0)),
                  pl.BlockSpec((tl, tl), const2), pl.BlockSpec(pk.shape, const2),
                  pl.BlockSpec(pb.shape, lambda bi, ti: (0, 0, 0))],
        out_specs=pl.BlockSpec((1, tl, N_HEADS * LANES), lambda bi, ti: (bi, ti, 0)),
        out_shape=jax.ShapeDtypeStruct((nb, t, N_HEADS * LANES), BF16),
        scratch_shapes=[pltpu.VMEM((8, LANES), F32)],
        compiler_params=_params(("arbitrary", "arbitrary")), name="fox_keys",
    )(smr, kr, tril, pk, pb)


def _fox_kernel(qi_ref, ki_ref, kx_ref, qxt_ref, vx_ref, o_ref, m_ref, acc_ref, *, tq, tk):
    step = pl.program_id(1)
    qi = qi_ref[step]
    ki = ki_ref[step]

    @pl.when(ki == 0)
    def _():
        m_ref[...] = jnp.full_like(m_ref, NEG_INF)
        acc_ref[...] = jnp.zeros_like(acc_ref)

    def update(masked):
        if masked:
            keep = _iota((tk, tq), 0) <= _iota((tk, tq), 1)
        for h in range(N_HEADS):
            st = _dot(kx_ref[0, :, h * LANES:(h + 1) * LANES], qxt_ref[0, h * LANES:(h + 1) * LANES, :])
            if masked:
                st = jnp.where(keep, st, NEG_INF)
            m_prev = m_ref[h:h + 1, :]
            m_new = jnp.maximum(m_prev, jnp.max(st, axis=0, keepdims=True))
            alpha = jnp.exp(m_prev - m_new)
            pt = jnp.exp(st - m_new).astype(BF16)
            acc_ref[h] = alpha * acc_ref[h] + _dot(vx_ref[0, h * _VX_ROWS:(h + 1) * _VX_ROWS, :], pt)
            m_ref[h:h + 1, :] = m_new

    @pl.when(ki < qi)
    def _():
        update(False)

    @pl.when(ki == qi)
    def _():
        update(True)
        for h in range(N_HEADS):
            acc = acc_ref[h]
            o_ref[0, h * HEAD_DIM:(h + 1) * HEAD_DIM, :] = acc[0:HEAD_DIM] / acc[HEAD_DIM:HEAD_DIM + 1]


def _fox_prompt(kx, qxt, vx, *, tq):
    nb, t, _ = kx.shape
    nq = t // tq
    qi_tab = jnp.asarray([qi for qi in range(nq) for _ in range(qi + 1)], jnp.int32)
    ki_tab = jnp.asarray([ki for qi in range(nq) for ki in range(qi + 1)], jnp.int32)
    grid_spec = pltpu.PrefetchScalarGridSpec(
        num_scalar_prefetch=2, grid=(nb, int(qi_tab.shape[0])),
        in_specs=[pl.BlockSpec((1, tq, N_HEADS * LANES), lambda b, s, qt, kt: (b, kt[s], 0)),
                  pl.BlockSpec((1, N_HEADS * LANES, tq), lambda b, s, qt, kt: (b, 0, qt[s])),
                  pl.BlockSpec((1, N_HEADS * _VX_ROWS, tq), lambda b, s, qt, kt: (b, 0, kt[s]))],
        out_specs=pl.BlockSpec((1, HW, tq), lambda b, s, qt, kt: (b, 0, qt[s])),
        scratch_shapes=[pltpu.VMEM((8, tq), F32), pltpu.VMEM((N_HEADS, _VX_ROWS, tq), F32)],
    )
    return pl.pallas_call(
        functools.partial(_fox_kernel, tq=tq, tk=tq), grid_spec=grid_spec,
        out_shape=jax.ShapeDtypeStruct((nb, HW, t), F32),
        compiler_params=_params(("arbitrary", "arbitrary")), name="fox_prompt",
    )(qi_tab, ki_tab, kx, qxt, vx)


def _page_cumsum_kernel(x_ref, triu_ref, o_ref):
    o_ref[...] = _dot_exact(x_ref[...], triu_ref[...])


def _page_cumsum(lf, *, rows):
    r = lf.shape[0]
    pos = jnp.arange(PAGE_SIZE)
    triu = (pos[:, None] <= pos[None, :]).astype(F32)
    return pl.pallas_call(
        _page_cumsum_kernel, grid=(r // rows,),
        in_specs=[pl.BlockSpec((rows, PAGE_SIZE), lambda i: (i, 0)),
                  pl.BlockSpec((PAGE_SIZE, PAGE_SIZE), lambda i: (0, 0))],
        out_specs=pl.BlockSpec((rows, PAGE_SIZE), lambda i: (i, 0)),
        out_shape=jax.ShapeDtypeStruct((r, PAGE_SIZE), F32),
        compiler_params=_params(("arbitrary",)), name="page_cumsum",
    )(lf, triu)


def _fox_decode_kernel(pt_ref, qx_ref, *refs, n_steps, group):
    kt_refs = refs[0:group]
    vt_refs = refs[group:2 * group]
    lcs_refs = refs[2 * group:3 * group]
    kn_ref, vn_ref, lfn_ref, o_ref, qbd_ref, m_ref, l_ref, acc_ref, carry_ref = refs[3 * group:]
    j = pl.program_id(1)

    @pl.when(j == 0)
    def _():
        row = _iota((8, HW), 0)
        col = _iota((8, HW), 1)
        own = (col >= row * HEAD_DIM) & (col < (row + 1) * HEAD_DIM)
        qbd_ref[...] = jnp.where(own, qx_ref[0], 0.0)
        m_ref[...] = jnp.full_like(m_ref, NEG_INF)
        l_ref[...] = jnp.zeros_like(l_ref)
        acc_ref[...] = jnp.zeros_like(acc_ref)
        carry_ref[...] = jnp.zeros_like(carry_ref)

    qbd = qbd_ref[...]
    c_end = carry_ref[:, 0:1]
    scores = []
    for g in range(group):
        lcs = lcs_refs[g][0, 0]
        scores.append(_dot(qbd, kt_refs[g][0, 0]) - (lcs + c_end))
        c_end = c_end + lcs[:, PAGE_SIZE - 1:PAGE_SIZE]
    s = jnp.concatenate(scores, axis=1)
    m_prev = m_ref[:, 0:1]
    m_new = jnp.maximum(m_prev, jnp.max(s, axis=-1, keepdims=True))
    alpha = jnp.exp(m_prev - m_new)
    pr = jnp.exp(s - m_new)
    l_new = alpha * l_ref[:, 0:1] + jnp.sum(pr, axis=-1, keepdims=True)
    pv = _dot_nt(pr[:, 0:PAGE_SIZE], vt_refs[0][0, 0])
    for g in range(1, group):
        pv = pv + _dot_nt(pr[:, g * PAGE_SIZE:(g + 1) * PAGE_SIZE], vt_refs[g][0, 0])
    acc_new = alpha * acc_ref[...] + pv
    m_ref[...] = jnp.broadcast_to(m_new, m_ref.shape)
    l_ref[...] = jnp.broadcast_to(l_new, l_ref.shape)
    acc_ref[...] = acc_new
    carry_ref[...] = jnp.broadcast_to(c_end, carry_ref.shape)

    @pl.when(j == n_steps - 1)
    def _():
        s_n = jnp.sum(qbd_ref[...] * kn_ref[0], axis=-1, keepdims=True) - (c_end + lfn_ref[0][:, 0:1])
        m_f = jnp.maximum(m_new, s_n)
        a_f = jnp.exp(m_new - m_f)
        p_n = jnp.exp(s_n - m_f)
        l_f = a_f * l_new + p_n
        o8 = (a_f * acc_new + p_n * vn_ref[0]) / l_f
        row = _iota((8, HW), 0)
        col = _iota((8, HW), 1)
        own = (col >= row * HEAD_DIM) & (col < (row + 1) * HEAD_DIM)
        o_ref[0] = jnp.sum(jnp.where(own, o8, 0.0), axis=0, keepdims=True)


def _fox_decode(page_table, layer, qx, cache_kt, cache_vt, cache_lcs, k_new, v_new, lf_new, *, group):
    bd = qx.shape[0]
    n_pages = page_table.shape[1]
    n_steps = n_pages // group

    def page(g):
        return lambda b, j, pt: (layer, pt[b, j * group + g], 0, 0)

    per_b = lambda b, j, pt: (b, 0, 0)
    in_specs = [pl.BlockSpec((1, 1, HW), per_b)]
    in_specs += [pl.BlockSpec((1, 1, HW, PAGE_SIZE), page(g)) for g in range(group)]
    in_specs += [pl.BlockSpec((1, 1, HW, PAGE_SIZE), page(g)) for g in range(group)]
    in_specs += [pl.BlockSpec((1, 1, 8, PAGE_SIZE), page(g)) for g in range(group)]
    in_specs += [pl.BlockSpec((1, 1, HW), per_b), pl.BlockSpec((1, 1, HW), per_b), pl.BlockSpec((1, 8, LANES), per_b)]
    grid_spec = pltpu.PrefetchScalarGridSpec(
        num_scalar_prefetch=1, grid=(bd, n_steps), in_specs=in_specs,
        out_specs=pl.BlockSpec((1, 1, HW), per_b),
        scratch_shapes=[pltpu.VMEM((8, HW), F32), pltpu.VMEM((8, LANES), F32), pltpu.VMEM((8, LANES), F32),
                        pltpu.VMEM((8, HW), F32), pltpu.VMEM((8, LANES), F32)],
    )
    return pl.pallas_call(
        functools.partial(_fox_decode_kernel, n_steps=n_steps, group=group), grid_spec=grid_spec,
        out_shape=jax.ShapeDtypeStruct((bd, 1, HW), F32),
        compiler_params=_params(("arbitrary", "arbitrary")), name="fox_decode",
    )(page_table, qx, *([cache_kt] * group), *([cache_vt] * group), *([cache_lcs] * group), k_new, v_new, lf_new)


_FF_CHUNK = 704


def _outffn_kernel(x_ref, ya_ref, yb_ref, yc_ref, wo_ref, wg_ref, wu_ref, wd_ref, gpm_ref, gpf_ref, gqf_ref, o_ref,
                   *, yc_transposed):
    a0 = CONV_A_WIDTH
    a1 = a0 + HW
    yc = yc_ref[0].T if yc_transposed else yc_ref[...]
    mix = (_dot(ya_ref[...].astype(BF16), wo_ref[0:a0, :]) + _dot(yb_ref[...].astype(BF16), wo_ref[a0:a1, :])
           + _dot(yc.astype(BF16), wo_ref[a1:a1 + HW, :]))
    x1 = x_ref[...] + _rms(mix, gpm_ref[...])
    hb = _rms(x1, gpf_ref[...]).astype(BF16)
    d_ff = wg_ref.shape[1]
    f = jnp.zeros_like(x1)
    for c0 in range(0, d_ff, _FF_CHUNK):
        gate = _dot(hb, wg_ref[:, c0:c0 + _FF_CHUNK])
        up = _dot(hb, wu_ref[:, c0:c0 + _FF_CHUNK])
        f = f + _dot((_silu(gate) * up).astype(BF16), wd_ref[c0:c0 + _FF_CHUNK, :])
    o_ref[...] = x1 + _rms(f, gqf_ref[...])


def _outffn(x, ya, yb, yc, wo, wg, wu, wd, gpm, gpf, gqf, *, tm):
    m, d = x.shape
    row = lambda i: (i, 0)
    const = lambda i: (0, 0)
    whole = lambda a: pl.BlockSpec(a.shape, const, pipeline_mode=pl.Buffered(1))
    yc_transposed = yc.ndim == 3
    if yc_transposed:
        nt = yc.shape[2] // tm
        yc_spec = pl.BlockSpec((1, HW, tm), lambda i: (i // nt, 0, i % nt))
    else:
        yc_spec = pl.BlockSpec((tm, HW), row)
    return pl.pallas_call(
        functools.partial(_outffn_kernel, yc_transposed=yc_transposed), grid=(m // tm,),
        in_specs=[pl.BlockSpec((tm, d), row), pl.BlockSpec((tm, CONV_A_WIDTH), row),
                  pl.BlockSpec((tm, HW), row), yc_spec,
                  whole(wo), whole(wg), whole(wu), whole(wd),
                  pl.BlockSpec((1, d), const), pl.BlockSpec((1, d), const), pl.BlockSpec((1, d), const)],
        out_specs=pl.BlockSpec((tm, d), row),
        out_shape=jax.ShapeDtypeStruct((m, d), F32),
        compiler_params=_params(("arbitrary",)), name="outffn",
    )(x, ya, yb, yc, wo, wg, wu, wd, gpm, gpf, gqf)


def _layer_weights(l, w_in, conv_a_w, conv_a_b, ln_a_g, ln_a_b, conv_b_w, a_log, dt_bias, norm_b_g, f_bias,
                   w_out, g_pre_mix, g_post_mix, g_pre_ffn, g_post_ffn, w_gate, w_up, w_down):
    wt = w_in[l].T
    wmain = jnp.concatenate([wt[0:2048], wt[2444:2828]], axis=0).astype(BF16)
    wq = wt[2060:2444].astype(BF16)
    wkv = wt[2444:3212].astype(BF16)
    z2 = jnp.zeros((2, wt.shape[1]), F32)
    wsm = jnp.concatenate([wt[3212:3218], z2, wt[2048:2054], z2, wt[2054:2060], z2,
                           jnp.zeros((LANES - 24, wt.shape[1]), F32)], axis=0).astype(BF16)
    z2v = jnp.zeros((2,), F32)
    z8v = jnp.zeros((8,), F32)
    add24 = jnp.concatenate([f_bias[l], z2v, z8v, dt_bias[l], z2v])
    alog24 = jnp.concatenate([z8v, z8v, a_log[l], z2v])
    pad_row = lambda v: jnp.concatenate([v, jnp.zeros((LANES - 24,), F32)])[None, :]
    return dict(
        g_pre_mix=g_pre_mix[l][None, :], wmain=wmain, wq=wq, wkv=wkv, wsm=wsm,
        addc=add24[:, None], alogc=alog24[:, None], addr=pad_row(add24), alogr=pad_row(alog24),
        conv_a_w=jnp.concatenate([conv_a_w[l], jnp.zeros((1, CONV_A_WIDTH), F32)], axis=0),
        conv_a_b=conv_a_b[l][None, :], ln_a_g=ln_a_g[l][None, :], ln_a_b=ln_a_b[l][None, :],
        conv_b_w=conv_b_w[l], norm_b_g=jnp.tile(norm_b_g[l], N_HEADS)[None, :],
        w_out=w_out[l].astype(BF16), w_gate=w_gate[l].astype(BF16), w_up=w_up[l].astype(BF16),
        w_down=w_down[l].astype(BF16),
        g_post_mix=g_post_mix[l][None, :], g_pre_ffn=g_pre_ffn[l][None, :], g_post_ffn=g_post_ffn[l][None, :],
    )


def _state_to_pairs(s):
    nb = s.shape[0]
    s = s.reshape(nb, N_PAIRS, 2, HEAD_DIM, HEAD_DIM)
    z = jnp.zeros_like(s[:, :, 0])
    top = jnp.concatenate([s[:, :, 0], z], axis=-1)
    bot = jnp.concatenate([z, s[:, :, 1]], axis=-1)
    return jnp.concatenate([top, bot], axis=-2)


def _pairs_to_state(sp):
    nb = sp.shape[0]
    s0 = sp[:, :, :HEAD_DIM, :HEAD_DIM]
    s1 = sp[:, :, HEAD_DIM:, HEAD_DIM:]
    return jnp.stack([s0, s1], axis=2).reshape(nb, N_HEADS, HEAD_DIM, HEAD_DIM)


def _pad_rows_front(buf, rows):
    nb, r, c = buf.shape
    return jnp.concatenate([jnp.zeros((nb, rows - r, c), buf.dtype), buf], axis=1)


def _prompt_layer(x, p, *, tm, tq, tt_a, tt_b):
    nb, t, d = x.shape
    xf = x.reshape(nb * t, d)
    u, qkvb, zb, kr, qxt, vx, kt, vt, smt, smr = _inproj(
        xf, p["g_pre_mix"], p["wmain"], p["wq"], p["wkv"], p["wsm"], p["addc"], p["alogc"], p["addr"], p["alogr"],
        nb=nb, tm=tm, decode=False)
    ya, nbuf_a = _conv_a(u.reshape(nb, t, -1), jnp.zeros((nb, _CA_PAD, CONV_A_WIDTH), F32), p["conv_a_w"],
                         p["conv_a_b"], p["ln_a_g"], p["ln_a_b"], tt=tt_a)
    yb, nbuf_b, s_new = _delta(qkvb.reshape(nb, t, -1), jnp.zeros((nb, _CB_PAD, 3 * HW), F32), p["conv_b_w"],
                               smr.reshape(nb, t, LANES), smt, zb.reshape(nb, t, HW), p["norm_b_g"],
                               jnp.zeros((nb, N_PAIRS, LANES, LANES), F32), tt=tt_b)
    kx = _fox_keys(smr.reshape(nb, t, LANES), kr.reshape(nb, t, HW), tl=tq)
    yc = _fox_prompt(kx, qxt, vx, tq=tq)
    y = _outffn(xf, ya.reshape(nb * t, -1), yb.reshape(nb * t, -1), yc, p["w_out"],
                p["w_gate"], p["w_up"], p["w_down"], p["g_post_mix"], p["g_pre_ffn"], p["g_post_ffn"], tm=tm)
    k_out = kt.reshape(nb, N_HEADS, HEAD_DIM, t).transpose(0, 3, 1, 2)
    v_out = vt.reshape(nb, N_HEADS, HEAD_DIM, t).transpose(0, 3, 1, 2)
    logf_out = smt[:, 0:N_HEADS, :].transpose(0, 2, 1)
    states = (k_out, v_out, logf_out, nbuf_a[:, _CA_PAD - (CONV_A_KERNEL - 1):],
              nbuf_b[:, _CB_PAD - (SHORT_CONV - 1):], _pairs_to_state(s_new))
    return y.reshape(nb, t, d), states


def _sample_layer(x, p, layer, buf_a, buf_b, s_delta, cache_kt, cache_vt, cache_lcs, page_table, *, group):
    bd, _, d = x.shape
    xf = x.reshape(bd, d)
    u, qkvb, zb, qx, kt, vt, smt, smr = _inproj(
        xf, p["g_pre_mix"], p["wmain"], p["wq"], p["wkv"], p["wsm"], p["addc"], p["alogc"], p["addr"], p["alogr"],
        nb=1, tm=bd, decode=True)
    ya, nbuf_a = _conv_a(u.reshape(bd, 1, -1), _pad_rows_front(buf_a, _CA_PAD), p["conv_a_w"], p["conv_a_b"],
                         p["ln_a_g"], p["ln_a_b"], tt=1)

    c = DELTA_CHUNK
    pad_t = lambda a: jnp.concatenate([a[:, None, :], jnp.zeros((bd, c - 1, a.shape[-1]), a.dtype)], axis=1)
    smr_b = pad_t(smr)
    smt_b = jnp.transpose(smr_b[:, :, 0:24], (0, 2, 1))
    yb, nbuf_b, s_new = _delta(pad_t(qkvb), _pad_rows_front(buf_b, _CB_PAD), p["conv_b_w"], smr_b, smt_b,
                               pad_t(zb), p["norm_b_g"], _state_to_pairs(s_delta), tt=c)
    yb = yb[:, 0, :]
    new_buf_b = jnp.concatenate([buf_b[:, 1:], qkvb[:, None, :]], axis=1)

    k_new = kt[0].T
    v_new = vt[0].T
    lf_new = smt[0, 0:8, :].T
    yc = _fox_decode(page_table, layer, qx[:, None, :], cache_kt, cache_vt, cache_lcs, k_new[:, None, :],
                     v_new[:, None, :], jnp.broadcast_to(lf_new[:, :, None], (bd, 8, LANES)), group=group)
    y = _outffn(xf, ya.reshape(bd, -1), yb, yc.reshape(bd, -1), p["w_out"], p["w_gate"], p["w_up"], p["w_down"],
                p["g_post_mix"], p["g_pre_ffn"], p["g_post_ffn"], tm=bd)
    states = (k_new.reshape(bd, 1, N_HEADS, HEAD_DIM), v_new.reshape(bd, 1, N_HEADS, HEAD_DIM),
              lf_new[:, None, 0:N_HEADS], nbuf_a[:, _CA_PAD - (CONV_A_KERNEL - 1):], new_buf_b,
              _pairs_to_state(s_new))
    return y.reshape(bd, 1, d), states


def _forward(x_prompt, x_sample, cache_k, cache_v, cache_logf, page_table, state_conv_a, state_conv_b,
             state_delta, weights, *, tm, tq, tt_a, tt_b, group):
    depth = cache_k.shape[0]
    n_phys = cache_k.shape[1]
    ckt = jnp.transpose(cache_k, (0, 1, 3, 4, 2)).reshape(depth, n_phys, HW, PAGE_SIZE)
    cvt = jnp.transpose(cache_v, (0, 1, 3, 4, 2)).reshape(depth, n_phys, HW, PAGE_SIZE)
    clf = jnp.transpose(cache_logf, (0, 1, 3, 2))
    clf = jnp.concatenate([clf, jnp.zeros((depth, n_phys, 8 - N_HEADS, PAGE_SIZE), F32)], axis=2)
    n_rows = depth * n_phys * 8
    rows = max(r for r in range(8, min(n_rows, 2048) + 1, 8) if n_rows % r == 0)
    clcs = _page_cumsum(clf.reshape(n_rows, PAGE_SIZE), rows=rows).reshape(depth, n_phys, 8, PAGE_SIZE)
    xp, xs = x_prompt, x_sample
    prompt_states, sample_states = [], []
    for l in range(depth):
        p = _layer_weights(l, *weights)
        xp, sp = _prompt_layer(xp, p, tm=tm, tq=tq, tt_a=tt_a, tt_b=tt_b)
        prompt_states.append(sp)
        xs, ss = _sample_layer(xs, p, l, state_conv_a[l], state_conv_b[l], state_delta[l], ckt, cvt, clcs,
                               page_table, group=group)
        sample_states.append(ss)
    ps = [jnp.stack(t) for t in zip(*prompt_states)]
    ss = [jnp.stack(t) for t in zip(*sample_states)]
    return (xp, xs, *ps, *ss)


def kernel(x_prompt, x_sample, cache_k, cache_v, cache_logf, page_table, state_conv_a, state_conv_b, state_delta,
           w_in, conv_a_w, conv_a_b, ln_a_g, ln_a_b, conv_b_w, a_log, dt_bias, norm_b_g, f_bias, w_out, g_pre_mix,
           g_post_mix, g_pre_ffn, g_post_ffn, w_gate, w_up, w_down):
    weights = (w_in, conv_a_w, conv_a_b, ln_a_g, ln_a_b, conv_b_w, a_log, dt_bias, norm_b_g, f_bias, w_out,
               g_pre_mix, g_post_mix, g_pre_ffn, g_post_ffn, w_gate, w_up, w_down)
    return _forward(x_prompt, x_sample, cache_k, cache_v, cache_logf, page_table, state_conv_a, state_conv_b,
                    state_delta, weights, tm=512, tq=512, tt_a=512, tt_b=256, group=16)
```

```python
import functools

import jax
import jax.numpy as jnp
from jax import lax
from jax.experimental import pallas as pl
from jax.experimental.pallas import tpu as pltpu

F32 = jnp.float32
BF16 = jnp.bfloat16

HEAD_DIM = 64
N_HEADS = 6
N_PAIRS = N_HEADS // 2
HW = N_HEADS * HEAD_DIM
CONV_A_WIDTH = 256
CONV_A_KERNEL = 31
SHORT_CONV = 4
DELTA_CHUNK = 64
PAGE_SIZE = 128
LANES = 128
RMS_EPS = 1e-6
LN_EPS = 1e-5
VMEM_LIMIT_BYTES = 56 * 1024 * 1024
NEG_INF = float("-inf")
LOG2E = 1.4426950408889634

_NT = (((1,), (1,)), ((), ()))


def _dot(a, b):
    return jnp.dot(a, b, preferred_element_type=F32)


def _dot_nt(a, b):
    return lax.dot_general(a, b, _NT, preferred_element_type=F32)


def _dot_exact(a, b):
    return jnp.dot(a, b, preferred_element_type=F32, precision=lax.Precision.HIGHEST)


def _sigmoid(x):
    return 1.0 / (1.0 + jnp.exp(-x))


def _softplus(x):
    return jnp.maximum(x, 0.0) + jnp.log(1.0 + jnp.exp(-jnp.abs(x)))


def _silu(x):
    return x * _sigmoid(x)


def _rms(x, g):
    return x * lax.rsqrt(jnp.mean(x * x, axis=-1, keepdims=True) + RMS_EPS) * g


def _params(sem):
    return pltpu.CompilerParams(dimension_semantics=sem, vmem_limit_bytes=VMEM_LIMIT_BYTES)


def _iota(shape, dim):
    return lax.broadcasted_iota(jnp.int32, shape, dim)


def _split_hi_lo(x):
    hi = x.astype(BF16)
    lo = (x - hi.astype(F32)).astype(BF16)
    return hi, lo


def _split3(x):
    hi = x.astype(BF16)
    r = x - hi.astype(F32)
    mid = r.astype(BF16)
    lo = (r - mid.astype(F32)).astype(BF16)
    return hi, mid, lo


def _group_sum(x2, gmat):
    hi, lo = _split_hi_lo(x2)
    return _dot(hi, gmat) + _dot(lo, gmat)


_MAIN_SPLITS = (0, 256, 512, 1664, 2048, 2432)


def _gate_fns(raw, add, a_log):
    x = raw + add
    logf = -_softplus(-x)
    beta = _sigmoid(x)
    g = -jnp.exp(a_log) * _softplus(x)
    return logf, beta, g


_VX_ROWS = 80
_BIAS_ROWS = 3


def _inproj_kernel(x_ref, g_ref, wmain_ref, wq_ref, wkv_ref, wsm_ref, addc_ref, alogc_ref, addr_ref, alogr_ref,
                   u_ref, qkvb_ref, zb_ref, *out_refs, decode):
    hb = _rms(x_ref[...], g_ref[...]).astype(BF16)
    tm = hb.shape[0]

    def proj(i):
        return _dot_nt(hb, wmain_ref[_MAIN_SPLITS[i]:_MAIN_SPLITS[i + 1], :])

    u_ref[...] = proj(0) * _sigmoid(proj(1))
    qkvb_ref[...] = proj(2)
    zb_ref[...] = proj(3)
    kv = _dot_nt(wkv_ref[...], hb)
    if decode:
        q_ref, kt_ref, vt_ref, smt_ref, smr_ref = out_refs
        q_ref[...] = _dot_nt(hb, wq_ref[...]) * (HEAD_DIM ** -0.5)
    else:
        kr_ref, qxt_ref, vx_ref, kt_ref, vt_ref, smt_ref, smr_ref = out_refs
        kr_ref[...] = proj(4).astype(BF16)
        qt = _dot_nt(wq_ref[...], hb) * (HEAD_DIM ** -0.5 * LOG2E)
        ones_rows = jnp.where(_iota((LANES - HEAD_DIM, tm), 0) < _BIAS_ROWS, 1.0, 0.0).astype(BF16)
        for h in range(N_HEADS):
            qxt_ref[0, h * LANES:h * LANES + HEAD_DIM, :] = qt[h * HEAD_DIM:(h + 1) * HEAD_DIM].astype(BF16)
            qxt_ref[0, h * LANES + HEAD_DIM:(h + 1) * LANES, :] = ones_rows
            vx_ref[0, h * _VX_ROWS:h * _VX_ROWS + HEAD_DIM, :] = kv[HW + h * HEAD_DIM:HW + (h + 1) * HEAD_DIM].astype(BF16)
            vx_ref[0, h * _VX_ROWS + HEAD_DIM:(h + 1) * _VX_ROWS, :] = jnp.ones((_VX_ROWS - HEAD_DIM, tm), BF16)
    kt_ref[0] = kv[:HW]
    vt_ref[0] = kv[HW:]

    rawt = _dot_nt(wsm_ref[0:24, :], hb)
    logf, beta, g = _gate_fns(rawt, addc_ref[...], alogc_ref[...])
    smt_ref[0, 0:8] = logf[0:8]
    smt_ref[0, 8:16] = beta[8:16]
    smt_ref[0, 16:24] = g[16:24]

    rawr = _dot_nt(hb, wsm_ref[...])
    logf, beta, g = _gate_fns(rawr, addr_ref[...], alogr_ref[...])
    col = _iota((1, LANES), 1)
    smr_ref[...] = jnp.where(col < 8, logf, jnp.where(col < 16, beta, g))


def _inproj(x, g, wmain, wq, wkv, wsm, addc, alogc, addr, alogr, *, nb, tm, decode):
    m, d = x.shape
    t = m // nb
    nt = t // tm
    row = lambda i: (i, 0)
    const = lambda i: (0, 0)
    tr = lambda i: (i // nt, 0, i % nt)
    outs = [jax.ShapeDtypeStruct((m, CONV_A_WIDTH), F32), jax.ShapeDtypeStruct((m, 3 * HW), F32),
            jax.ShapeDtypeStruct((m, HW), F32)]
    out_specs = [pl.BlockSpec((tm, CONV_A_WIDTH), row), pl.BlockSpec((tm, 3 * HW), row), pl.BlockSpec((tm, HW), row)]
    if decode:
        outs += [jax.ShapeDtypeStruct((m, HW), F32)]
        out_specs += [pl.BlockSpec((tm, HW), row)]
    else:
        outs += [jax.ShapeDtypeStruct((m, HW), BF16), jax.ShapeDtypeStruct((nb, N_HEADS * LANES, t), BF16),
                 jax.ShapeDtypeStruct((nb, N_HEADS * _VX_ROWS, t), BF16)]
        out_specs += [pl.BlockSpec((tm, HW), row), pl.BlockSpec((1, N_HEADS * LANES, tm), tr),
                      pl.BlockSpec((1, N_HEADS * _VX_ROWS, tm), tr)]
    outs += [jax.ShapeDtypeStruct((nb, HW, t), F32), jax.ShapeDtypeStruct((nb, HW, t), F32),
             jax.ShapeDtypeStruct((nb, 24, t), F32), jax.ShapeDtypeStruct((m, LANES), F32)]
    out_specs += [pl.BlockSpec((1, HW, tm), tr), pl.BlockSpec((1, HW, tm), tr),
                  pl.BlockSpec((1, 24, tm), tr), pl.BlockSpec((tm, LANES), row)]
    in_specs = [
        pl.BlockSpec((tm, d), row), pl.BlockSpec((1, d), const),
        pl.BlockSpec(wmain.shape, const), pl.BlockSpec(wq.shape, const), pl.BlockSpec(wkv.shape, const),
        pl.BlockSpec(wsm.shape, const),
        pl.BlockSpec((24, 1), const), pl.BlockSpec((24, 1), const),
        pl.BlockSpec((1, LANES), const), pl.BlockSpec((1, LANES), const),
    ]
    return pl.pallas_call(
        functools.partial(_inproj_kernel, decode=decode), grid=(m // tm,), in_specs=in_specs, out_specs=out_specs,
        out_shape=outs, compiler_params=_params(("arbitrary",)), name="inproj",
    )(x, g, wmain, wq, wkv, wsm, addc, alogc, addr, alogr)


_CA_PAD = 32
_CA_ROWS = 64


def _conv_a_kernel(u_ref, buf_ref, w_ref, b_ref, lg_ref, lb_ref, y_ref, nbuf_ref, ext_ref, *, tt):
    ti = pl.program_id(1)

    @pl.when(ti == 0)
    def _():
        ext_ref[0:_CA_PAD] = buf_ref[0]

    ext_ref[_CA_PAD:_CA_PAD + tt] = u_ref[0]
    off = _CA_PAD - (CONV_A_KERNEL - 1)
    rows = min(_CA_ROWS, tt)
    for c in range(tt // rows):
        acc = jnp.zeros((rows, CONV_A_WIDTH), F32)
        for j in range(CONV_A_KERNEL):
            acc = acc + w_ref[j:j + 1, :] * ext_ref[off + j + c * rows:off + j + c * rows + rows, :]
        ca = acc + b_ref[...]
        mu = jnp.mean(ca, axis=-1, keepdims=True)
        xc = ca - mu
        var = jnp.mean(xc * xc, axis=-1, keepdims=True)
        y_ref[0, c * rows:(c + 1) * rows, :] = _silu(xc * lax.rsqrt(var + LN_EPS) * lg_ref[...] + lb_ref[...])
    tail = ext_ref[tt:tt + _CA_PAD]
    nbuf_ref[0] = tail
    ext_ref[0:_CA_PAD] = tail


def _conv_a(u, buf, w, b, lg, lb, *, tt):
    nb, t, c = u.shape
    const = lambda bi, ti: (0, 0)
    return pl.pallas_call(
        functools.partial(_conv_a_kernel, tt=tt),
        grid=(nb, t // tt),
        in_specs=[pl.BlockSpec((1, tt, c), lambda bi, ti: (bi, ti, 0)),
                  pl.BlockSpec((1, _CA_PAD, c), lambda bi, ti: (bi, 0, 0)),
                  pl.BlockSpec((_CA_PAD, c), const), pl.BlockSpec((1, c), const),
                  pl.BlockSpec((1, c), const), pl.BlockSpec((1, c), const)],
        out_specs=[pl.BlockSpec((1, tt, c), lambda bi, ti: (bi, ti, 0)),
                   pl.BlockSpec((1, _CA_PAD, c), lambda bi, ti: (bi, 0, 0))],
        out_shape=[jax.ShapeDtypeStruct((nb, t, c), F32), jax.ShapeDtypeStruct((nb, _CA_PAD, c), F32)],
        scratch_shapes=[pltpu.VMEM((_CA_PAD + tt, c), F32)],
        compiler_params=_params(("arbitrary", "arbitrary")), name="conv_a",
    )(u, buf, w, b, lg, lb)


_CB_PAD = 8


def _pair_cols(x, p):
    return x[:, p * LANES:(p + 1) * LANES]


def _head_cols(sm, base, p, lane_lo):
    c0 = sm[:, base + 2 * p:base + 2 * p + 1]
    c1 = sm[:, base + 2 * p + 1:base + 2 * p + 2]
    return jnp.where(lane_lo, c0, c1)


def _block_diag(x, lane_lo):
    return jnp.concatenate([jnp.where(lane_lo, x, 0.0), jnp.where(lane_lo, 0.0, x)], axis=0)


def _delta_kernel(x_ref, buf_ref, w_ref, smr_ref, smt_ref, z_ref, gb_ref, s0_ref, gmat_ref, tril_ref, triu_ref,
                  y_ref, nbuf_ref, sout_ref, ext_ref, s_ref, *, tt):
    ti = pl.program_id(1)
    nchunk = tt // DELTA_CHUNK
    c = DELTA_CHUNK

    @pl.when(ti == 0)
    def _():
        ext_ref[0:_CB_PAD] = buf_ref[0]
        s_ref[...] = s0_ref[0]

    ext_ref[_CB_PAD:_CB_PAD + tt] = x_ref[0]
    off = _CB_PAD - (SHORT_CONV - 1)
    acc = w_ref[0:1, :] * ext_ref[off:off + tt, :]
    for j in range(1, SHORT_CONV):
        acc = acc + w_ref[j:j + 1, :] * ext_ref[off + j:off + j + tt, :]
    tail = ext_ref[tt:tt + _CB_PAD]
    nbuf_ref[0] = tail
    ext_ref[0:_CB_PAD] = tail
    cb = _silu(acc)

    gmat = gmat_ref[...]
    q = cb[:, 0:HW]
    k = cb[:, HW:2 * HW]
    v = cb[:, 2 * HW:3 * HW]
    qs = q * lax.rsqrt(_group_sum(q * q, gmat) + 1e-6) * (HEAD_DIM ** -0.5)
    kn = k * lax.rsqrt(_group_sum(k * k, gmat) + 1e-6)

    smr = smr_ref[0]
    smt = smt_ref[0]
    s_hi, s_mid, s_lo = _split3(smr)
    gc_col = _dot(tril_ref[...], s_hi) + _dot(tril_ref[...], s_mid) + _dot(tril_ref[...], s_lo)
    t_hi, t_mid, t_lo = _split3(smt[16:24])
    gc_row = _dot(t_hi, triu_ref[...]) + _dot(t_mid, triu_ref[...]) + _dot(t_lo, triu_ref[...])
    if tt < LANES:
        gc_row = jnp.concatenate([gc_row, jnp.zeros((8, LANES - tt), F32)], axis=1)

    lane = _iota((1, LANES), 1)
    lane_lo = lane < HEAD_DIM
    ri = _iota((c, LANES), 0)
    ci_ = _iota((c, LANES), 1)
    cj = jnp.where(ci_ < HEAD_DIM, ci_, ci_ - HEAD_DIM)
    causal = ri >= cj
    strict = ri > cj
    eye2 = jnp.where(ri == cj, 1.0, 0.0)
    r128 = _iota((LANES, LANES), 0)
    c128 = _iota((LANES, LANES), 1)
    bdmask = (r128 < HEAD_DIM) == (c128 < HEAD_DIM)

    combos = [(ci, p) for ci in range(nchunk) for p in range(N_PAIRS)]
    pre = []
    for ci, p in combos:
        r0, r1 = ci * c, (ci + 1) * c
        qs_p = _pair_cols(qs, p)[r0:r1]
        kn_p = _pair_cols(kn, p)[r0:r1]
        v_p = _pair_cols(v, p)[r0:r1]
        beta = _head_cols(smr[r0:r1], 8, p, lane_lo)
        gi = _head_cols(gc_col[r0:r1], 16, p, lane_lo)
        blk = gc_row[:, (ci // 2) * LANES:(ci // 2 + 1) * LANES]
        rot = pltpu.roll(blk, HEAD_DIM, 1)
        if ci % 2 == 0:
            gj = jnp.where(lane_lo, blk[2 * p:2 * p + 1], rot[2 * p + 1:2 * p + 2])
        else:
            gj = jnp.where(lane_lo, rot[2 * p:2 * p + 1], blk[2 * p + 1:2 * p + 2])
        glast = gi[c - 1:c, :]
        eg = jnp.exp(gi)
        decay = jnp.exp(jnp.where(causal, gi - gj, NEG_INF))
        kb = kn_p * beta
        pre.append(dict(qs=qs_p, kn=kn_p, vb=v_p * beta, kb=kb, kbe=kb * eg, qe=qs_p * eg, eg_last=jnp.exp(glast),
                        kdec=kn_p * jnp.exp(glast - gi), decay=decay, kd=_block_diag(kn_p, lane_lo)))

    lmats = [jnp.where(strict, _dot_nt(d["kb"], d["kd"]) * d["decay"], 0.0) for d in pre]
    qks = [jnp.where(causal, _dot_nt(d["qs"], d["kd"]) * d["decay"], 0.0) for d in pre]

    def split_bd(m):
        hi, lo = _split_hi_lo(m)
        return hi, lo, _block_diag(hi, lane_lo), _block_diag(lo, lane_lo)

    def dot3(a_hi, a_lo, b_hi, b_lo):
        return _dot(a_hi, b_hi) + _dot(a_hi, b_lo) + _dot(a_lo, b_hi)

    xs = [eye2 - l for l in lmats]
    pws = lmats
    for _ in range(5):
        sp = [split_bd(pw) for pw in pws]
        pws = [dot3(hi, lo, bhi, blo) for hi, lo, bhi, blo in sp]
        sp = [split_bd(pw) for pw in pws]
        xsp = [_split_hi_lo(x) for x in xs]
        xs = [x + dot3(xh, xl, s[2], s[3]) for x, (xh, xl), s in zip(xs, xsp, sp)]

    us = [_dot(x, _block_diag(d["vb"], lane_lo)) for x, d in zip(xs, pre)]
    ws = [_dot(x, _block_diag(d["kbe"], lane_lo)) for x, d in zip(xs, pre)]
    kts = [d["kdec"].T for d in pre]
    kws = [jnp.where(bdmask, _dot(kt, w), 0.0) for kt, w in zip(kts, ws)]
    kus = [jnp.where(bdmask, _dot(kt, u), 0.0) for kt, u in zip(kts, us)]
    q2s = [d["qe"] - _dot(qk, _block_diag(w, lane_lo)) for d, qk, w in zip(pre, qks, ws)]
    o2s = [_dot(qk, _block_diag(u, lane_lo)) for qk, u in zip(qks, us)]

    states = [s_ref[p] for p in range(N_PAIRS)]
    o_chunks = []
    for ci in range(nchunk):
        o_pairs = []
        for p in range(N_PAIRS):
            i = ci * N_PAIRS + p
            s_bd = states[p]
            o_pairs.append(_dot(q2s[i], s_bd) + o2s[i])
            states[p] = s_bd * pre[i]["eg_last"] - _dot(kws[i], s_bd) + kus[i]
        o_chunks.append(jnp.concatenate(o_pairs, axis=1))
    for p in range(N_PAIRS):
        s_ref[p] = states[p]
    o = jnp.concatenate(o_chunks, axis=0) if nchunk > 1 else o_chunks[0]
    ms = _group_sum(o * o, gmat) * (1.0 / HEAD_DIM)
    y_ref[0] = o * lax.rsqrt(ms + RMS_EPS) * gb_ref[...] * _silu(z_ref[0])
    sout_ref[0] = s_ref[...]


def _delta(x, buf, w, smr, smt, z, gb, s0, *, tt):
    nb, t, cw = x.shape
    const = lambda bi, ti: (0, 0)
    head = jnp.arange(HW) // HEAD_DIM
    gmat = (head[:, None] == head[None, :]).astype(BF16)
    chunk = jnp.arange(tt) // DELTA_CHUNK
    same = chunk[:, None] == chunk[None, :]
    pos = jnp.arange(tt)
    tril = (same & (pos[:, None] >= pos[None, :])).astype(BF16)
    triu = tril.T
    return pl.pallas_call(
        functools.partial(_delta_kernel, tt=tt),
        grid=(nb, t // tt),
        in_specs=[pl.BlockSpec((1, tt, cw), lambda bi, ti: (bi, ti, 0)),
                  pl.BlockSpec((1, _CB_PAD, cw), lambda bi, ti: (bi, 0, 0)),
                  pl.BlockSpec((SHORT_CONV, cw), const),
                  pl.BlockSpec((1, tt, LANES), lambda bi, ti: (bi, ti, 0)),
                  pl.BlockSpec((1, 24, tt), lambda bi, ti: (bi, 0, ti)),
                  pl.BlockSpec((1, tt, HW), lambda bi, ti: (bi, ti, 0)),
                  pl.BlockSpec((1, HW), const),
                  pl.BlockSpec((1, N_PAIRS, LANES, LANES), lambda bi, ti: (bi, 0, 0, 0)),
                  pl.BlockSpec((HW, HW), const), pl.BlockSpec((tt, tt), const), pl.BlockSpec((tt, tt), const)],
        out_specs=[pl.BlockSpec((1, tt, HW), lambda bi, ti: (bi, ti, 0)),
                   pl.BlockSpec((1, _CB_PAD, cw), lambda bi, ti: (bi, 0, 0)),
                   pl.BlockSpec((1, N_PAIRS, LANES, LANES), lambda bi, ti: (bi, 0, 0, 0))],
        out_shape=[jax.ShapeDtypeStruct((nb, t, HW), F32), jax.ShapeDtypeStruct((nb, _CB_PAD, cw), F32),
                   jax.ShapeDtypeStruct((nb, N_PAIRS, LANES, LANES), F32)],
        scratch_shapes=[pltpu.VMEM((_CB_PAD + tt, cw), F32), pltpu.VMEM((N_PAIRS, LANES, LANES), F32)],
        compiler_params=_params(("arbitrary", "arbitrary")), name="delta",
    )(x, buf, w, smr, smt, z, gb, s0, gmat, tril, triu)


def _fox_keys_kernel(smr_ref, kr_ref, tril_ref, pk_ref, pb_ref, kx_ref, carry_ref):
    ti = pl.program_id(1)

    @pl.when(ti == 0)
    def _():
        carry_ref[...] = jnp.zeros_like(carry_ref)

    l_hi, l_mid, l_lo = _split3(smr_ref[0])
    tril = tril_ref[...]
    cs = _dot(tril, l_hi) + _dot(tril, l_mid) + _dot(tril, l_lo) + carry_ref[0:1, :]
    tl = cs.shape[0]
    carry_ref[...] = jnp.broadcast_to(cs[tl - 1:tl, :], carry_ref.shape)
    hi, mid, lo = _split3(jnp.where(_iota((1, LANES), 1) < N_HEADS, cs * (-LOG2E), 0.0))
    kx = (_dot(kr_ref[0], pk_ref[...]) + _dot(hi, pb_ref[0]) + _dot(mid, pb_ref[1]) + _dot(lo, pb_ref[2]))
    kx_ref[0] = kx.astype(BF16)


def _fox_keys(smr, kr, *, tl):
    nb, t, _ = smr.shape
    pos = jnp.arange(tl)
    tril = (pos[:, None] >= pos[None, :]).astype(BF16)
    src = jnp.arange(HW)
    dst = (src // HEAD_DIM) * LANES + src % HEAD_DIM
    pk = (dst[:, None] == jnp.arange(N_HEADS * LANES)[None, :]).astype(BF16)
    col = jnp.arange(LANES)
    pb = jnp.stack([((col[:, None] < N_HEADS)
                     & (col[:, None] * LANES + HEAD_DIM + piece == jnp.arange(N_HEADS * LANES)[None, :])).astype(BF16)
                    for piece in range(_BIAS_ROWS)])
    const2 = lambda bi, ti: (0, 0)
    return pl.pallas_call(
        _fox_keys_kernel, grid=(nb, t // tl),
        in_specs=[pl.BlockSpec((1, tl, LANES), lambda bi, ti: (bi, ti, 0)),
                  pl.BlockSpec((1, tl, HW), lambda bi, ti: (bi, ti, 0)),
                  pl.BlockSpec((tl, tl), const2), pl.BlockSpec(pk.shape, const2),
                  pl.BlockSpec(pb.shape, lambda bi, ti: (0, 0, 0))],
        out_specs=pl.BlockSpec((1, tl, N_HEADS * LANES), lambda bi, ti: (bi, ti, 0)),
        out_shape=jax.ShapeDtypeStruct((nb, t, N_HEADS * LANES), BF16),
        scratch_shapes=[pltpu.VMEM((8, LANES), F32)],
        compiler_params=_params(("arbitrary", "arbitrary")), name="fox_keys",
    )(smr, kr, tril, pk, pb)


def _fox_kernel(qi_ref, ki_ref, kx_ref, qxt_ref, vx_ref, o_ref, m_ref, acc_ref, *, tq, tk):
    step = pl.program_id(1)
    qi = qi_ref[step]
    ki = ki_ref[step]

    @pl.when(ki == 0)
    def _():
        m_ref[...] = jnp.full_like(m_ref, NEG_INF)
        acc_ref[...] = jnp.zeros_like(acc_ref)

    def update(masked):
        if masked:
            keep = _iota((tk, tq), 0) <= _iota((tk, tq), 1)
        heads = range(N_HEADS)
        sts = [_dot(kx_ref[0, :, h * LANES:(h + 1) * LANES], qxt_ref[0, h * LANES:(h + 1) * LANES, :]) for h in heads]
        if masked:
            sts = [jnp.where(keep, st, NEG_INF) for st in sts]
        m_prevs = [m_ref[h:h + 1, :] for h in heads]
        m_news = [jnp.maximum(mp, jnp.max(st, axis=0, keepdims=True)) for mp, st in zip(m_prevs, sts)]
        pts = [jnp.exp2(st - mn).astype(BF16) for st, mn in zip(sts, m_news)]
        for h in heads:
            alpha = jnp.exp2(m_prevs[h] - m_news[h])
            acc_ref[h] = alpha * acc_ref[h] + _dot(vx_ref[0, h * _VX_ROWS:(h + 1) * _VX_ROWS, :], pts[h])
            m_ref[h:h + 1, :] = m_news[h]

    @pl.when(ki < qi)
    def _():
        update(False)

    @pl.when(ki == qi)
    def _():
        update(True)
        for h in range(N_HEADS):
            acc = acc_ref[h]
            o_ref[0, h * HEAD_DIM:(h + 1) * HEAD_DIM, :] = acc[0:HEAD_DIM] / acc[HEAD_DIM:HEAD_DIM + 1]


def _fox_prompt(kx, qxt, vx, *, tq):
    nb, t, _ = kx.shape
    nq = t // tq
    qi_tab = jnp.asarray([qi for qi in range(nq) for _ in range(qi + 1)], jnp.int32)
    ki_tab = jnp.asarray([ki for qi in range(nq) for ki in range(qi + 1)], jnp.int32)
    grid_spec = pltpu.PrefetchScalarGridSpec(
        num_scalar_prefetch=2, grid=(nb, int(qi_tab.shape[0])),
        in_specs=[pl.BlockSpec((1, tq, N_HEADS * LANES), lambda b, s, qt, kt: (b, kt[s], 0)),
                  pl.BlockSpec((1, N_HEADS * LANES, tq), lambda b, s, qt, kt: (b, 0, qt[s])),
                  pl.BlockSpec((1, N_HEADS * _VX_ROWS, tq), lambda b, s, qt, kt: (b, 0, kt[s]))],
        out_specs=pl.BlockSpec((1, HW, tq), lambda b, s, qt, kt: (b, 0, qt[s])),
        scratch_shapes=[pltpu.VMEM((8, tq), F32), pltpu.VMEM((N_HEADS, _VX_ROWS, tq), F32)],
    )
    return pl.pallas_call(
        functools.partial(_fox_kernel, tq=tq, tk=tq), grid_spec=grid_spec,
        out_shape=jax.ShapeDtypeStruct((nb, HW, t), F32),
        compiler_params=_params(("arbitrary", "arbitrary")), name="fox_prompt",
    )(qi_tab, ki_tab, kx, qxt, vx)


def _page_cumsum_kernel(x_ref, triu_ref, o_ref):
    o_ref[...] = _dot_exact(x_ref[...], triu_ref[...])


def _page_cumsum(lf, *, rows):
    r = lf.shape[0]
    pos = jnp.arange(PAGE_SIZE)
    triu = (pos[:, None] <= pos[None, :]).astype(F32)
    return pl.pallas_call(
        _page_cumsum_kernel, grid=(r // rows,),
        in_specs=[pl.BlockSpec((rows, PAGE_SIZE), lambda i: (i, 0)),
                  pl.BlockSpec((PAGE_SIZE, PAGE_SIZE), lambda i: (0, 0))],
        out_specs=pl.BlockSpec((rows, PAGE_SIZE), lambda i: (i, 0)),
        out_shape=jax.ShapeDtypeStruct((r, PAGE_SIZE), F32),
        compiler_params=_params(("arbitrary",)), name="page_cumsum",
    )(lf, triu)


def _fox_decode_kernel(pt_ref, qx_ref, *refs, n_steps, group):
    kt_refs = refs[0:group]
    vt_refs = refs[group:2 * group]
    lcs_refs = refs[2 * group:3 * group]
    kn_ref, vn_ref, lfn_ref, o_ref, qbd_ref, m_ref, l_ref, acc_ref, carry_ref = refs[3 * group:]
    j = pl.program_id(1)

    @pl.when(j == 0)
    def _():
        row = _iota((8, HW), 0)
        col = _iota((8, HW), 1)
        own = (col >= row * HEAD_DIM) & (col < (row + 1) * HEAD_DIM)
        qbd_ref[...] = jnp.where(own, qx_ref[0], 0.0)
        m_ref[...] = jnp.full_like(m_ref, NEG_INF)
        l_ref[...] = jnp.zeros_like(l_ref)
        acc_ref[...] = jnp.zeros_like(acc_ref)
        carry_ref[...] = jnp.zeros_like(carry_ref)

    qbd = qbd_ref[...]
    lcs = [lcs_refs[g][0, 0] for g in range(group)]
    totals = [jnp.broadcast_to(x[:, PAGE_SIZE - 1:PAGE_SIZE], x.shape) for x in lcs]
    qk = [_dot(qbd, kt_refs[g][0, 0]) for g in range(group)]
    c_run = carry_ref[...]
    scores = []
    for g in range(group):
        scores.append(qk[g] - (lcs[g] + c_run))
        c_run = c_run + totals[g]
    c_end = c_run[:, 0:1]
    s = jnp.concatenate(scores, axis=1)
    m_prev = m_ref[:, 0:1]
    m_new = jnp.maximum(m_prev, jnp.max(s, axis=-1, keepdims=True))
    alpha = jnp.exp(m_prev - m_new)
    pr = jnp.exp(s - m_new)
    l_new = alpha * l_ref[:, 0:1] + jnp.sum(pr, axis=-1, keepdims=True)
    pv = _dot_nt(pr[:, 0:PAGE_SIZE], vt_refs[0][0, 0])
    for g in range(1, group):
        pv = pv + _dot_nt(pr[:, g * PAGE_SIZE:(g + 1) * PAGE_SIZE], vt_refs[g][0, 0])
    acc_new = alpha * acc_ref[...] + pv
    m_ref[...] = jnp.broadcast_to(m_new, m_ref.shape)
    l_ref[...] = jnp.broadcast_to(l_new, l_ref.shape)
    acc_ref[...] = acc_new
    carry_ref[...] = c_run

    @pl.when(j == n_steps - 1)
    def _():
        s_n = jnp.sum(qbd_ref[...] * kn_ref[0], axis=-1, keepdims=True) - (c_end + lfn_ref[0][:, 0:1])
        m_f = jnp.maximum(m_new, s_n)
        a_f = jnp.exp(m_new - m_f)
        p_n = jnp.exp(s_n - m_f)
        l_f = a_f * l_new + p_n
        o8 = (a_f * acc_new + p_n * vn_ref[0]) / l_f
        row = _iota((8, HW), 0)
        col = _iota((8, HW), 1)
        own = (col >= row * HEAD_DIM) & (col < (row + 1) * HEAD_DIM)
        o_ref[0] = jnp.sum(jnp.where(own, o8, 0.0), axis=0, keepdims=True)


def _fox_decode(page_table, layer, qx, cache_kt, cache_vt, cache_lcs, k_new, v_new, lf_new, *, group):
    bd = qx.shape[0]
    n_pages = page_table.shape[1]
    n_steps = n_pages // group

    def page(g):
        return lambda b, j, pt: (layer, pt[b, j * group + g], 0, 0)

    per_b = lambda b, j, pt: (b, 0, 0)
    in_specs = [pl.BlockSpec((1, 1, HW), per_b)]
    in_specs += [pl.BlockSpec((1, 1, HW, PAGE_SIZE), page(g)) for g in range(group)]
    in_specs += [pl.BlockSpec((1, 1, HW, PAGE_SIZE), page(g)) for g in range(group)]
    in_specs += [pl.BlockSpec((1, 1, 8, PAGE_SIZE), page(g)) for g in range(group)]
    in_specs += [pl.BlockSpec((1, 1, HW), per_b), pl.BlockSpec((1, 1, HW), per_b), pl.BlockSpec((1, 8, LANES), per_b)]
    grid_spec = pltpu.PrefetchScalarGridSpec(
        num_scalar_prefetch=1, grid=(bd, n_steps), in_specs=in_specs,
        out_specs=pl.BlockSpec((1, 1, HW), per_b),
        scratch_shapes=[pltpu.VMEM((8, HW), F32), pltpu.VMEM((8, LANES), F32), pltpu.VMEM((8, LANES), F32),
                        pltpu.VMEM((8, HW), F32), pltpu.VMEM((8, LANES), F32)],
    )
    return pl.pallas_call(
        functools.partial(_fox_decode_kernel, n_steps=n_steps, group=group), grid_spec=grid_spec,
        out_shape=jax.ShapeDtypeStruct((bd, 1, HW), F32),
        compiler_params=_params(("arbitrary", "arbitrary")), name="fox_decode",
    )(page_table, qx, *([cache_kt] * group), *([cache_vt] * group), *([cache_lcs] * group), k_new, v_new, lf_new)


_FF_CHUNK = 704


def _outffn_kernel(x_ref, ya_ref, yb_ref, yc_ref, wo_ref, wg_ref, wu_ref, wd_ref, gpm_ref, gpf_ref, gqf_ref, o_ref,
                   *, yc_transposed):
    a0 = CONV_A_WIDTH
    a1 = a0 + HW
    yc = yc_ref[0].T if yc_transposed else yc_ref[...]
    mix = (_dot(ya_ref[...].astype(BF16), wo_ref[0:a0, :]) + _dot(yb_ref[...].astype(BF16), wo_ref[a0:a1, :])
           + _dot(yc.astype(BF16), wo_ref[a1:a1 + HW, :]))
    x1 = x_ref[...] + _rms(mix, gpm_ref[...])
    hb = _rms(x1, gpf_ref[...]).astype(BF16)
    d_ff = wg_ref.shape[1]
    f = jnp.zeros_like(x1)
    for c0 in range(0, d_ff, _FF_CHUNK):
        gate = _dot(hb, wg_ref[:, c0:c0 + _FF_CHUNK])
        up = _dot(hb, wu_ref[:, c0:c0 + _FF_CHUNK])
        f = f + _dot((_silu(gate) * up).astype(BF16), wd_ref[c0:c0 + _FF_CHUNK, :])
    o_ref[...] = x1 + _rms(f, gqf_ref[...])


def _outffn(x, ya, yb, yc, wo, wg, wu, wd, gpm, gpf, gqf, *, tm):
    m, d = x.shape
    row = lambda i: (i, 0)
    const = lambda i: (0, 0)
    whole = lambda a: pl.BlockSpec(a.shape, const, pipeline_mode=pl.Buffered(1))
    yc_transposed = yc.ndim == 3
    if yc_transposed:
        nt = yc.shape[2] // tm
        yc_spec = pl.BlockSpec((1, HW, tm), lambda i: (i // nt, 0, i % nt))
    else:
        yc_spec = pl.BlockSpec((tm, HW), row)
    return pl.pallas_call(
        functools.partial(_outffn_kernel, yc_transposed=yc_transposed), grid=(m // tm,),
        in_specs=[pl.BlockSpec((tm, d), row), pl.BlockSpec((tm, CONV_A_WIDTH), row),
                  pl.BlockSpec((tm, HW), row), yc_spec,
                  whole(wo), whole(wg), whole(wu), whole(wd),
                  pl.BlockSpec((1, d), const), pl.BlockSpec((1, d), const), pl.BlockSpec((1, d), const)],
        out_specs=pl.BlockSpec((tm, d), row),
        out_shape=jax.ShapeDtypeStruct((m, d), F32),
        compiler_params=_params(("arbitrary",)), name="outffn",
    )(x, ya, yb, yc, wo, wg, wu, wd, gpm, gpf, gqf)


def _layer_weights(l, w_in, conv_a_w, conv_a_b, ln_a_g, ln_a_b, conv_b_w, a_log, dt_bias, norm_b_g, f_bias,
                   w_out, g_pre_mix, g_post_mix, g_pre_ffn, g_post_ffn, w_gate, w_up, w_down):
    wt = w_in[l].T
    wmain = jnp.concatenate([wt[0:2048], wt[2444:2828]], axis=0).astype(BF16)
    wq = wt[2060:2444].astype(BF16)
    wkv = wt[2444:3212].astype(BF16)
    z2 = jnp.zeros((2, wt.shape[1]), F32)
    wsm = jnp.concatenate([wt[3212:3218], z2, wt[2048:2054], z2, wt[2054:2060], z2,
                           jnp.zeros((LANES - 24, wt.shape[1]), F32)], axis=0).astype(BF16)
    z2v = jnp.zeros((2,), F32)
    z8v = jnp.zeros((8,), F32)
    add24 = jnp.concatenate([f_bias[l], z2v, z8v, dt_bias[l], z2v])
    alog24 = jnp.concatenate([z8v, z8v, a_log[l], z2v])
    pad_row = lambda v: jnp.concatenate([v, jnp.zeros((LANES - 24,), F32)])[None, :]
    return dict(
        g_pre_mix=g_pre_mix[l][None, :], wmain=wmain, wq=wq, wkv=wkv, wsm=wsm,
        addc=add24[:, None], alogc=alog24[:, None], addr=pad_row(add24), alogr=pad_row(alog24),
        conv_a_w=jnp.concatenate([conv_a_w[l], jnp.zeros((1, CONV_A_WIDTH), F32)], axis=0),
        conv_a_b=conv_a_b[l][None, :], ln_a_g=ln_a_g[l][None, :], ln_a_b=ln_a_b[l][None, :],
        conv_b_w=conv_b_w[l], norm_b_g=jnp.tile(norm_b_g[l], N_HEADS)[None, :],
        w_out=w_out[l].astype(BF16), w_gate=w_gate[l].astype(BF16), w_up=w_up[l].astype(BF16),
        w_down=w_down[l].astype(BF16),
        g_post_mix=g_post_mix[l][None, :], g_pre_ffn=g_pre_ffn[l][None, :], g_post_ffn=g_post_ffn[l][None, :],
    )


def _state_to_pairs(s):
    nb = s.shape[0]
    s = s.reshape(nb, N_PAIRS, 2, HEAD_DIM, HEAD_DIM)
    z = jnp.zeros_like(s[:, :, 0])
    top = jnp.concatenate([s[:, :, 0], z], axis=-1)
    bot = jnp.concatenate([z, s[:, :, 1]], axis=-1)
    return jnp.concatenate([top, bot], axis=-2)


def _pairs_to_state(sp):
    nb = sp.shape[0]
    s0 = sp[:, :, :HEAD_DIM, :HEAD_DIM]
    s1 = sp[:, :, HEAD_DIM:, HEAD_DIM:]
    return jnp.stack([s0, s1], axis=2).reshape(nb, N_HEADS, HEAD_DIM, HEAD_DIM)


def _pad_rows_front(buf, rows):
    nb, r, c = buf.shape
    return jnp.concatenate([jnp.zeros((nb, rows - r, c), buf.dtype), buf], axis=1)


def _prompt_layer(x, p, *, tm, tq, tt_a, tt_b):
    nb, t, d = x.shape
    xf = x.reshape(nb * t, d)
    u, qkvb, zb, kr, qxt, vx, kt, vt, smt, smr = _inproj(
        xf, p["g_pre_mix"], p["wmain"], p["wq"], p["wkv"], p["wsm"], p["addc"], p["alogc"], p["addr"], p["alogr"],
        nb=nb, tm=tm, decode=False)
    ya, nbuf_a = _conv_a(u.reshape(nb, t, -1), jnp.zeros((nb, _CA_PAD, CONV_A_WIDTH), F32), p["conv_a_w"],
                         p["conv_a_b"], p["ln_a_g"], p["ln_a_b"], tt=tt_a)
    yb, nbuf_b, s_new = _delta(qkvb.reshape(nb, t, -1), jnp.zeros((nb, _CB_PAD, 3 * HW), F32), p["conv_b_w"],
                               smr.reshape(nb, t, LANES), smt, zb.reshape(nb, t, HW), p["norm_b_g"],
                               jnp.zeros((nb, N_PAIRS, LANES, LANES), F32), tt=tt_b)
    kx = _fox_keys(smr.reshape(nb, t, LANES), kr.reshape(nb, t, HW), tl=tq)
    yc = _fox_prompt(kx, qxt, vx, tq=tq)
    y = _outffn(xf, ya.reshape(nb * t, -1), yb.reshape(nb * t, -1), yc, p["w_out"],
                p["w_gate"], p["w_up"], p["w_down"], p["g_post_mix"], p["g_pre_ffn"], p["g_post_ffn"], tm=tm)
    k_out = kt.reshape(nb, N_HEADS, HEAD_DIM, t).transpose(0, 3, 1, 2)
    v_out = vt.reshape(nb, N_HEADS, HEAD_DIM, t).transpose(0, 3, 1, 2)
    logf_out = smt[:, 0:N_HEADS, :].transpose(0, 2, 1)
    states = (k_out, v_out, logf_out, nbuf_a[:, _CA_PAD - (CONV_A_KERNEL - 1):],
              nbuf_b[:, _CB_PAD - (SHORT_CONV - 1):], _pairs_to_state(s_new))
    return y.reshape(nb, t, d), states


def _sample_layer(x, p, layer, buf_a, buf_b, s_delta, cache_kt, cache_vt, cache_lcs, page_table, *, group):
    bd, _, d = x.shape
    xf = x.reshape(bd, d)
    u, qkvb, zb, qx, kt, vt, smt, smr = _inproj(
        xf, p["g_pre_mix"], p["wmain"], p["wq"], p["wkv"], p["wsm"], p["addc"], p["alogc"], p["addr"], p["alogr"],
        nb=1, tm=bd, decode=True)
    ya, nbuf_a = _conv_a(u.reshape(bd, 1, -1), _pad_rows_front(buf_a, _CA_PAD), p["conv_a_w"], p["conv_a_b"],
                         p["ln_a_g"], p["ln_a_b"], tt=1)

    c = DELTA_CHUNK
    pad_t = lambda a: jnp.concatenate([a[:, None, :], jnp.zeros((bd, c - 1, a.shape[-1]), a.dtype)], axis=1)
    smr_b = pad_t(smr)
    smt_b = jnp.transpose(smr_b[:, :, 0:24], (0, 2, 1))
    yb, nbuf_b, s_new = _delta(pad_t(qkvb), _pad_rows_front(buf_b, _CB_PAD), p["conv_b_w"], smr_b, smt_b,
                               pad_t(zb), p["norm_b_g"], _state_to_pairs(s_delta), tt=c)
    yb = yb[:, 0, :]
    new_buf_b = jnp.concatenate([buf_b[:, 1:], qkvb[:, None, :]], axis=1)

    k_new = kt[0].T
    v_new = vt[0].T
    lf_new = smt[0, 0:8, :].T
    yc = _fox_decode(page_table, layer, qx[:, None, :], cache_kt, cache_vt, cache_lcs, k_new[:, None, :],
                     v_new[:, None, :], jnp.broadcast_to(lf_new[:, :, None], (bd, 8, LANES)), group=group)
    y = _outffn(xf, ya.reshape(bd, -1), yb, yc.reshape(bd, -1), p["w_out"], p["w_gate"], p["w_up"], p["w_down"],
                p["g_post_mix"], p["g_pre_ffn"], p["g_post_ffn"], tm=bd)
    states = (k_new.reshape(bd, 1, N_HEADS, HEAD_DIM), v_new.reshape(bd, 1, N_HEADS, HEAD_DIM),
              lf_new[:, None, 0:N_HEADS], nbuf_a[:, _CA_PAD - (CONV_A_KERNEL - 1):], new_buf_b,
              _pairs_to_state(s_new))
    return y.reshape(bd, 1, d), states


def _forward(x_prompt, x_sample, cache_k, cache_v, cache_logf, page_table, state_conv_a, state_conv_b,
             state_delta, weights, *, tm, tq, tt_a, tt_b, group):
    depth = cache_k.shape[0]
    n_phys = cache_k.shape[1]
    ckt = jnp.transpose(cache_k, (0, 1, 3, 4, 2)).reshape(depth, n_phys, HW, PAGE_SIZE)
    cvt = jnp.transpose(cache_v, (0, 1, 3, 4, 2)).reshape(depth, n_phys, HW, PAGE_SIZE)
    clf = jnp.transpose(cache_logf, (0, 1, 3, 2))
    clf = jnp.concatenate([clf, jnp.zeros((depth, n_phys, 8 - N_HEADS, PAGE_SIZE), F32)], axis=2)
    n_rows = depth * n_phys * 8
    rows = max(r for r in range(8, min(n_rows, 2048) + 1, 8) if n_rows % r == 0)
    clcs = _page_cumsum(clf.reshape(n_rows, PAGE_SIZE), rows=rows).reshape(depth, n_phys, 8, PAGE_SIZE)
    xp, xs = x_prompt, x_sample
    prompt_states, sample_states = [], []
    for l in range(depth):
        p = _layer_weights(l, *weights)
        xp, sp = _prompt_layer(xp, p, tm=tm, tq=tq, tt_a=tt_a, tt_b=tt_b)
        prompt_states.append(sp)
        xs, ss = _sample_layer(xs, p, l, state_conv_a[l], state_conv_b[l], state_delta[l], ckt, cvt, clcs,
                               page_table, group=group)
        sample_states.append(ss)
    ps = [jnp.stack(t) for t in zip(*prompt_states)]
    ss = [jnp.stack(t) for t in zip(*sample_states)]
    return (xp, xs, *ps, *ss)


def kernel(x_prompt, x_sample, cache_k, cache_v, cache_logf, page_table, state_conv_a, state_conv_b, state_delta,
           w_in, conv_a_w, conv_a_b, ln_a_g, ln_a_b, conv_b_w, a_log, dt_bias, norm_b_g, f_bias, w_out, g_pre_mix,
           g_post_mix, g_pre_ffn, g_post_ffn, w_gate, w_up, w_down):
    weights = (w_in, conv_a_w, conv_a_b, ln_a_g, ln_a_b, conv_b_w, a_log, dt_bias, norm_b_g, f_bias, w_out,
               g_pre_mix, g_post_mix, g_pre_ffn, g_post_ffn, w_gate, w_up, w_down)
    return _forward(x_prompt, x_sample, cache_k, cache_v, cache_logf, page_table, state_conv_a, state_conv_b,
                    state_delta, weights, tm=512, tq=512, tt_a=512, tt_b=256, group=16)
```

```python
import functools

import jax
import jax.numpy as jnp
from jax import lax
from jax.experimental import pallas as pl
from jax.experimental.pallas import tpu as pltpu

F32 = jnp.float32
BF16 = jnp.bfloat16

HEAD_DIM = 64
N_HEADS = 6
N_PAIRS = N_HEADS // 2
HW = N_HEADS * HEAD_DIM
CONV_A_WIDTH = 256
CONV_A_KERNEL = 31
SHORT_CONV = 4
DELTA_CHUNK = 64
PAGE_SIZE = 128
LANES = 128
SUBLANES = 8
RMS_EPS = 1e-6
LN_EPS = 1e-5
VMEM_LIMIT_BYTES = 56 * 1024 * 1024
NEG_INF = float("-inf")
LOG2E = 1.4426950408889634

_NT = (((1,), (1,)), ((), ()))


def _dot(a, b):
    return jnp.dot(a, b, preferred_element_type=F32)


def _dot_nt(a, b):
    return lax.dot_general(a, b, _NT, preferred_element_type=F32)


def _sigmoid(x):
    return 1.0 / (1.0 + jnp.exp(-x))


def _softplus(x):
    return jnp.maximum(x, 0.0) + jnp.log(1.0 + jnp.exp(-jnp.abs(x)))


def _silu(x):
    return x * _sigmoid(x)


def _rms(x, g):
    return x * lax.rsqrt(jnp.mean(x * x, axis=-1, keepdims=True) + RMS_EPS) * g


def _params(sem):
    return pltpu.CompilerParams(dimension_semantics=sem, vmem_limit_bytes=VMEM_LIMIT_BYTES)


def _iota(shape, dim):
    return lax.broadcasted_iota(jnp.int32, shape, dim)


def _split_hi_lo(x):
    hi = x.astype(BF16)
    lo = (x - hi.astype(F32)).astype(BF16)
    return hi, lo


def _split3(x):
    hi = x.astype(BF16)
    r = x - hi.astype(F32)
    mid = r.astype(BF16)
    lo = (r - mid.astype(F32)).astype(BF16)
    return hi, mid, lo


def _group_sum(x2, gmat):
    hi, lo = _split_hi_lo(x2)
    return _dot(hi, gmat) + _dot(lo, gmat)


_MAIN_SPLITS = (0, 256, 512, 1664, 2048, 2432)


def _gate_fns(raw, add, a_log):
    x = raw + add
    logf = -_softplus(-x)
    beta = _sigmoid(x)
    g = -jnp.exp(a_log) * _softplus(x)
    return logf, beta, g


_VX_ROWS = 80
_BIAS_ROWS = 3
_FOX_KEY_SPLIT = 2
_FOX_Q_BLOCK = 128


def _inproj_kernel(x_ref, g_ref, wmain_ref, wq_ref, wkv_ref, wsm_ref, addc_ref, alogc_ref, addr_ref, alogr_ref,
                   u_ref, qkvb_ref, zb_ref, *out_refs, decode):
    hb = _rms(x_ref[...], g_ref[...]).astype(BF16)
    tm = hb.shape[0]

    def proj(i):
        return _dot_nt(hb, wmain_ref[_MAIN_SPLITS[i]:_MAIN_SPLITS[i + 1], :])

    u_ref[...] = proj(0) * _sigmoid(proj(1))
    qkvb_ref[...] = proj(2)
    zb_ref[...] = proj(3)
    kv = _dot_nt(wkv_ref[...], hb)
    if decode:
        q_ref, kt_ref, vt_ref, smt_ref, smr_ref = out_refs
        q_ref[...] = _dot_nt(hb, wq_ref[...]) * (HEAD_DIM ** -0.5)
    else:
        kr_ref, qxt_ref, vx_ref, kt_ref, vt_ref, smt_ref, smr_ref = out_refs
        kr_ref[...] = proj(4).astype(BF16)
        qt = _dot_nt(wq_ref[...], hb) * (HEAD_DIM ** -0.5 * LOG2E)
        ones_rows = jnp.where(_iota((LANES - HEAD_DIM, tm), 0) < _BIAS_ROWS, 1.0, 0.0).astype(BF16)
        for h in range(N_HEADS):
            qxt_ref[0, h * LANES:h * LANES + HEAD_DIM, :] = qt[h * HEAD_DIM:(h + 1) * HEAD_DIM].astype(BF16)
            qxt_ref[0, h * LANES + HEAD_DIM:(h + 1) * LANES, :] = ones_rows
            vx_ref[0, h * _VX_ROWS:h * _VX_ROWS + HEAD_DIM, :] = kv[HW + h * HEAD_DIM:HW + (h + 1) * HEAD_DIM].astype(BF16)
            vx_ref[0, h * _VX_ROWS + HEAD_DIM:(h + 1) * _VX_ROWS, :] = jnp.ones((_VX_ROWS - HEAD_DIM, tm), BF16)
    kt_ref[0] = kv[:HW]
    vt_ref[0] = kv[HW:]

    rawt = _dot_nt(wsm_ref[0:24, :], hb)
    logf, beta, g = _gate_fns(rawt, addc_ref[...], alogc_ref[...])
    smt_ref[0, 0:8] = logf[0:8]
    smt_ref[0, 8:16] = beta[8:16]
    smt_ref[0, 16:24] = g[16:24]

    rawr = _dot_nt(hb, wsm_ref[...])
    logf, beta, g = _gate_fns(rawr, addr_ref[...], alogr_ref[...])
    col = _iota((1, LANES), 1)
    smr_ref[...] = jnp.where(col < 8, logf, jnp.where(col < 16, beta, g))


def _inproj(x, g, wmain, wq, wkv, wsm, addc, alogc, addr, alogr, *, nb, tm, decode):
    m, d = x.shape
    t = m // nb
    nt = t // tm
    row = lambda i: (i, 0)
    const = lambda i: (0, 0)
    tr = lambda i: (i // nt, 0, i % nt)
    outs = [jax.ShapeDtypeStruct((m, CONV_A_WIDTH), F32), jax.ShapeDtypeStruct((m, 3 * HW), F32),
            jax.ShapeDtypeStruct((m, HW), F32)]
    out_specs = [pl.BlockSpec((tm, CONV_A_WIDTH), row), pl.BlockSpec((tm, 3 * HW), row), pl.BlockSpec((tm, HW), row)]
    if decode:
        outs += [jax.ShapeDtypeStruct((m, HW), F32)]
        out_specs += [pl.BlockSpec((tm, HW), row)]
    else:
        outs += [jax.ShapeDtypeStruct((m, HW), BF16), jax.ShapeDtypeStruct((nb, N_HEADS * LANES, t), BF16),
                 jax.ShapeDtypeStruct((nb, N_HEADS * _VX_ROWS, t), BF16)]
        out_specs += [pl.BlockSpec((tm, HW), row), pl.BlockSpec((1, N_HEADS * LANES, tm), tr),
                      pl.BlockSpec((1, N_HEADS * _VX_ROWS, tm), tr)]
    outs += [jax.ShapeDtypeStruct((nb, HW, t), F32), jax.ShapeDtypeStruct((nb, HW, t), F32),
             jax.ShapeDtypeStruct((nb, 24, t), F32), jax.ShapeDtypeStruct((m, LANES), F32)]
    out_specs += [pl.BlockSpec((1, HW, tm), tr), pl.BlockSpec((1, HW, tm), tr),
                  pl.BlockSpec((1, 24, tm), tr), pl.BlockSpec((tm, LANES), row)]
    in_specs = [
        pl.BlockSpec((tm, d), row), pl.BlockSpec((1, d), const),
        pl.BlockSpec(wmain.shape, const), pl.BlockSpec(wq.shape, const), pl.BlockSpec(wkv.shape, const),
        pl.BlockSpec(wsm.shape, const),
        pl.BlockSpec((24, 1), const), pl.BlockSpec((24, 1), const),
        pl.BlockSpec((1, LANES), const), pl.BlockSpec((1, LANES), const),
    ]
    return pl.pallas_call(
        functools.partial(_inproj_kernel, decode=decode), grid=(m // tm,), in_specs=in_specs, out_specs=out_specs,
        out_shape=outs, compiler_params=_params(("arbitrary",)), name="inproj",
    )(x, g, wmain, wq, wkv, wsm, addc, alogc, addr, alogr)


_CA_PAD = 32
_CA_ROWS = 64


def _conv_a_kernel(u_ref, buf_ref, w_ref, b_ref, lg_ref, lb_ref, y_ref, nbuf_ref, ext_ref, *scratch, tt):
    ti = pl.program_id(1)

    @pl.when(ti == 0)
    def _():
        ext_ref[0:_CA_PAD] = buf_ref[0]

    ext_ref[_CA_PAD:_CA_PAD + tt] = u_ref[0]
    off = _CA_PAD - (CONV_A_KERNEL - 1)
    rows = min(_CA_ROWS, tt)
    aligned = tt % SUBLANES == 0
    if aligned:
        (sh_ref,) = scratch
        span = tt + _CA_PAD - SUBLANES
        for r in range(1, SUBLANES):
            sh_ref[r - 1] = ext_ref[r:r + span]
    for c in range(tt // rows):
        acc = jnp.zeros((rows, CONV_A_WIDTH), F32)
        for j in range(CONV_A_KERNEL):
            o = off + j + c * rows
            if aligned and o % SUBLANES:
                tap = sh_ref[o % SUBLANES - 1, o - o % SUBLANES:o - o % SUBLANES + rows, :]
            else:
                tap = ext_ref[o:o + rows, :]
            acc = acc + w_ref[j:j + 1, :] * tap
        ca = acc + b_ref[...]
        mu = jnp.mean(ca, axis=-1, keepdims=True)
        xc = ca - mu
        var = jnp.mean(xc * xc, axis=-1, keepdims=True)
        y_ref[0, c * rows:(c + 1) * rows, :] = _silu(xc * lax.rsqrt(var + LN_EPS) * lg_ref[...] + lb_ref[...])
    tail = ext_ref[tt:tt + _CA_PAD]
    nbuf_ref[0] = tail
    ext_ref[0:_CA_PAD] = tail


def _conv_a(u, buf, w, b, lg, lb, *, tt):
    nb, t, c = u.shape
    const = lambda bi, ti: (0, 0)
    return pl.pallas_call(
        functools.partial(_conv_a_kernel, tt=tt),
        grid=(nb, t // tt),
        in_specs=[pl.BlockSpec((1, tt, c), lambda bi, ti: (bi, ti, 0)),
                  pl.BlockSpec((1, _CA_PAD, c), lambda bi, ti: (bi, 0, 0)),
                  pl.BlockSpec((_CA_PAD, c), const), pl.BlockSpec((1, c), const),
                  pl.BlockSpec((1, c), const), pl.BlockSpec((1, c), const)],
        out_specs=[pl.BlockSpec((1, tt, c), lambda bi, ti: (bi, ti, 0)),
                   pl.BlockSpec((1, _CA_PAD, c), lambda bi, ti: (bi, 0, 0))],
        out_shape=[jax.ShapeDtypeStruct((nb, t, c), F32), jax.ShapeDtypeStruct((nb, _CA_PAD, c), F32)],
        scratch_shapes=[pltpu.VMEM((_CA_PAD + tt, c), F32)]
        + ([pltpu.VMEM((SUBLANES - 1, tt + _CA_PAD - SUBLANES, c), F32)] if tt % SUBLANES == 0 else []),
        compiler_params=_params(("arbitrary", "arbitrary")), name="conv_a",
    )(u, buf, w, b, lg, lb)


_CB_PAD = 8


def _pair_cols(x, p):
    return x[:, p * LANES:(p + 1) * LANES]


def _head_cols(sm, base, p, lane_lo):
    c0 = sm[:, base + 2 * p:base + 2 * p + 1]
    c1 = sm[:, base + 2 * p + 1:base + 2 * p + 2]
    return jnp.where(lane_lo, c0, c1)


def _block_diag(x, lane_lo):
    return jnp.concatenate([jnp.where(lane_lo, x, 0.0), jnp.where(lane_lo, 0.0, x)], axis=0)


def _delta_kernel(x_ref, buf_ref, w_ref, smr_ref, smt_ref, z_ref, gb_ref, s0_ref, gmat_ref, tril_ref, triu_ref,
                  y_ref, nbuf_ref, sout_ref, ext_ref, s_ref, *, tt):
    ti = pl.program_id(1)
    nchunk = tt // DELTA_CHUNK
    c = DELTA_CHUNK

    @pl.when(ti == 0)
    def _():
        ext_ref[0:_CB_PAD] = buf_ref[0]
        s_ref[...] = s0_ref[0]

    ext_ref[_CB_PAD:_CB_PAD + tt] = x_ref[0]
    off = _CB_PAD - (SHORT_CONV - 1)
    acc = w_ref[0:1, :] * ext_ref[off:off + tt, :]
    for j in range(1, SHORT_CONV):
        acc = acc + w_ref[j:j + 1, :] * ext_ref[off + j:off + j + tt, :]
    tail = ext_ref[tt:tt + _CB_PAD]
    nbuf_ref[0] = tail
    ext_ref[0:_CB_PAD] = tail
    cb = _silu(acc)

    gmat = gmat_ref[...]
    q = cb[:, 0:HW]
    k = cb[:, HW:2 * HW]
    v = cb[:, 2 * HW:3 * HW]
    qs = q * lax.rsqrt(_group_sum(q * q, gmat) + 1e-6) * (HEAD_DIM ** -0.5)
    kn = k * lax.rsqrt(_group_sum(k * k, gmat) + 1e-6)

    smr = smr_ref[0]
    smt = smt_ref[0]
    s_hi, s_mid, s_lo = _split3(smr)
    gc_col = _dot(tril_ref[...], s_hi) + _dot(tril_ref[...], s_mid) + _dot(tril_ref[...], s_lo)
    t_hi, t_mid, t_lo = _split3(smt[16:24])
    gc_row = _dot(t_hi, triu_ref[...]) + _dot(t_mid, triu_ref[...]) + _dot(t_lo, triu_ref[...])
    if tt < LANES:
        gc_row = jnp.concatenate([gc_row, jnp.zeros((8, LANES - tt), F32)], axis=1)

    lane = _iota((1, LANES), 1)
    lane_lo = lane < HEAD_DIM
    ri = _iota((c, LANES), 0)
    ci_ = _iota((c, LANES), 1)
    cj = jnp.where(ci_ < HEAD_DIM, ci_, ci_ - HEAD_DIM)
    causal = ri >= cj
    strict = ri > cj
    eye2 = jnp.where(ri == cj, 1.0, 0.0)
    r128 = _iota((LANES, LANES), 0)
    c128 = _iota((LANES, LANES), 1)
    bdmask = (r128 < HEAD_DIM) == (c128 < HEAD_DIM)

    combos = [(ci, p) for ci in range(nchunk) for p in range(N_PAIRS)]
    pre = []
    for ci, p in combos:
        r0, r1 = ci * c, (ci + 1) * c
        qs_p = _pair_cols(qs, p)[r0:r1]
        kn_p = _pair_cols(kn, p)[r0:r1]
        v_p = _pair_cols(v, p)[r0:r1]
        beta = _head_cols(smr[r0:r1], 8, p, lane_lo)
        gi = _head_cols(gc_col[r0:r1], 16, p, lane_lo)
        blk = gc_row[:, (ci // 2) * LANES:(ci // 2 + 1) * LANES]
        rot = pltpu.roll(blk, HEAD_DIM, 1)
        if ci % 2 == 0:
            gj = jnp.where(lane_lo, blk[2 * p:2 * p + 1], rot[2 * p + 1:2 * p + 2])
        else:
            gj = jnp.where(lane_lo, rot[2 * p:2 * p + 1], blk[2 * p + 1:2 * p + 2])
        glast = gi[c - 1:c, :]
        eg = jnp.exp(gi)
        decay = jnp.exp(jnp.where(causal, gi - gj, NEG_INF))
        kb = kn_p * beta
        pre.append(dict(qs=qs_p, kn=kn_p, vb=v_p * beta, kb=kb, kbe=kb * eg, qe=qs_p * eg, eg_last=jnp.exp(glast),
                        kdec=kn_p * jnp.exp(glast - gi), decay=decay, kd=_block_diag(kn_p, lane_lo)))

    grams = [_dot_nt(jnp.concatenate([d["kb"], d["qs"]], axis=0), d["kd"]) for d in pre]
    lmats = [jnp.where(strict, g[0:c] * d["decay"], 0.0) for g, d in zip(grams, pre)]
    qks = [jnp.where(causal, g[c:2 * c] * d["decay"], 0.0) for g, d in zip(grams, pre)]

    def dot3(a, b):
        a_hi, a_lo = _split_hi_lo(a)
        b_hi, b_lo = _split_hi_lo(b)
        b_hi, b_lo = _block_diag(b_hi, lane_lo), _block_diag(b_lo, lane_lo)
        return _dot(a_hi, b_hi) + _dot(a_lo, b_hi) + _dot(a_hi, b_lo)

    xs = [eye2 - l for l in lmats]
    pws = [dot3(l, l) for l in lmats]
    for level in range(4):
        prods = [dot3(jnp.concatenate([pw, x], axis=0), pw) for pw, x in zip(pws, xs)]
        xs = [x + pr[c:2 * c] for x, pr in zip(xs, prods)]
        pws = [pr[0:c] for pr in prods]
    xs = [x + dot3(x, pw) for pw, x in zip(pws, xs)]

    rhs_uw = [jnp.concatenate([_block_diag(d["vb"], lane_lo), _block_diag(d["kbe"], lane_lo)], axis=1) for d in pre]
    uws = [_dot(x, r) for x, r in zip(xs, rhs_uw)]
    kts = [d["kdec"].T for d in pre]
    kuws = [_dot(kt, uw) for kt, uw in zip(kts, uws)]
    kus = [jnp.where(bdmask, m[:, 0:LANES], 0.0) for m in kuws]
    kws = [jnp.where(bdmask, m[:, LANES:2 * LANES], 0.0) for m in kuws]
    rhs_q = [jnp.concatenate([_block_diag(uw[:, 0:LANES], lane_lo), _block_diag(uw[:, LANES:2 * LANES], lane_lo)],
                             axis=1) for uw in uws]
    qkuw = [_dot(qk, r) for qk, r in zip(qks, rhs_q)]
    o2s = [m[:, 0:LANES] for m in qkuw]
    q2s = [d["qe"] - m[:, LANES:2 * LANES] for d, m in zip(pre, qkuw)]

    states = [s_ref[p] for p in range(N_PAIRS)]
    o_chunks = []
    for ci in range(nchunk):
        o_pairs = []
        for p in range(N_PAIRS):
            i = ci * N_PAIRS + p
            s_bd = states[p]
            prod = _dot(jnp.concatenate([q2s[i], kws[i]], axis=0), s_bd)
            o_pairs.append(prod[0:c] + o2s[i])
            states[p] = s_bd * pre[i]["eg_last"] - prod[c:c + LANES] + kus[i]
        o_chunks.append(jnp.concatenate(o_pairs, axis=1))
    for p in range(N_PAIRS):
        s_ref[p] = states[p]
    o = jnp.concatenate(o_chunks, axis=0) if nchunk > 1 else o_chunks[0]
    ms = _group_sum(o * o, gmat) * (1.0 / HEAD_DIM)
    y_ref[0] = o * lax.rsqrt(ms + RMS_EPS) * gb_ref[...] * _silu(z_ref[0])
    sout_ref[0] = s_ref[...]


def _delta(x, buf, w, smr, smt, z, gb, s0, *, tt):
    nb, t, cw = x.shape
    const = lambda bi, ti: (0, 0)
    head = jnp.arange(HW) // HEAD_DIM
    gmat = (head[:, None] == head[None, :]).astype(BF16)
    chunk = jnp.arange(tt) // DELTA_CHUNK
    same = chunk[:, None] == chunk[None, :]
    pos = jnp.arange(tt)
    tril = (same & (pos[:, None] >= pos[None, :])).astype(BF16)
    triu = tril.T
    return pl.pallas_call(
        functools.partial(_delta_kernel, tt=tt),
        grid=(nb, t // tt),
        in_specs=[pl.BlockSpec((1, tt, cw), lambda bi, ti: (bi, ti, 0)),
                  pl.BlockSpec((1, _CB_PAD, cw), lambda bi, ti: (bi, 0, 0)),
                  pl.BlockSpec((SHORT_CONV, cw), const),
                  pl.BlockSpec((1, tt, LANES), lambda bi, ti: (bi, ti, 0)),
                  pl.BlockSpec((1, 24, tt), lambda bi, ti: (bi, 0, ti)),
                  pl.BlockSpec((1, tt, HW), lambda bi, ti: (bi, ti, 0)),
                  pl.BlockSpec((1, HW), const),
                  pl.BlockSpec((1, N_PAIRS, LANES, LANES), lambda bi, ti: (bi, 0, 0, 0)),
                  pl.BlockSpec((HW, HW), const), pl.BlockSpec((tt, tt), const), pl.BlockSpec((tt, tt), const)],
        out_specs=[pl.BlockSpec((1, tt, HW), lambda bi, ti: (bi, ti, 0)),
                   pl.BlockSpec((1, _CB_PAD, cw), lambda bi, ti: (bi, 0, 0)),
                   pl.BlockSpec((1, N_PAIRS, LANES, LANES), lambda bi, ti: (bi, 0, 0, 0))],
        out_shape=[jax.ShapeDtypeStruct((nb, t, HW), F32), jax.ShapeDtypeStruct((nb, _CB_PAD, cw), F32),
                   jax.ShapeDtypeStruct((nb, N_PAIRS, LANES, LANES), F32)],
        scratch_shapes=[pltpu.VMEM((_CB_PAD + tt, cw), F32), pltpu.VMEM((N_PAIRS, LANES, LANES), F32)],
        compiler_params=_params(("arbitrary", "arbitrary")), name="delta",
    )(x, buf, w, smr, smt, z, gb, s0, gmat, tril, triu)


def _fox_keys_kernel(smr_ref, kr_ref, tril_ref, pk_ref, pb_ref, kx_ref, carry_ref):
    ti = pl.program_id(1)

    @pl.when(ti == 0)
    def _():
        carry_ref[...] = jnp.zeros_like(carry_ref)

    l_hi, l_mid, l_lo = _split3(smr_ref[0])
    tril = tril_ref[...]
    cs = _dot(tril, l_hi) + _dot(tril, l_mid) + _dot(tril, l_lo) + carry_ref[0:1, :]
    tl = cs.shape[0]
    carry_ref[...] = jnp.broadcast_to(cs[tl - 1:tl, :], carry_ref.shape)
    hi, mid, lo = _split3(jnp.where(_iota((1, LANES), 1) < N_HEADS, cs * (-LOG2E), 0.0))
    kx = (_dot(kr_ref[0], pk_ref[...]) + _dot(hi, pb_ref[0]) + _dot(mid, pb_ref[1]) + _dot(lo, pb_ref[2]))
    kx_ref[0] = kx.astype(BF16)


def _fox_keys(smr, kr, *, tl):
    nb, t, _ = smr.shape
    pos = jnp.arange(tl)
    tril = (pos[:, None] >= pos[None, :]).astype(BF16)
    src = jnp.arange(HW)
    dst = (src // HEAD_DIM) * LANES + src % HEAD_DIM
    pk = (dst[:, None] == jnp.arange(N_HEADS * LANES)[None, :]).astype(BF16)
    col = jnp.arange(LANES)
    pb = jnp.stack([((col[:, None] < N_HEADS)
                     & (col[:, None] * LANES + HEAD_DIM + piece == jnp.arange(N_HEADS * LANES)[None, :])).astype(BF16)
                    for piece in range(_BIAS_ROWS)])
    const2 = lambda bi, ti: (0, 0)
    return pl.pallas_call(
        _fox_keys_kernel, grid=(nb, t // tl),
        in_specs=[pl.BlockSpec((1, tl, LANES), lambda bi, ti: (bi, ti, 0)),
                  pl.BlockSpec((1, tl, HW), lambda bi, ti: (bi, ti, 0)),
                  pl.BlockSpec((tl, tl), const2), pl.BlockSpec(pk.shape, const2),
                  pl.BlockSpec(pb.shape, lambda bi, ti: (0, 0, 0))],
        out_specs=pl.BlockSpec((1, tl, N_HEADS * LANES), lambda bi, ti: (bi, ti, 0)),
        out_shape=jax.ShapeDtypeStruct((nb, t, N_HEADS * LANES), BF16),
        scratch_shapes=[pltpu.VMEM((8, LANES), F32)],
        compiler_params=_params(("arbitrary", "arbitrary")), name="fox_keys",
    )(smr, kr, tril, pk, pb)


def _fox_kernel(qi_ref, ki_ref, kx_ref, qxt_ref, vx_ref, o_ref, m_ref, acc_ref, *, tq, tk):
    step = pl.program_id(1)
    qi = qi_ref[step]
    ki = ki_ref[step]

    @pl.when(ki == 0)
    def _():
        m_ref[...] = jnp.full_like(m_ref, NEG_INF)
        acc_ref[...] = jnp.zeros_like(acc_ref)

    def update(masked):
        if masked:
            keep = _iota((tk, tq), 0) <= _iota((tk, tq), 1)
        combos = [(h, qc) for h in range(N_HEADS) for qc in range(tq // _FOX_Q_BLOCK)]
        half = tk // _FOX_KEY_SPLIT
        ms = [m_ref[h:h + 1, qc * _FOX_Q_BLOCK:(qc + 1) * _FOX_Q_BLOCK] for h, qc in combos]
        accs = [acc_ref[h, :, qc * _FOX_Q_BLOCK:(qc + 1) * _FOX_Q_BLOCK] for h, qc in combos]
        for kh in range(_FOX_KEY_SPLIT):
            k0 = kh * half
            sts = [_dot(kx_ref[0, k0:k0 + half, h * LANES:(h + 1) * LANES],
                        qxt_ref[0, h * LANES:(h + 1) * LANES, qc * _FOX_Q_BLOCK:(qc + 1) * _FOX_Q_BLOCK]) for h, qc in combos]
            if masked:
                sts = [jnp.where(keep[k0:k0 + half, qc * _FOX_Q_BLOCK:(qc + 1) * _FOX_Q_BLOCK], st, NEG_INF)
                       for st, (h, qc) in zip(sts, combos)]
            m_news = [jnp.maximum(mp, jnp.max(st, axis=0, keepdims=True)) for mp, st in zip(ms, sts)]
            pts = [jnp.exp2(st - mn).astype(BF16) for st, mn in zip(sts, m_news)]
            accs = [jnp.exp2(mp - mn) * acc + _dot(vx_ref[0, h * _VX_ROWS:(h + 1) * _VX_ROWS, k0:k0 + half], pt)
                    for (h, qc), mp, mn, acc, pt in zip(combos, ms, m_news, accs, pts)]
            ms = m_news
        for (h, qc), mn, acc in zip(combos, ms, accs):
            acc_ref[h, :, qc * _FOX_Q_BLOCK:(qc + 1) * _FOX_Q_BLOCK] = acc
            m_ref[h:h + 1, qc * _FOX_Q_BLOCK:(qc + 1) * _FOX_Q_BLOCK] = mn

    @pl.when(ki < qi)
    def _():
        update(False)

    @pl.when(ki == qi)
    def _():
        update(True)
        for h in range(N_HEADS):
            acc = acc_ref[h]
            o_ref[0, h * HEAD_DIM:(h + 1) * HEAD_DIM, :] = acc[0:HEAD_DIM] / acc[HEAD_DIM:HEAD_DIM + 1]


def _fox_prompt(kx, qxt, vx, *, tq):
    nb, t, _ = kx.shape
    nq = t // tq
    qi_tab = jnp.asarray([qi for qi in range(nq) for _ in range(qi + 1)], jnp.int32)
    ki_tab = jnp.asarray([ki for qi in range(nq) for ki in range(qi + 1)], jnp.int32)
    grid_spec = pltpu.PrefetchScalarGridSpec(
        num_scalar_prefetch=2, grid=(nb, int(qi_tab.shape[0])),
        in_specs=[pl.BlockSpec((1, tq, N_HEADS * LANES), lambda b, s, qt, kt: (b, kt[s], 0)),
                  pl.BlockSpec((1, N_HEADS * LANES, tq), lambda b, s, qt, kt: (b, 0, qt[s])),
                  pl.BlockSpec((1, N_HEADS * _VX_ROWS, tq), lambda b, s, qt, kt: (b, 0, kt[s]))],
        out_specs=pl.BlockSpec((1, HW, tq), lambda b, s, qt, kt: (b, 0, qt[s])),
        scratch_shapes=[pltpu.VMEM((8, tq), F32), pltpu.VMEM((N_HEADS, _VX_ROWS, tq), F32)],
    )
    return pl.pallas_call(
        functools.partial(_fox_kernel, tq=tq, tk=tq), grid_spec=grid_spec,
        out_shape=jax.ShapeDtypeStruct((nb, HW, t), F32),
        compiler_params=_params(("arbitrary", "arbitrary")), name="fox_prompt",
    )(qi_tab, ki_tab, kx, qxt, vx)


def _page_cumsum_kernel(x_ref, triu_ref, o_ref):
    hi, mid, lo = _split3(x_ref[...])
    triu = triu_ref[...]
    o_ref[...] = _dot(hi, triu) + _dot(mid, triu) + _dot(lo, triu)


def _page_cumsum(lf, *, rows):
    r = lf.shape[0]
    pos = jnp.arange(PAGE_SIZE)
    triu = (pos[:, None] <= pos[None, :]).astype(BF16)
    return pl.pallas_call(
        _page_cumsum_kernel, grid=(r // rows,),
        in_specs=[pl.BlockSpec((rows, PAGE_SIZE), lambda i: (i, 0)),
                  pl.BlockSpec((PAGE_SIZE, PAGE_SIZE), lambda i: (0, 0))],
        out_specs=pl.BlockSpec((rows, PAGE_SIZE), lambda i: (i, 0)),
        out_shape=jax.ShapeDtypeStruct((r, PAGE_SIZE), F32),
        compiler_params=_params(("arbitrary",)), name="page_cumsum",
    )(lf, triu)


def _fox_decode_kernel(pt_ref, qx_ref, *refs, n_steps, group):
    kt_refs = refs[0:group]
    vt_refs = refs[group:2 * group]
    lcs_refs = refs[2 * group:3 * group]
    kn_ref, vn_ref, lfn_ref, o_ref, qbd_ref, m_ref, l_ref, acc_ref, carry_ref = refs[3 * group:]
    j = pl.program_id(1)

    @pl.when(j == 0)
    def _():
        row = _iota((8, HW), 0)
        col = _iota((8, HW), 1)
        own = (col >= row * HEAD_DIM) & (col < (row + 1) * HEAD_DIM)
        qbd_ref[...] = jnp.where(own, qx_ref[0], 0.0)
        m_ref[...] = jnp.full_like(m_ref, NEG_INF)
        l_ref[...] = jnp.zeros_like(l_ref)
        acc_ref[...] = jnp.zeros_like(acc_ref)
        carry_ref[...] = jnp.zeros_like(carry_ref)

    qbd = qbd_ref[...]
    lcs = [lcs_refs[g][0, 0] for g in range(group)]
    totals = [jnp.broadcast_to(x[:, PAGE_SIZE - 1:PAGE_SIZE], x.shape) for x in lcs]
    qk = [_dot(qbd, kt_refs[g][0, 0]) for g in range(group)]
    c_run = carry_ref[...]
    scores = []
    for g in range(group):
        scores.append(qk[g] - (lcs[g] + c_run))
        c_run = c_run + totals[g]
    c_end = c_run[:, 0:1]
    s = jnp.concatenate(scores, axis=1)
    m_prev = m_ref[:, 0:1]
    m_new = jnp.maximum(m_prev, jnp.max(s, axis=-1, keepdims=True))
    alpha = jnp.exp(m_prev - m_new)
    pr = jnp.exp(s - m_new)
    l_new = alpha * l_ref[:, 0:1] + jnp.sum(pr, axis=-1, keepdims=True)
    pv = _dot_nt(pr[:, 0:PAGE_SIZE], vt_refs[0][0, 0])
    for g in range(1, group):
        pv = pv + _dot_nt(pr[:, g * PAGE_SIZE:(g + 1) * PAGE_SIZE], vt_refs[g][0, 0])
    acc_new = alpha * acc_ref[...] + pv
    m_ref[...] = jnp.broadcast_to(m_new, m_ref.shape)
    l_ref[...] = jnp.broadcast_to(l_new, l_ref.shape)
    acc_ref[...] = acc_new
    carry_ref[...] = c_run

    @pl.when(j == n_steps - 1)
    def _():
        s_n = jnp.sum(qbd_ref[...] * kn_ref[0], axis=-1, keepdims=True) - (c_end + lfn_ref[0][:, 0:1])
        m_f = jnp.maximum(m_new, s_n)
        a_f = jnp.exp(m_new - m_f)
        p_n = jnp.exp(s_n - m_f)
        l_f = a_f * l_new + p_n
        o8 = (a_f * acc_new + p_n * vn_ref[0]) / l_f
        row = _iota((8, HW), 0)
        col = _iota((8, HW), 1)
        own = (col >= row * HEAD_DIM) & (col < (row + 1) * HEAD_DIM)
        o_ref[0] = jnp.sum(jnp.where(own, o8, 0.0), axis=0, keepdims=True)


def _fox_decode(page_table, layer, qx, cache_kt, cache_vt, cache_lcs, k_new, v_new, lf_new, *, group):
    bd = qx.shape[0]
    n_pages = page_table.shape[1]
    n_steps = n_pages // group

    def page(g):
        return lambda b, j, pt: (layer, pt[b, j * group + g], 0, 0)

    per_b = lambda b, j, pt: (b, 0, 0)
    in_specs = [pl.BlockSpec((1, 1, HW), per_b)]
    in_specs += [pl.BlockSpec((1, 1, HW, PAGE_SIZE), page(g)) for g in range(group)]
    in_specs += [pl.BlockSpec((1, 1, HW, PAGE_SIZE), page(g)) for g in range(group)]
    in_specs += [pl.BlockSpec((1, 1, 8, PAGE_SIZE), page(g)) for g in range(group)]
    in_specs += [pl.BlockSpec((1, 1, HW), per_b), pl.BlockSpec((1, 1, HW), per_b), pl.BlockSpec((1, 8, LANES), per_b)]
    grid_spec = pltpu.PrefetchScalarGridSpec(
        num_scalar_prefetch=1, grid=(bd, n_steps), in_specs=in_specs,
        out_specs=pl.BlockSpec((1, 1, HW), per_b),
        scratch_shapes=[pltpu.VMEM((8, HW), F32), pltpu.VMEM((8, LANES), F32), pltpu.VMEM((8, LANES), F32),
                        pltpu.VMEM((8, HW), F32), pltpu.VMEM((8, LANES), F32)],
    )
    return pl.pallas_call(
        functools.partial(_fox_decode_kernel, n_steps=n_steps, group=group), grid_spec=grid_spec,
        out_shape=jax.ShapeDtypeStruct((bd, 1, HW), F32),
        compiler_params=_params(("arbitrary", "arbitrary")), name="fox_decode",
    )(page_table, qx, *([cache_kt] * group), *([cache_vt] * group), *([cache_lcs] * group), k_new, v_new, lf_new)


_FF_CHUNK = 704


def _outffn_kernel(x_ref, ya_ref, yb_ref, yc_ref, wo_ref, wg_ref, wu_ref, wd_ref, gpm_ref, gpf_ref, gqf_ref, o_ref,
                   *, yc_transposed):
    a0 = CONV_A_WIDTH
    a1 = a0 + HW
    yc = yc_ref[0].T if yc_transposed else yc_ref[...]
    mix = (_dot(ya_ref[...].astype(BF16), wo_ref[0:a0, :]) + _dot(yb_ref[...].astype(BF16), wo_ref[a0:a1, :])
           + _dot(yc.astype(BF16), wo_ref[a1:a1 + HW, :]))
    x1 = x_ref[...] + _rms(mix, gpm_ref[...])
    hb = _rms(x1, gpf_ref[...]).astype(BF16)
    d_ff = wg_ref.shape[1]
    f = jnp.zeros_like(x1)
    for c0 in range(0, d_ff, _FF_CHUNK):
        gate = _dot(hb, wg_ref[:, c0:c0 + _FF_CHUNK])
        up = _dot(hb, wu_ref[:, c0:c0 + _FF_CHUNK])
        f = f + _dot((_silu(gate) * up).astype(BF16), wd_ref[c0:c0 + _FF_CHUNK, :])
    o_ref[...] = x1 + _rms(f, gqf_ref[...])


def _outffn(x, ya, yb, yc, wo, wg, wu, wd, gpm, gpf, gqf, *, tm):
    m, d = x.shape
    row = lambda i: (i, 0)
    const = lambda i: (0, 0)
    whole = lambda a: pl.BlockSpec(a.shape, const, pipeline_mode=pl.Buffered(1))
    yc_transposed = yc.ndim == 3
    if yc_transposed:
        nt = yc.shape[2] // tm
        yc_spec = pl.BlockSpec((1, HW, tm), lambda i: (i // nt, 0, i % nt))
    else:
        yc_spec = pl.BlockSpec((tm, HW), row)
    return pl.pallas_call(
        functools.partial(_outffn_kernel, yc_transposed=yc_transposed), grid=(m // tm,),
        in_specs=[pl.BlockSpec((tm, d), row), pl.BlockSpec((tm, CONV_A_WIDTH), row),
                  pl.BlockSpec((tm, HW), row), yc_spec,
                  whole(wo), whole(wg), whole(wu), whole(wd),
                  pl.BlockSpec((1, d), const), pl.BlockSpec((1, d), const), pl.BlockSpec((1, d), const)],
        out_specs=pl.BlockSpec((tm, d), row),
        out_shape=jax.ShapeDtypeStruct((m, d), F32),
        compiler_params=_params(("arbitrary",)), name="outffn",
    )(x, ya, yb, yc, wo, wg, wu, wd, gpm, gpf, gqf)


def _layer_weights(l, w_in, conv_a_w, conv_a_b, ln_a_g, ln_a_b, conv_b_w, a_log, dt_bias, norm_b_g, f_bias,
                   w_out, g_pre_mix, g_post_mix, g_pre_ffn, g_post_ffn, w_gate, w_up, w_down):
    wt = w_in[l].T
    wmain = jnp.concatenate([wt[0:2048], wt[2444:2828]], axis=0).astype(BF16)
    wq = wt[2060:2444].astype(BF16)
    wkv = wt[2444:3212].astype(BF16)
    z2 = jnp.zeros((2, wt.shape[1]), F32)
    wsm = jnp.concatenate([wt[3212:3218], z2, wt[2048:2054], z2, wt[2054:2060], z2,
                           jnp.zeros((LANES - 24, wt.shape[1]), F32)], axis=0).astype(BF16)
    z2v = jnp.zeros((2,), F32)
    z8v = jnp.zeros((8,), F32)
    add24 = jnp.concatenate([f_bias[l], z2v, z8v, dt_bias[l], z2v])
    alog24 = jnp.concatenate([z8v, z8v, a_log[l], z2v])
    pad_row = lambda v: jnp.concatenate([v, jnp.zeros((LANES - 24,), F32)])[None, :]
    return dict(
        g_pre_mix=g_pre_mix[l][None, :], wmain=wmain, wq=wq, wkv=wkv, wsm=wsm,
        addc=add24[:, None], alogc=alog24[:, None], addr=pad_row(add24), alogr=pad_row(alog24),
        conv_a_w=jnp.concatenate([conv_a_w[l], jnp.zeros((1, CONV_A_WIDTH), F32)], axis=0),
        conv_a_b=conv_a_b[l][None, :], ln_a_g=ln_a_g[l][None, :], ln_a_b=ln_a_b[l][None, :],
        conv_b_w=conv_b_w[l], norm_b_g=jnp.tile(norm_b_g[l], N_HEADS)[None, :],
        w_out=w_out[l].astype(BF16), w_gate=w_gate[l].astype(BF16), w_up=w_up[l].astype(BF16),
        w_down=w_down[l].astype(BF16),
        g_post_mix=g_post_mix[l][None, :], g_pre_ffn=g_pre_ffn[l][None, :], g_post_ffn=g_post_ffn[l][None, :],
    )


def _state_to_pairs(s):
    nb = s.shape[0]
    s = s.reshape(nb, N_PAIRS, 2, HEAD_DIM, HEAD_DIM)
    z = jnp.zeros_like(s[:, :, 0])
    top = jnp.concatenate([s[:, :, 0], z], axis=-1)
    bot = jnp.concatenate([z, s[:, :, 1]], axis=-1)
    return jnp.concatenate([top, bot], axis=-2)


def _pairs_to_state(sp):
    nb = sp.shape[0]
    s0 = sp[:, :, :HEAD_DIM, :HEAD_DIM]
    s1 = sp[:, :, HEAD_DIM:, HEAD_DIM:]
    return jnp.stack([s0, s1], axis=2).reshape(nb, N_HEADS, HEAD_DIM, HEAD_DIM)


def _pad_rows_front(buf, rows):
    nb, r, c = buf.shape
    return jnp.concatenate([jnp.zeros((nb, rows - r, c), buf.dtype), buf], axis=1)


def _prompt_layer(x, p, *, tm, tq, tt_a, tt_b):
    nb, t, d = x.shape
    xf = x.reshape(nb * t, d)
    u, qkvb, zb, kr, qxt, vx, kt, vt, smt, smr = _inproj(
        xf, p["g_pre_mix"], p["wmain"], p["wq"], p["wkv"], p["wsm"], p["addc"], p["alogc"], p["addr"], p["alogr"],
        nb=nb, tm=tm, decode=False)
    ya, nbuf_a = _conv_a(u.reshape(nb, t, -1), jnp.zeros((nb, _CA_PAD, CONV_A_WIDTH), F32), p["conv_a_w"],
                         p["conv_a_b"], p["ln_a_g"], p["ln_a_b"], tt=tt_a)
    yb, nbuf_b, s_new = _delta(qkvb.reshape(nb, t, -1), jnp.zeros((nb, _CB_PAD, 3 * HW), F32), p["conv_b_w"],
                               smr.reshape(nb, t, LANES), smt, zb.reshape(nb, t, HW), p["norm_b_g"],
                               jnp.zeros((nb, N_PAIRS, LANES, LANES), F32), tt=tt_b)
    kx = _fox_keys(smr.reshape(nb, t, LANES), kr.reshape(nb, t, HW), tl=tq)
    yc = _fox_prompt(kx, qxt, vx, tq=tq)
    y = _outffn(xf, ya.reshape(nb * t, -1), yb.reshape(nb * t, -1), yc, p["w_out"],
                p["w_gate"], p["w_up"], p["w_down"], p["g_post_mix"], p["g_pre_ffn"], p["g_post_ffn"], tm=tm)
    k_out = kt.reshape(nb, N_HEADS, HEAD_DIM, t).transpose(0, 3, 1, 2)
    v_out = vt.reshape(nb, N_HEADS, HEAD_DIM, t).transpose(0, 3, 1, 2)
    logf_out = smt[:, 0:N_HEADS, :].transpose(0, 2, 1)
    states = (k_out, v_out, logf_out, nbuf_a[:, _CA_PAD - (CONV_A_KERNEL - 1):],
              nbuf_b[:, _CB_PAD - (SHORT_CONV - 1):], _pairs_to_state(s_new))
    return y.reshape(nb, t, d), states


def _sample_layer(x, p, layer, buf_a, buf_b, s_delta, cache_kt, cache_vt, cache_lcs, page_table, *, group):
    bd, _, d = x.shape
    xf = x.reshape(bd, d)
    u, qkvb, zb, qx, kt, vt, smt, smr = _inproj(
        xf, p["g_pre_mix"], p["wmain"], p["wq"], p["wkv"], p["wsm"], p["addc"], p["alogc"], p["addr"], p["alogr"],
        nb=1, tm=bd, decode=True)
    ya, nbuf_a = _conv_a(u.reshape(bd, 1, -1), _pad_rows_front(buf_a, _CA_PAD), p["conv_a_w"], p["conv_a_b"],
                         p["ln_a_g"], p["ln_a_b"], tt=1)

    c = DELTA_CHUNK
    pad_t = lambda a: jnp.concatenate([a[:, None, :], jnp.zeros((bd, c - 1, a.shape[-1]), a.dtype)], axis=1)
    smr_b = pad_t(smr)
    smt_b = jnp.transpose(smr_b[:, :, 0:24], (0, 2, 1))
    yb, nbuf_b, s_new = _delta(pad_t(qkvb), _pad_rows_front(buf_b, _CB_PAD), p["conv_b_w"], smr_b, smt_b,
                               pad_t(zb), p["norm_b_g"], _state_to_pairs(s_delta), tt=c)
    yb = yb[:, 0, :]
    new_buf_b = jnp.concatenate([buf_b[:, 1:], qkvb[:, None, :]], axis=1)

    k_new = kt[0].T
    v_new = vt[0].T
    lf_new = smt[0, 0:8, :].T
    yc = _fox_decode(page_table, layer, qx[:, None, :], cache_kt, cache_vt, cache_lcs, k_new[:, None, :],
                     v_new[:, None, :], jnp.broadcast_to(lf_new[:, :, None], (bd, 8, LANES)), group=group)
    y = _outffn(xf, ya.reshape(bd, -1), yb, yc.reshape(bd, -1), p["w_out"], p["w_gate"], p["w_up"], p["w_down"],
                p["g_post_mix"], p["g_pre_ffn"], p["g_post_ffn"], tm=bd)
    states = (k_new.reshape(bd, 1, N_HEADS, HEAD_DIM), v_new.reshape(bd, 1, N_HEADS, HEAD_DIM),
              lf_new[:, None, 0:N_HEADS], nbuf_a[:, _CA_PAD - (CONV_A_KERNEL - 1):], new_buf_b,
              _pairs_to_state(s_new))
    return y.reshape(bd, 1, d), states


def _forward(x_prompt, x_sample, cache_k, cache_v, cache_logf, page_table, state_conv_a, state_conv_b,
             state_delta, weights, *, tm, tq, tt_a, tt_b, group):
    depth = cache_k.shape[0]
    n_phys = cache_k.shape[1]
    ckt = jnp.transpose(cache_k, (0, 1, 3, 4, 2)).reshape(depth, n_phys, HW, PAGE_SIZE)
    cvt = jnp.transpose(cache_v, (0, 1, 3, 4, 2)).reshape(depth, n_phys, HW, PAGE_SIZE)
    clf = jnp.transpose(cache_logf, (0, 1, 3, 2))
    clf = jnp.concatenate([clf, jnp.zeros((depth, n_phys, 8 - N_HEADS, PAGE_SIZE), F32)], axis=2)
    n_rows = depth * n_phys * 8
    rows = max(r for r in range(8, min(n_rows, 2048) + 1, 8) if n_rows % r == 0)
    clcs = _page_cumsum(clf.reshape(n_rows, PAGE_SIZE), rows=rows).reshape(depth, n_phys, 8, PAGE_SIZE)
    xp, xs = x_prompt, x_sample
    prompt_states, sample_states = [], []
    for l in range(depth):
        p = _layer_weights(l, *weights)
        xp, sp = _prompt_layer(xp, p, tm=tm, tq=tq, tt_a=tt_a, tt_b=tt_b)
        prompt_states.append(sp)
        xs, ss = _sample_layer(xs, p, l, state_conv_a[l], state_conv_b[l], state_delta[l], ckt, cvt, clcs,
                               page_table, group=group)
        sample_states.append(ss)
    ps = [jnp.stack(t) for t in zip(*prompt_states)]
    ss = [jnp.stack(t) for t in zip(*sample_states)]
    return (xp, xs, *ps, *ss)


def kernel(x_prompt, x_sample, cache_k, cache_v, cache_logf, page_table, state_conv_a, state_conv_b, state_delta,
           w_in, conv_a_w, conv_a_b, ln_a_g, ln_a_b, conv_b_w, a_log, dt_bias, norm_b_g, f_bias, w_out, g_pre_mix,
           g_post_mix, g_pre_ffn, g_post_ffn, w_gate, w_up, w_down):
    weights = (w_in, conv_a_w, conv_a_b, ln_a_g, ln_a_b, conv_b_w, a_log, dt_bias, norm_b_g, f_bias, w_out,
               g_pre_mix, g_post_mix, g_pre_ffn, g_post_ffn, w_gate, w_up, w_down)
    return _forward(x_prompt, x_sample, cache_k, cache_v, cache_logf, page_table, state_conv_a, state_conv_b,
                    state_delta, weights, tm=512, tq=512, tt_a=512, tt_b=256, group=16)
```

```python
import functools

import jax
import jax.numpy as jnp
from jax import lax
from jax.experimental import pallas as pl
from jax.experimental.pallas import tpu as pltpu

F32 = jnp.float32
BF16 = jnp.bfloat16

HEAD_DIM = 64
N_HEADS = 6
N_PAIRS = N_HEADS // 2
HW = N_HEADS * HEAD_DIM
CONV_A_WIDTH = 256
CONV_A_KERNEL = 31
SHORT_CONV = 4
DELTA_CHUNK = 64
PAGE_SIZE = 128
LANES = 128
SUBLANES = 8
RMS_EPS = 1e-6
LN_EPS = 1e-5
VMEM_LIMIT_BYTES = 56 * 1024 * 1024
NEG_INF = float("-inf")
LOG2E = 1.4426950408889634

_NT = (((1,), (1,)), ((), ()))


def _dot(a, b):
    return jnp.dot(a, b, preferred_element_type=F32)


def _dot_nt(a, b):
    return lax.dot_general(a, b, _NT, preferred_element_type=F32)


def _sigmoid(x):
    return 1.0 / (1.0 + jnp.exp(-x))


def _softplus(x):
    return jnp.maximum(x, 0.0) + jnp.log(1.0 + jnp.exp(-jnp.abs(x)))


def _silu(x):
    return x * _sigmoid(x)


def _rms(x, g):
    return x * lax.rsqrt(jnp.mean(x * x, axis=-1, keepdims=True) + RMS_EPS) * g


def _params(sem):
    return pltpu.CompilerParams(dimension_semantics=sem, vmem_limit_bytes=VMEM_LIMIT_BYTES)


def _iota(shape, dim):
    return lax.broadcasted_iota(jnp.int32, shape, dim)


def _split_hi_lo(x):
    hi = x.astype(BF16)
    lo = (x - hi.astype(F32)).astype(BF16)
    return hi, lo


def _split3(x):
    hi = x.astype(BF16)
    r = x - hi.astype(F32)
    mid = r.astype(BF16)
    lo = (r - mid.astype(F32)).astype(BF16)
    return hi, mid, lo


def _group_sum(x2, gmat):
    hi, lo = _split_hi_lo(x2)
    return _dot(hi, gmat) + _dot(lo, gmat)


_MAIN_SPLITS = (0, 256, 512, 1664, 2048, 2432)


def _gate_fns(raw, add, a_log):
    x = raw + add
    logf = -_softplus(-x)
    beta = _sigmoid(x)
    g = -jnp.exp(a_log) * _softplus(x)
    return logf, beta, g


_VX_ROWS = 80
_BIAS_ROWS = 3
_FOX_KEY_SPLIT = 2
_FOX_Q_BLOCK = 128


def _inproj_kernel(x_ref, g_ref, wmain_ref, wq_ref, wkv_ref, wsm_ref, addc_ref, alogc_ref, addr_ref, alogr_ref,
                   u_ref, qkvb_ref, zb_ref, *out_refs, decode):
    hb = _rms(x_ref[...], g_ref[...]).astype(BF16)
    tm = hb.shape[0]

    def proj(i):
        return _dot_nt(hb, wmain_ref[_MAIN_SPLITS[i]:_MAIN_SPLITS[i + 1], :])

    u_ref[...] = proj(0) * _sigmoid(proj(1))
    qkvb_ref[...] = proj(2)
    zb_ref[...] = proj(3)
    kv = _dot_nt(wkv_ref[...], hb)
    if decode:
        q_ref, kt_ref, vt_ref, smt_ref, smr_ref = out_refs
        q_ref[...] = _dot_nt(hb, wq_ref[...]) * (HEAD_DIM ** -0.5)
    else:
        kr_ref, qxt_ref, vx_ref, kt_ref, vt_ref, smt_ref, smr_ref = out_refs
        kr_ref[...] = proj(4).astype(BF16)
        qt = _dot_nt(wq_ref[...], hb) * (HEAD_DIM ** -0.5 * LOG2E)
        ones_rows = jnp.where(_iota((LANES - HEAD_DIM, tm), 0) < _BIAS_ROWS, 1.0, 0.0).astype(BF16)
        for h in range(N_HEADS):
            qxt_ref[0, h * LANES:h * LANES + HEAD_DIM, :] = qt[h * HEAD_DIM:(h + 1) * HEAD_DIM].astype(BF16)
            qxt_ref[0, h * LANES + HEAD_DIM:(h + 1) * LANES, :] = ones_rows
            vx_ref[0, h * _VX_ROWS:h * _VX_ROWS + HEAD_DIM, :] = kv[HW + h * HEAD_DIM:HW + (h + 1) * HEAD_DIM].astype(BF16)
            vx_ref[0, h * _VX_ROWS + HEAD_DIM:(h + 1) * _VX_ROWS, :] = jnp.ones((_VX_ROWS - HEAD_DIM, tm), BF16)
    kt_ref[0] = kv[:HW]
    vt_ref[0] = kv[HW:]

    rawt = _dot_nt(wsm_ref[0:24, :], hb)
    logf, beta, g = _gate_fns(rawt, addc_ref[...], alogc_ref[...])
    smt_ref[0, 0:8] = logf[0:8]
    smt_ref[0, 8:16] = beta[8:16]
    smt_ref[0, 16:24] = g[16:24]

    rawr = _dot_nt(hb, wsm_ref[...])
    logf, beta, g = _gate_fns(rawr, addr_ref[...], alogr_ref[...])
    col = _iota((1, LANES), 1)
    smr_ref[...] = jnp.where(col < 8, logf, jnp.where(col < 16, beta, g))


def _inproj(x, g, wmain, wq, wkv, wsm, addc, alogc, addr, alogr, *, nb, tm, decode):
    m, d = x.shape
    t = m // nb
    nt = t // tm
    row = lambda i: (i, 0)
    const = lambda i: (0, 0)
    tr = lambda i: (i // nt, 0, i % nt)
    outs = [jax.ShapeDtypeStruct((m, CONV_A_WIDTH), F32), jax.ShapeDtypeStruct((m, 3 * HW), F32),
            jax.ShapeDtypeStruct((m, HW), F32)]
    out_specs = [pl.BlockSpec((tm, CONV_A_WIDTH), row), pl.BlockSpec((tm, 3 * HW), row), pl.BlockSpec((tm, HW), row)]
    if decode:
        outs += [jax.ShapeDtypeStruct((m, HW), F32)]
        out_specs += [pl.BlockSpec((tm, HW), row)]
    else:
        outs += [jax.ShapeDtypeStruct((m, HW), BF16), jax.ShapeDtypeStruct((nb, N_HEADS * LANES, t), BF16),
                 jax.ShapeDtypeStruct((nb, N_HEADS * _VX_ROWS, t), BF16)]
        out_specs += [pl.BlockSpec((tm, HW), row), pl.BlockSpec((1, N_HEADS * LANES, tm), tr),
                      pl.BlockSpec((1, N_HEADS * _VX_ROWS, tm), tr)]
    outs += [jax.ShapeDtypeStruct((nb, HW, t), F32), jax.ShapeDtypeStruct((nb, HW, t), F32),
             jax.ShapeDtypeStruct((nb, 24, t), F32), jax.ShapeDtypeStruct((m, LANES), F32)]
    out_specs += [pl.BlockSpec((1, HW, tm), tr), pl.BlockSpec((1, HW, tm), tr),
                  pl.BlockSpec((1, 24, tm), tr), pl.BlockSpec((tm, LANES), row)]
    in_specs = [
        pl.BlockSpec((tm, d), row), pl.BlockSpec((1, d), const),
        pl.BlockSpec(wmain.shape, const), pl.BlockSpec(wq.shape, const), pl.BlockSpec(wkv.shape, const),
        pl.BlockSpec(wsm.shape, const),
        pl.BlockSpec((24, 1), const), pl.BlockSpec((24, 1), const),
        pl.BlockSpec((1, LANES), const), pl.BlockSpec((1, LANES), const),
    ]
    return pl.pallas_call(
        functools.partial(_inproj_kernel, decode=decode), grid=(m // tm,), in_specs=in_specs, out_specs=out_specs,
        out_shape=outs, compiler_params=_params(("arbitrary",)), name="inproj",
    )(x, g, wmain, wq, wkv, wsm, addc, alogc, addr, alogr)


_CA_PAD = 32
_CA_ROWS = 64


def _conv_a_kernel(u_ref, buf_ref, w_ref, b_ref, lg_ref, lb_ref, y_ref, nbuf_ref, ext_ref, *scratch, tt):
    ti = pl.program_id(1)

    @pl.when(ti == 0)
    def _():
        ext_ref[0:_CA_PAD] = buf_ref[0]

    ext_ref[_CA_PAD:_CA_PAD + tt] = u_ref[0]
    off = _CA_PAD - (CONV_A_KERNEL - 1)
    rows = min(_CA_ROWS, tt)
    aligned = tt % SUBLANES == 0
    if aligned:
        (sh_ref,) = scratch
        span = tt + _CA_PAD - SUBLANES
        for r in range(1, SUBLANES):
            sh_ref[r - 1] = ext_ref[r:r + span]
    for c in range(tt // rows):
        acc = jnp.zeros((rows, CONV_A_WIDTH), F32)
        for j in range(CONV_A_KERNEL):
            o = off + j + c * rows
            if aligned and o % SUBLANES:
                tap = sh_ref[o % SUBLANES - 1, o - o % SUBLANES:o - o % SUBLANES + rows, :]
            else:
                tap = ext_ref[o:o + rows, :]
            acc = acc + w_ref[j:j + 1, :] * tap
        ca = acc + b_ref[...]
        mu = jnp.mean(ca, axis=-1, keepdims=True)
        xc = ca - mu
        var = jnp.mean(xc * xc, axis=-1, keepdims=True)
        y_ref[0, c * rows:(c + 1) * rows, :] = _silu(xc * lax.rsqrt(var + LN_EPS) * lg_ref[...] + lb_ref[...])
    tail = ext_ref[tt:tt + _CA_PAD]
    nbuf_ref[0] = tail
    ext_ref[0:_CA_PAD] = tail


def _conv_a(u, buf, w, b, lg, lb, *, tt):
    nb, t, c = u.shape
    const = lambda bi, ti: (0, 0)
    return pl.pallas_call(
        functools.partial(_conv_a_kernel, tt=tt),
        grid=(nb, t // tt),
        in_specs=[pl.BlockSpec((1, tt, c), lambda bi, ti: (bi, ti, 0)),
                  pl.BlockSpec((1, _CA_PAD, c), lambda bi, ti: (bi, 0, 0)),
                  pl.BlockSpec((_CA_PAD, c), const), pl.BlockSpec((1, c), const),
                  pl.BlockSpec((1, c), const), pl.BlockSpec((1, c), const)],
        out_specs=[pl.BlockSpec((1, tt, c), lambda bi, ti: (bi, ti, 0)),
                   pl.BlockSpec((1, _CA_PAD, c), lambda bi, ti: (bi, 0, 0))],
        out_shape=[jax.ShapeDtypeStruct((nb, t, c), F32), jax.ShapeDtypeStruct((nb, _CA_PAD, c), F32)],
        scratch_shapes=[pltpu.VMEM((_CA_PAD + tt, c), F32)]
        + ([pltpu.VMEM((SUBLANES - 1, tt + _CA_PAD - SUBLANES, c), F32)] if tt % SUBLANES == 0 else []),
        compiler_params=_params(("arbitrary", "arbitrary")), name="conv_a",
    )(u, buf, w, b, lg, lb)


_CB_PAD = 8


def _pair_cols(x, p):
    return x[:, p * LANES:(p + 1) * LANES]


def _head_cols(sm, base, p, lane_lo):
    c0 = sm[:, base + 2 * p:base + 2 * p + 1]
    c1 = sm[:, base + 2 * p + 1:base + 2 * p + 2]
    return jnp.where(lane_lo, c0, c1)


def _block_diag(x, lane_lo):
    return jnp.concatenate([jnp.where(lane_lo, x, 0.0), jnp.where(lane_lo, 0.0, x)], axis=0)


def _delta_kernel(x_ref, buf_ref, w_ref, smr_ref, smt_ref, z_ref, gb_ref, s0_ref, gmat_ref, tril_ref, triu_ref,
                  y_ref, nbuf_ref, sout_ref, ext_ref, s_ref, *, tt):
    ti = pl.program_id(1)
    nchunk = tt // DELTA_CHUNK
    c = DELTA_CHUNK

    @pl.when(ti == 0)
    def _():
        ext_ref[0:_CB_PAD] = buf_ref[0]
        s_ref[...] = s0_ref[0]

    ext_ref[_CB_PAD:_CB_PAD + tt] = x_ref[0]
    off = _CB_PAD - (SHORT_CONV - 1)
    acc = w_ref[0:1, :] * ext_ref[off:off + tt, :]
    for j in range(1, SHORT_CONV):
        acc = acc + w_ref[j:j + 1, :] * ext_ref[off + j:off + j + tt, :]
    tail = ext_ref[tt:tt + _CB_PAD]
    nbuf_ref[0] = tail
    ext_ref[0:_CB_PAD] = tail
    cb = _silu(acc)

    gmat = gmat_ref[...]
    q = cb[:, 0:HW]
    k = cb[:, HW:2 * HW]
    v = cb[:, 2 * HW:3 * HW]
    qs = q * lax.rsqrt(_group_sum(q * q, gmat) + 1e-6) * (HEAD_DIM ** -0.5)
    kn = k * lax.rsqrt(_group_sum(k * k, gmat) + 1e-6)

    smr = smr_ref[0]
    smt = smt_ref[0]
    s_hi, s_mid, s_lo = _split3(smr)
    gc_col = _dot(tril_ref[...], s_hi) + _dot(tril_ref[...], s_mid) + _dot(tril_ref[...], s_lo)
    t_hi, t_mid, t_lo = _split3(smt[16:24])
    gc_row = _dot(t_hi, triu_ref[...]) + _dot(t_mid, triu_ref[...]) + _dot(t_lo, triu_ref[...])
    if tt < LANES:
        gc_row = jnp.concatenate([gc_row, jnp.zeros((8, LANES - tt), F32)], axis=1)

    lane = _iota((1, LANES), 1)
    lane_lo = lane < HEAD_DIM
    ri = _iota((c, LANES), 0)
    ci_ = _iota((c, LANES), 1)
    cj = jnp.where(ci_ < HEAD_DIM, ci_, ci_ - HEAD_DIM)
    causal = ri >= cj
    strict = ri > cj
    eye2 = jnp.where(ri == cj, 1.0, 0.0)
    r128 = _iota((LANES, LANES), 0)
    c128 = _iota((LANES, LANES), 1)
    bdmask = (r128 < HEAD_DIM) == (c128 < HEAD_DIM)

    combos = [(ci, p) for ci in range(nchunk) for p in range(N_PAIRS)]
    pre = []
    for ci, p in combos:
        r0, r1 = ci * c, (ci + 1) * c
        qs_p = _pair_cols(qs, p)[r0:r1]
        kn_p = _pair_cols(kn, p)[r0:r1]
        v_p = _pair_cols(v, p)[r0:r1]
        beta = _head_cols(smr[r0:r1], 8, p, lane_lo)
        gi = _head_cols(gc_col[r0:r1], 16, p, lane_lo)
        blk = gc_row[:, (ci // 2) * LANES:(ci // 2 + 1) * LANES]
        rot = pltpu.roll(blk, HEAD_DIM, 1)
        if ci % 2 == 0:
            gj = jnp.where(lane_lo, blk[2 * p:2 * p + 1], rot[2 * p + 1:2 * p + 2])
        else:
            gj = jnp.where(lane_lo, rot[2 * p:2 * p + 1], blk[2 * p + 1:2 * p + 2])
        glast = gi[c - 1:c, :]
        eg = jnp.exp(gi)
        decay = jnp.exp(jnp.where(causal, gi - gj, NEG_INF))
        kb = kn_p * beta
        pre.append(dict(qs=qs_p, kn=kn_p, vb=v_p * beta, kb=kb, kbe=kb * eg, qe=qs_p * eg, eg_last=jnp.exp(glast),
                        kdec=kn_p * jnp.exp(glast - gi), decay=decay, kd=_block_diag(kn_p, lane_lo)))

    grams = [_dot_nt(jnp.concatenate([d["kb"], d["qs"]], axis=0), d["kd"]) for d in pre]
    lmats = [jnp.where(strict, g[0:c] * d["decay"], 0.0) for g, d in zip(grams, pre)]
    qks = [jnp.where(causal, g[c:2 * c] * d["decay"], 0.0) for g, d in zip(grams, pre)]

    def split_bd(m):
        hi, lo = _split_hi_lo(m)
        return hi, lo, _block_diag(hi, lane_lo), _block_diag(lo, lane_lo)

    def dot3(a_hi, a_lo, b_hi, b_lo):
        return _dot(a_hi, b_hi) + _dot(a_lo, b_hi) + _dot(a_hi, b_lo)

    xs = [eye2 - l for l in lmats]
    pws = [dot3(*split_bd(l)) for l in lmats]
    for level in range(4):
        psp = [split_bd(pw) for pw in pws]
        xsp = [_split_hi_lo(x) for x in xs]
        prods = [dot3(jnp.concatenate([ph, xh], axis=0), jnp.concatenate([plo, xl], axis=0), bh, bl)
                 for (ph, plo, bh, bl), (xh, xl) in zip(psp, xsp)]
        xs = [x + pr[c:2 * c] for x, pr in zip(xs, prods)]
        pws = [pr[0:c] for pr in prods]
    xs = [x + dot3(*_split_hi_lo(x), *split_bd(pw)[2:]) for pw, x in zip(pws, xs)]

    rhs_uw = [jnp.concatenate([_block_diag(d["vb"], lane_lo), _block_diag(d["kbe"], lane_lo)], axis=1) for d in pre]
    uws = [_dot(x, r) for x, r in zip(xs, rhs_uw)]
    kts = [d["kdec"].T for d in pre]
    kuws = [_dot(kt, uw) for kt, uw in zip(kts, uws)]
    kus = [jnp.where(bdmask, m[:, 0:LANES], 0.0) for m in kuws]
    kws = [jnp.where(bdmask, m[:, LANES:2 * LANES], 0.0) for m in kuws]
    rhs_q = [jnp.concatenate([_block_diag(uw[:, 0:LANES], lane_lo), _block_diag(uw[:, LANES:2 * LANES], lane_lo)],
                             axis=1) for uw in uws]
    qkuw = [_dot(qk, r) for qk, r in zip(qks, rhs_q)]
    o2s = [m[:, 0:LANES] for m in qkuw]
    q2s = [d["qe"] - m[:, LANES:2 * LANES] for d, m in zip(pre, qkuw)]

    states = [s_ref[p] for p in range(N_PAIRS)]
    o_chunks = []
    for ci in range(nchunk):
        o_pairs = []
        for p in range(N_PAIRS):
            i = ci * N_PAIRS + p
            s_bd = states[p]
            prod = _dot(jnp.concatenate([q2s[i], kws[i]], axis=0), s_bd)
            o_pairs.append(prod[0:c] + o2s[i])
            states[p] = s_bd * pre[i]["eg_last"] - prod[c:c + LANES] + kus[i]
        o_chunks.append(jnp.concatenate(o_pairs, axis=1))
    for p in range(N_PAIRS):
        s_ref[p] = states[p]
    o = jnp.concatenate(o_chunks, axis=0) if nchunk > 1 else o_chunks[0]
    ms = _group_sum(o * o, gmat) * (1.0 / HEAD_DIM)
    y_ref[0] = o * lax.rsqrt(ms + RMS_EPS) * gb_ref[...] * _silu(z_ref[0])
    sout_ref[0] = s_ref[...]


def _delta(x, buf, w, smr, smt, z, gb, s0, *, tt):
    nb, t, cw = x.shape
    const = lambda bi, ti: (0, 0)
    head = jnp.arange(HW) // HEAD_DIM
    gmat = (head[:, None] == head[None, :]).astype(BF16)
    chunk = jnp.arange(tt) // DELTA_CHUNK
    same = chunk[:, None] == chunk[None, :]
    pos = jnp.arange(tt)
    tril = (same & (pos[:, None] >= pos[None, :])).astype(BF16)
    triu = tril.T
    return pl.pallas_call(
        functools.partial(_delta_kernel, tt=tt),
        grid=(nb, t // tt),
        in_specs=[pl.BlockSpec((1, tt, cw), lambda bi, ti: (bi, ti, 0)),
                  pl.BlockSpec((1, _CB_PAD, cw), lambda bi, ti: (bi, 0, 0)),
                  pl.BlockSpec((SHORT_CONV, cw), const),
                  pl.BlockSpec((1, tt, LANES), lambda bi, ti: (bi, ti, 0)),
                  pl.BlockSpec((1, 24, tt), lambda bi, ti: (bi, 0, ti)),
                  pl.BlockSpec((1, tt, HW), lambda bi, ti: (bi, ti, 0)),
                  pl.BlockSpec((1, HW), const),
                  pl.BlockSpec((1, N_PAIRS, LANES, LANES), lambda bi, ti: (bi, 0, 0, 0)),
                  pl.BlockSpec((HW, HW), const), pl.BlockSpec((tt, tt), const), pl.BlockSpec((tt, tt), const)],
        out_specs=[pl.BlockSpec((1, tt, HW), lambda bi, ti: (bi, ti, 0)),
                   pl.BlockSpec((1, _CB_PAD, cw), lambda bi, ti: (bi, 0, 0)),
                   pl.BlockSpec((1, N_PAIRS, LANES, LANES), lambda bi, ti: (bi, 0, 0, 0))],
        out_shape=[jax.ShapeDtypeStruct((nb, t, HW), F32), jax.ShapeDtypeStruct((nb, _CB_PAD, cw), F32),
                   jax.ShapeDtypeStruct((nb, N_PAIRS, LANES, LANES), F32)],
        scratch_shapes=[pltpu.VMEM((_CB_PAD + tt, cw), F32), pltpu.VMEM((N_PAIRS, LANES, LANES), F32)],
        compiler_params=_params(("arbitrary", "arbitrary")), name="delta",
    )(x, buf, w, smr, smt, z, gb, s0, gmat, tril, triu)


def _fox_keys_kernel(smr_ref, kr_ref, tril_ref, pk_ref, pb_ref, kx_ref, carry_ref):
    ti = pl.program_id(1)

    @pl.when(ti == 0)
    def _():
        carry_ref[...] = jnp.zeros_like(carry_ref)

    l_hi, l_mid, l_lo = _split3(smr_ref[0])
    tril = tril_ref[...]
    cs = _dot(tril, l_hi) + _dot(tril, l_mid) + _dot(tril, l_lo) + carry_ref[0:1, :]
    tl = cs.shape[0]
    carry_ref[...] = jnp.broadcast_to(cs[tl - 1:tl, :], carry_ref.shape)
    hi, mid, lo = _split3(jnp.where(_iota((1, LANES), 1) < N_HEADS, cs * (-LOG2E), 0.0))
    kx = (_dot(kr_ref[0], pk_ref[...]) + _dot(hi, pb_ref[0]) + _dot(mid, pb_ref[1]) + _dot(lo, pb_ref[2]))
    kx_ref[0] = kx.astype(BF16)


def _fox_keys(smr, kr, *, tl):
    nb, t, _ = smr.shape
    pos = jnp.arange(tl)
    tril = (pos[:, None] >= pos[None, :]).astype(BF16)
    src = jnp.arange(HW)
    dst = (src // HEAD_DIM) * LANES + src % HEAD_DIM
    pk = (dst[:, None] == jnp.arange(N_HEADS * LANES)[None, :]).astype(BF16)
    col = jnp.arange(LANES)
    pb = jnp.stack([((col[:, None] < N_HEADS)
                     & (col[:, None] * LANES + HEAD_DIM + piece == jnp.arange(N_HEADS * LANES)[None, :])).astype(BF16)
                    for piece in range(_BIAS_ROWS)])
    const2 = lambda bi, ti: (0, 0)
    return pl.pallas_call(
        _fox_keys_kernel, grid=(nb, t // tl),
        in_specs=[pl.BlockSpec((1, tl, LANES), lambda bi, ti: (bi, ti, 0)),
                  pl.BlockSpec((1, tl, HW), lambda bi, ti: (bi, ti, 0)),
                  pl.BlockSpec((tl, tl), const2), pl.BlockSpec(pk.shape, const2),
                  pl.BlockSpec(pb.shape, lambda bi, ti: (0, 0, 0))],
        out_specs=pl.BlockSpec((1, tl, N_HEADS * LANES), lambda bi, ti: (bi, ti, 0)),
        out_shape=jax.ShapeDtypeStruct((nb, t, N_HEADS * LANES), BF16),
        scratch_shapes=[pltpu.VMEM((8, LANES), F32)],
        compiler_params=_params(("arbitrary", "arbitrary")), name="fox_keys",
    )(smr, kr, tril, pk, pb)


def _fox_kernel(qi_ref, ki_ref, kx_ref, qxt_ref, vx_ref, o_ref, m_ref, acc_ref, *, tq, tk):
    step = pl.program_id(1)
    qi = qi_ref[step]
    ki = ki_ref[step]

    @pl.when(ki == 0)
    def _():
        m_ref[...] = jnp.full_like(m_ref, NEG_INF)
        acc_ref[...] = jnp.zeros_like(acc_ref)

    def update(masked):
        if masked:
            keep = _iota((tk, tq), 0) <= _iota((tk, tq), 1)
        combos = [(h, qc) for h in range(N_HEADS) for qc in range(tq // _FOX_Q_BLOCK)]
        half = tk // _FOX_KEY_SPLIT
        ms = [m_ref[h:h + 1, qc * _FOX_Q_BLOCK:(qc + 1) * _FOX_Q_BLOCK] for h, qc in combos]
        accs = [acc_ref[h, :, qc * _FOX_Q_BLOCK:(qc + 1) * _FOX_Q_BLOCK] for h, qc in combos]
        for kh in range(_FOX_KEY_SPLIT):
            k0 = kh * half
            sts = [_dot(kx_ref[0, k0:k0 + half, h * LANES:(h + 1) * LANES],
                        qxt_ref[0, h * LANES:(h + 1) * LANES, qc * _FOX_Q_BLOCK:(qc + 1) * _FOX_Q_BLOCK]) for h, qc in combos]
            if masked:
                sts = [jnp.where(keep[k0:k0 + half, qc * _FOX_Q_BLOCK:(qc + 1) * _FOX_Q_BLOCK], st, NEG_INF)
                       for st, (h, qc) in zip(sts, combos)]
            m_news = [jnp.maximum(mp, jnp.max(st, axis=0, keepdims=True)) for mp, st in zip(ms, sts)]
            pts = [jnp.exp2(st - mn).astype(BF16) for st, mn in zip(sts, m_news)]
            accs = [jnp.exp2(mp - mn) * acc + _dot(vx_ref[0, h * _VX_ROWS:(h + 1) * _VX_ROWS, k0:k0 + half], pt)
                    for (h, qc), mp, mn, acc, pt in zip(combos, ms, m_news, accs, pts)]
            ms = m_news
        for (h, qc), mn, acc in zip(combos, ms, accs):
            acc_ref[h, :, qc * _FOX_Q_BLOCK:(qc + 1) * _FOX_Q_BLOCK] = acc
            m_ref[h:h + 1, qc * _FOX_Q_BLOCK:(qc + 1) * _FOX_Q_BLOCK] = mn

    @pl.when(ki < qi)
    def _():
        update(False)

    @pl.when(ki == qi)
    def _():
        update(True)
        for h in range(N_HEADS):
            acc = acc_ref[h]
            o_ref[0, h * HEAD_DIM:(h + 1) * HEAD_DIM, :] = acc[0:HEAD_DIM] / acc[HEAD_DIM:HEAD_DIM + 1]


def _fox_prompt(kx, qxt, vx, *, tq):
    nb, t, _ = kx.shape
    nq = t // tq
    qi_tab = jnp.asarray([qi for qi in range(nq) for _ in range(qi + 1)], jnp.int32)
    ki_tab = jnp.asarray([ki for qi in range(nq) for ki in range(qi + 1)], jnp.int32)
    grid_spec = pltpu.PrefetchScalarGridSpec(
        num_scalar_prefetch=2, grid=(nb, int(qi_tab.shape[0])),
        in_specs=[pl.BlockSpec((1, tq, N_HEADS * LANES), lambda b, s, qt, kt: (b, kt[s], 0)),
                  pl.BlockSpec((1, N_HEADS * LANES, tq), lambda b, s, qt, kt: (b, 0, qt[s])),
                  pl.BlockSpec((1, N_HEADS * _VX_ROWS, tq), lambda b, s, qt, kt: (b, 0, kt[s]))],
        out_specs=pl.BlockSpec((1, HW, tq), lambda b, s, qt, kt: (b, 0, qt[s])),
        scratch_shapes=[pltpu.VMEM((8, tq), F32), pltpu.VMEM((N_HEADS, _VX_ROWS, tq), F32)],
    )
    return pl.pallas_call(
        functools.partial(_fox_kernel, tq=tq, tk=tq), grid_spec=grid_spec,
        out_shape=jax.ShapeDtypeStruct((nb, HW, t), F32),
        compiler_params=_params(("arbitrary", "arbitrary")), name="fox_prompt",
    )(qi_tab, ki_tab, kx, qxt, vx)


def _page_cumsum_kernel(x_ref, triu_ref, o_ref):
    hi, mid, lo = _split3(x_ref[...])
    triu = triu_ref[...]
    o_ref[...] = _dot(hi, triu) + _dot(mid, triu) + _dot(lo, triu)


def _page_cumsum(lf, *, rows):
    r = lf.shape[0]
    pos = jnp.arange(PAGE_SIZE)
    triu = (pos[:, None] <= pos[None, :]).astype(BF16)
    return pl.pallas_call(
        _page_cumsum_kernel, grid=(r // rows,),
        in_specs=[pl.BlockSpec((rows, PAGE_SIZE), lambda i: (i, 0)),
                  pl.BlockSpec((PAGE_SIZE, PAGE_SIZE), lambda i: (0, 0))],
        out_specs=pl.BlockSpec((rows, PAGE_SIZE), lambda i: (i, 0)),
        out_shape=jax.ShapeDtypeStruct((r, PAGE_SIZE), F32),
        compiler_params=_params(("arbitrary",)), name="page_cumsum",
    )(lf, triu)


def _fox_decode_kernel(pt_ref, qx_ref, kn_ref, vn_ref, lfn_ref, kt_hbm, vt_hbm, lcs_hbm, o_ref,
                       kbuf, vbuf, lbuf, sem, qbd_ref, m_ref, l_ref, acc_ref, carry_ref, *, layer, n_steps, group):
    b = pl.program_id(0)
    j = pl.program_id(1)
    step = b * n_steps + j
    slot = lax.rem(step, 2)
    total = pl.num_programs(0) * n_steps

    def page_copies(bb, jj, to_slot, lookup):
        out = []
        for g in range(group):
            page = pt_ref[bb, jj * group + g] if lookup else 0
            out.append(pltpu.make_async_copy(kt_hbm.at[layer, page], kbuf.at[to_slot, g], sem.at[0, to_slot]))
            out.append(pltpu.make_async_copy(vt_hbm.at[layer, page], vbuf.at[to_slot, g], sem.at[1, to_slot]))
            out.append(pltpu.make_async_copy(lcs_hbm.at[layer, page], lbuf.at[to_slot, g], sem.at[2, to_slot]))
        return out

    @pl.when(step == 0)
    def _():
        for cp in page_copies(0, 0, 0, True):
            cp.start()

    @pl.when(step + 1 < total)
    def _():
        last = j == n_steps - 1
        for cp in page_copies(jnp.where(last, b + 1, b), jnp.where(last, 0, j + 1), 1 - slot, True):
            cp.start()

    for cp in page_copies(0, 0, slot, False):
        cp.wait()
    kt_refs = [kbuf.at[slot, g] for g in range(group)]
    vt_refs = [vbuf.at[slot, g] for g in range(group)]
    lcs_refs = [lbuf.at[slot, g] for g in range(group)]

    @pl.when(j == 0)
    def _():
        row = _iota((8, HW), 0)
        col = _iota((8, HW), 1)
        own = (col >= row * HEAD_DIM) & (col < (row + 1) * HEAD_DIM)
        qbd_ref[...] = jnp.where(own, qx_ref[0], 0.0)
        m_ref[...] = jnp.full_like(m_ref, NEG_INF)
        l_ref[...] = jnp.zeros_like(l_ref)
        acc_ref[...] = jnp.zeros_like(acc_ref)
        carry_ref[...] = jnp.zeros_like(carry_ref)

    qbd = qbd_ref[...]
    lcs = [lcs_refs[g][...] for g in range(group)]
    totals = [jnp.broadcast_to(x[:, PAGE_SIZE - 1:PAGE_SIZE], x.shape) for x in lcs]
    qk = [_dot(qbd, kt_refs[g][...]) for g in range(group)]
    c_run = carry_ref[...]
    scores = []
    for g in range(group):
        scores.append(qk[g] - (lcs[g] + c_run))
        c_run = c_run + totals[g]
    c_end = c_run[:, 0:1]
    s = jnp.concatenate(scores, axis=1)
    m_prev = m_ref[:, 0:1]
    m_new = jnp.maximum(m_prev, jnp.max(s, axis=-1, keepdims=True))
    alpha = jnp.exp(m_prev - m_new)
    pr = jnp.exp(s - m_new)
    l_new = alpha * l_ref[:, 0:1] + jnp.sum(pr, axis=-1, keepdims=True)
    pv = _dot_nt(pr[:, 0:PAGE_SIZE], vt_refs[0][...])
    for g in range(1, group):
        pv = pv + _dot_nt(pr[:, g * PAGE_SIZE:(g + 1) * PAGE_SIZE], vt_refs[g][...])
    acc_new = alpha * acc_ref[...] + pv
    m_ref[...] = jnp.broadcast_to(m_new, m_ref.shape)
    l_ref[...] = jnp.broadcast_to(l_new, l_ref.shape)
    acc_ref[...] = acc_new
    carry_ref[...] = c_run

    @pl.when(j == n_steps - 1)
    def _():
        s_n = jnp.sum(qbd_ref[...] * kn_ref[0], axis=-1, keepdims=True) - (c_end + lfn_ref[0][:, 0:1])
        m_f = jnp.maximum(m_new, s_n)
        a_f = jnp.exp(m_new - m_f)
        p_n = jnp.exp(s_n - m_f)
        l_f = a_f * l_new + p_n
        o8 = (a_f * acc_new + p_n * vn_ref[0]) / l_f
        row = _iota((8, HW), 0)
        col = _iota((8, HW), 1)
        own = (col >= row * HEAD_DIM) & (col < (row + 1) * HEAD_DIM)
        o_ref[0] = jnp.sum(jnp.where(own, o8, 0.0), axis=0, keepdims=True)


def _fox_decode(page_table, layer, qx, cache_kt, cache_vt, cache_lcs, k_new, v_new, lf_new, *, group):
    bd = qx.shape[0]
    n_pages = page_table.shape[1]
    n_steps = n_pages // group
    per_b = lambda b, j, pt: (b, 0, 0)
    hbm = pl.BlockSpec(memory_space=pl.ANY)
    grid_spec = pltpu.PrefetchScalarGridSpec(
        num_scalar_prefetch=1, grid=(bd, n_steps),
        in_specs=[pl.BlockSpec((1, 1, HW), per_b), pl.BlockSpec((1, 1, HW), per_b), pl.BlockSpec((1, 1, HW), per_b),
                  pl.BlockSpec((1, 8, LANES), per_b), hbm, hbm, hbm],
        out_specs=pl.BlockSpec((1, 1, HW), per_b),
        scratch_shapes=[pltpu.VMEM((2, group, HW, PAGE_SIZE), F32), pltpu.VMEM((2, group, HW, PAGE_SIZE), F32),
                        pltpu.VMEM((2, group, 8, PAGE_SIZE), F32), pltpu.SemaphoreType.DMA((3, 2)),
                        pltpu.VMEM((8, HW), F32), pltpu.VMEM((8, LANES), F32), pltpu.VMEM((8, LANES), F32),
                        pltpu.VMEM((8, HW), F32), pltpu.VMEM((8, LANES), F32)],
    )
    return pl.pallas_call(
        functools.partial(_fox_decode_kernel, layer=layer, n_steps=n_steps, group=group), grid_spec=grid_spec,
        out_shape=jax.ShapeDtypeStruct((bd, 1, HW), F32),
        compiler_params=_params(("arbitrary", "arbitrary")), name="fox_decode",
    )(page_table, qx, k_new, v_new, lf_new, cache_kt, cache_vt, cache_lcs)


_FF_CHUNK = 704


def _outffn_kernel(x_ref, ya_ref, yb_ref, yc_ref, wo_ref, wg_ref, wu_ref, wd_ref, gpm_ref, gpf_ref, gqf_ref, o_ref,
                   *, yc_transposed):
    a0 = CONV_A_WIDTH
    a1 = a0 + HW
    yc = yc_ref[0].T if yc_transposed else yc_ref[...]
    mix = (_dot(ya_ref[...].astype(BF16), wo_ref[0:a0, :]) + _dot(yb_ref[...].astype(BF16), wo_ref[a0:a1, :])
           + _dot(yc.astype(BF16), wo_ref[a1:a1 + HW, :]))
    x1 = x_ref[...] + _rms(mix, gpm_ref[...])
    hb = _rms(x1, gpf_ref[...]).astype(BF16)
    d_ff = wg_ref.shape[1]
    f = jnp.zeros_like(x1)
    for c0 in range(0, d_ff, _FF_CHUNK):
        gate = _dot(hb, wg_ref[:, c0:c0 + _FF_CHUNK])
        up = _dot(hb, wu_ref[:, c0:c0 + _FF_CHUNK])
        f = f + _dot((_silu(gate) * up).astype(BF16), wd_ref[c0:c0 + _FF_CHUNK, :])
    o_ref[...] = x1 + _rms(f, gqf_ref[...])


def _outffn(x, ya, yb, yc, wo, wg, wu, wd, gpm, gpf, gqf, *, tm):
    m, d = x.shape
    row = lambda i: (i, 0)
    const = lambda i: (0, 0)
    whole = lambda a: pl.BlockSpec(a.shape, const, pipeline_mode=pl.Buffered(1))
    yc_transposed = yc.ndim == 3
    if yc_transposed:
        nt = yc.shape[2] // tm
        yc_spec = pl.BlockSpec((1, HW, tm), lambda i: (i // nt, 0, i % nt))
    else:
        yc_spec = pl.BlockSpec((tm, HW), row)
    return pl.pallas_call(
        functools.partial(_outffn_kernel, yc_transposed=yc_transposed), grid=(m // tm,),
        in_specs=[pl.BlockSpec((tm, d), row), pl.BlockSpec((tm, CONV_A_WIDTH), row),
                  pl.BlockSpec((tm, HW), row), yc_spec,
                  whole(wo), whole(wg), whole(wu), whole(wd),
                  pl.BlockSpec((1, d), const), pl.BlockSpec((1, d), const), pl.BlockSpec((1, d), const)],
        out_specs=pl.BlockSpec((tm, d), row),
        out_shape=jax.ShapeDtypeStruct((m, d), F32),
        compiler_params=_params(("arbitrary",)), name="outffn",
    )(x, ya, yb, yc, wo, wg, wu, wd, gpm, gpf, gqf)


def _layer_weights(l, w_in, conv_a_w, conv_a_b, ln_a_g, ln_a_b, conv_b_w, a_log, dt_bias, norm_b_g, f_bias,
                   w_out, g_pre_mix, g_post_mix, g_pre_ffn, g_post_ffn, w_gate, w_up, w_down):
    wt = w_in[l].T
    wmain = jnp.concatenate([wt[0:2048], wt[2444:2828]], axis=0).astype(BF16)
    wq = wt[2060:2444].astype(BF16)
    wkv = wt[2444:3212].astype(BF16)
    z2 = jnp.zeros((2, wt.shape[1]), F32)
    wsm = jnp.concatenate([wt[3212:3218], z2, wt[2048:2054], z2, wt[2054:2060], z2,
                           jnp.zeros((LANES - 24, wt.shape[1]), F32)], axis=0).astype(BF16)
    z2v = jnp.zeros((2,), F32)
    z8v = jnp.zeros((8,), F32)
    add24 = jnp.concatenate([f_bias[l], z2v, z8v, dt_bias[l], z2v])
    alog24 = jnp.concatenate([z8v, z8v, a_log[l], z2v])
    pad_row = lambda v: jnp.concatenate([v, jnp.zeros((LANES - 24,), F32)])[None, :]
    return dict(
        g_pre_mix=g_pre_mix[l][None, :], wmain=wmain, wq=wq, wkv=wkv, wsm=wsm,
        addc=add24[:, None], alogc=alog24[:, None], addr=pad_row(add24), alogr=pad_row(alog24),
        conv_a_w=jnp.concatenate([conv_a_w[l], jnp.zeros((1, CONV_A_WIDTH), F32)], axis=0),
        conv_a_b=conv_a_b[l][None, :], ln_a_g=ln_a_g[l][None, :], ln_a_b=ln_a_b[l][None, :],
        conv_b_w=conv_b_w[l], norm_b_g=jnp.tile(norm_b_g[l], N_HEADS)[None, :],
        w_out=w_out[l].astype(BF16), w_gate=w_gate[l].astype(BF16), w_up=w_up[l].astype(BF16),
        w_down=w_down[l].astype(BF16),
        g_post_mix=g_post_mix[l][None, :], g_pre_ffn=g_pre_ffn[l][None, :], g_post_ffn=g_post_ffn[l][None, :],
    )


def _state_to_pairs(s):
    nb = s.shape[0]
    s = s.reshape(nb, N_PAIRS, 2, HEAD_DIM, HEAD_DIM)
    z = jnp.zeros_like(s[:, :, 0])
    top = jnp.concatenate([s[:, :, 0], z], axis=-1)
    bot = jnp.concatenate([z, s[:, :, 1]], axis=-1)
    return jnp.concatenate([top, bot], axis=-2)


def _pairs_to_state(sp):
    nb = sp.shape[0]
    s0 = sp[:, :, :HEAD_DIM, :HEAD_DIM]
    s1 = sp[:, :, HEAD_DIM:, HEAD_DIM:]
    return jnp.stack([s0, s1], axis=2).reshape(nb, N_HEADS, HEAD_DIM, HEAD_DIM)


def _pad_rows_front(buf, rows):
    nb, r, c = buf.shape
    return jnp.concatenate([jnp.zeros((nb, rows - r, c), buf.dtype), buf], axis=1)


def _prompt_layer(x, p, *, tm, tq, tt_a, tt_b):
    nb, t, d = x.shape
    xf = x.reshape(nb * t, d)
    u, qkvb, zb, kr, qxt, vx, kt, vt, smt, smr = _inproj(
        xf, p["g_pre_mix"], p["wmain"], p["wq"], p["wkv"], p["wsm"], p["addc"], p["alogc"], p["addr"], p["alogr"],
        nb=nb, tm=tm, decode=False)
    ya, nbuf_a = _conv_a(u.reshape(nb, t, -1), jnp.zeros((nb, _CA_PAD, CONV_A_WIDTH), F32), p["conv_a_w"],
                         p["conv_a_b"], p["ln_a_g"], p["ln_a_b"], tt=tt_a)
    yb, nbuf_b, s_new = _delta(qkvb.reshape(nb, t, -1), jnp.zeros((nb, _CB_PAD, 3 * HW), F32), p["conv_b_w"],
                               smr.reshape(nb, t, LANES), smt, zb.reshape(nb, t, HW), p["norm_b_g"],
                               jnp.zeros((nb, N_PAIRS, LANES, LANES), F32), tt=tt_b)
    kx = _fox_keys(smr.reshape(nb, t, LANES), kr.reshape(nb, t, HW), tl=tq)
    yc = _fox_prompt(kx, qxt, vx, tq=tq)
    y = _outffn(xf, ya.reshape(nb * t, -1), yb.reshape(nb * t, -1), yc, p["w_out"],
                p["w_gate"], p["w_up"], p["w_down"], p["g_post_mix"], p["g_pre_ffn"], p["g_post_ffn"], tm=tm)
    k_out = kt.reshape(nb, N_HEADS, HEAD_DIM, t).transpose(0, 3, 1, 2)
    v_out = vt.reshape(nb, N_HEADS, HEAD_DIM, t).transpose(0, 3, 1, 2)
    logf_out = smt[:, 0:N_HEADS, :].transpose(0, 2, 1)
    states = (k_out, v_out, logf_out, nbuf_a[:, _CA_PAD - (CONV_A_KERNEL - 1):],
              nbuf_b[:, _CB_PAD - (SHORT_CONV - 1):], _pairs_to_state(s_new))
    return y.reshape(nb, t, d), states


def _sample_layer(x, p, layer, buf_a, buf_b, s_delta, cache_kt, cache_vt, cache_lcs, page_table, *, group):
    bd, _, d = x.shape
    xf = x.reshape(bd, d)
    u, qkvb, zb, qx, kt, vt, smt, smr = _inproj(
        xf, p["g_pre_mix"], p["wmain"], p["wq"], p["wkv"], p["wsm"], p["addc"], p["alogc"], p["addr"], p["alogr"],
        nb=1, tm=bd, decode=True)
    ya, nbuf_a = _conv_a(u.reshape(bd, 1, -1), _pad_rows_front(buf_a, _CA_PAD), p["conv_a_w"], p["conv_a_b"],
                         p["ln_a_g"], p["ln_a_b"], tt=1)

    c = DELTA_CHUNK
    pad_t = lambda a: jnp.concatenate([a[:, None, :], jnp.zeros((bd, c - 1, a.shape[-1]), a.dtype)], axis=1)
    smr_b = pad_t(smr)
    smt_b = jnp.transpose(smr_b[:, :, 0:24], (0, 2, 1))
    yb, nbuf_b, s_new = _delta(pad_t(qkvb), _pad_rows_front(buf_b, _CB_PAD), p["conv_b_w"], smr_b, smt_b,
                               pad_t(zb), p["norm_b_g"], _state_to_pairs(s_delta), tt=c)
    yb = yb[:, 0, :]
    new_buf_b = jnp.concatenate([buf_b[:, 1:], qkvb[:, None, :]], axis=1)

    k_new = kt[0].T
    v_new = vt[0].T
    lf_new = smt[0, 0:8, :].T
    yc = _fox_decode(page_table, layer, qx[:, None, :], cache_kt, cache_vt, cache_lcs, k_new[:, None, :],
                     v_new[:, None, :], jnp.broadcast_to(lf_new[:, :, None], (bd, 8, LANES)), group=group)
    y = _outffn(xf, ya.reshape(bd, -1), yb, yc.reshape(bd, -1), p["w_out"], p["w_gate"], p["w_up"], p["w_down"],
                p["g_post_mix"], p["g_pre_ffn"], p["g_post_ffn"], tm=bd)
    states = (k_new.reshape(bd, 1, N_HEADS, HEAD_DIM), v_new.reshape(bd, 1, N_HEADS, HEAD_DIM),
              lf_new[:, None, 0:N_HEADS], nbuf_a[:, _CA_PAD - (CONV_A_KERNEL - 1):], new_buf_b,
              _pairs_to_state(s_new))
    return y.reshape(bd, 1, d), states


def _forward(x_prompt, x_sample, cache_k, cache_v, cache_logf, page_table, state_conv_a, state_conv_b,
             state_delta, weights, *, tm, tq, tt_a, tt_b, group):
    depth = cache_k.shape[0]
    n_phys = cache_k.shape[1]
    ckt = jnp.transpose(cache_k, (0, 1, 3, 4, 2)).reshape(depth, n_phys, HW, PAGE_SIZE)
    cvt = jnp.transpose(cache_v, (0, 1, 3, 4, 2)).reshape(depth, n_phys, HW, PAGE_SIZE)
    clf = jnp.transpose(cache_logf, (0, 1, 3, 2))
    clf = jnp.concatenate([clf, jnp.zeros((depth, n_phys, 8 - N_HEADS, PAGE_SIZE), F32)], axis=2)
    n_rows = depth * n_phys * 8
    rows = max(r for r in range(8, min(n_rows, 2048) + 1, 8) if n_rows % r == 0)
    clcs = _page_cumsum(clf.reshape(n_rows, PAGE_SIZE), rows=rows).reshape(depth, n_phys, 8, PAGE_SIZE)
    xp, xs = x_prompt, x_sample
    prompt_states, sample_states = [], []
    for l in range(depth):
        p = _layer_weights(l, *weights)
        xp, sp = _prompt_layer(xp, p, tm=tm, tq=tq, tt_a=tt_a, tt_b=tt_b)
        prompt_states.append(sp)
        xs, ss = _sample_layer(xs, p, l, state_conv_a[l], state_conv_b[l], state_delta[l], ckt, cvt, clcs,
                               page_table, group=group)
        sample_states.append(ss)
    ps = [jnp.stack(t) for t in zip(*prompt_states)]
    ss = [jnp.stack(t) for t in zip(*sample_states)]
    return (xp, xs, *ps, *ss)


def kernel(x_prompt, x_sample, cache_k, cache_v, cache_logf, page_table, state_conv_a, state_conv_b, state_delta,
           w_in, conv_a_w, conv_a_b, ln_a_g, ln_a_b, conv_b_w, a_log, dt_bias, norm_b_g, f_bias, w_out, g_pre_mix,
           g_post_mix, g_pre_ffn, g_post_ffn, w_gate, w_up, w_down):
    weights = (w_in, conv_a_w, conv_a_b, ln_a_g, ln_a_b, conv_b_w, a_log, dt_bias, norm_b_g, f_bias, w_out,
               g_pre_mix, g_post_mix, g_pre_ffn, g_post_ffn, w_gate, w_up, w_down)
    return _forward(x_prompt, x_sample, cache_k, cache_v, cache_logf, page_table, state_conv_a, state_conv_b,
                    state_delta, weights, tm=512, tq=512, tt_a=512, tt_b=256, group=16)
```

```python
import functools

import jax
import jax.numpy as jnp
from jax import lax
from jax.experimental import pallas as pl
from jax.experimental.pallas import tpu as pltpu

F32 = jnp.float32
BF16 = jnp.bfloat16

HEAD_DIM = 64
N_HEADS = 6
N_PAIRS = N_HEADS // 2
HW = N_HEADS * HEAD_DIM
CONV_A_WIDTH = 256
CONV_A_KERNEL = 31
SHORT_CONV = 4
DELTA_CHUNK = 64
PAGE_SIZE = 128
LANES = 128
SUBLANES = 8
RMS_EPS = 1e-6
LN_EPS = 1e-5
VMEM_LIMIT_BYTES = 56 * 1024 * 1024
NEG_INF = float("-inf")
LOG2E = 1.4426950408889634

_NT = (((1,), (1,)), ((), ()))


def _dot(a, b):
    return jnp.dot(a, b, preferred_element_type=F32)


def _dot_nt(a, b):
    return lax.dot_general(a, b, _NT, preferred_element_type=F32)


def _sigmoid(x):
    return 1.0 / (1.0 + jnp.exp(-x))


def _softplus(x):
    return jnp.maximum(x, 0.0) + jnp.log(1.0 + jnp.exp(-jnp.abs(x)))


def _silu(x):
    return x * _sigmoid(x)


def _rms(x, g):
    return x * lax.rsqrt(jnp.mean(x * x, axis=-1, keepdims=True) + RMS_EPS) * g


def _params(sem):
    return pltpu.CompilerParams(dimension_semantics=sem, vmem_limit_bytes=VMEM_LIMIT_BYTES)


def _iota(shape, dim):
    return lax.broadcasted_iota(jnp.int32, shape, dim)


def _split_hi_lo(x):
    hi = x.astype(BF16)
    lo = (x - hi.astype(F32)).astype(BF16)
    return hi, lo


def _split3(x):
    hi = x.astype(BF16)
    r = x - hi.astype(F32)
    mid = r.astype(BF16)
    lo = (r - mid.astype(F32)).astype(BF16)
    return hi, mid, lo


def _group_sum(x2, gmat):
    hi, lo = _split_hi_lo(x2)
    return _dot(hi, gmat) + _dot(lo, gmat)


_MAIN_SPLITS = (0, 256, 512, 1664, 2048, 2432)


def _gate_fns(raw, add, a_log):
    x = raw + add
    logf = -_softplus(-x)
    beta = _sigmoid(x)
    g = -jnp.exp(a_log) * _softplus(x)
    return logf, beta, g


_VX_ROWS = 80
_BIAS_ROWS = 3
_FOX_KEY_SPLIT = 2
_FOX_Q_BLOCK = 128


def _inproj_kernel(x_ref, g_ref, wmain_ref, wq_ref, wkv_ref, wsm_ref, addc_ref, alogc_ref, addr_ref, alogr_ref,
                   u_ref, qkvb_ref, zb_ref, *out_refs, decode):
    hb = _rms(x_ref[...], g_ref[...]).astype(BF16)
    tm = hb.shape[0]

    def proj(i):
        return _dot_nt(hb, wmain_ref[_MAIN_SPLITS[i]:_MAIN_SPLITS[i + 1], :])

    u_ref[...] = proj(0) * _sigmoid(proj(1))
    qkvb_ref[...] = proj(2)
    zb_ref[...] = proj(3)
    kv = _dot_nt(wkv_ref[...], hb)
    if decode:
        q_ref, kt_ref, vt_ref, smt_ref, smr_ref = out_refs
        q_ref[...] = _dot_nt(hb, wq_ref[...]) * (HEAD_DIM ** -0.5)
    else:
        kr_ref, qxt_ref, vx_ref, kt_ref, vt_ref, smt_ref, smr_ref = out_refs
        kr_ref[...] = proj(4).astype(BF16)
        qt = _dot_nt(wq_ref[...], hb) * (HEAD_DIM ** -0.5 * LOG2E)
        ones_rows = jnp.where(_iota((LANES - HEAD_DIM, tm), 0) < _BIAS_ROWS, 1.0, 0.0).astype(BF16)
        for h in range(N_HEADS):
            qxt_ref[0, h * LANES:h * LANES + HEAD_DIM, :] = qt[h * HEAD_DIM:(h + 1) * HEAD_DIM].astype(BF16)
            qxt_ref[0, h * LANES + HEAD_DIM:(h + 1) * LANES, :] = ones_rows
            vx_ref[0, h * _VX_ROWS:h * _VX_ROWS + HEAD_DIM, :] = kv[HW + h * HEAD_DIM:HW + (h + 1) * HEAD_DIM].astype(BF16)
            vx_ref[0, h * _VX_ROWS + HEAD_DIM:(h + 1) * _VX_ROWS, :] = jnp.ones((_VX_ROWS - HEAD_DIM, tm), BF16)
    kt_ref[0] = kv[:HW]
    vt_ref[0] = kv[HW:]

    rawt = _dot_nt(wsm_ref[0:24, :], hb)
    logf, beta, g = _gate_fns(rawt, addc_ref[...], alogc_ref[...])
    smt_ref[0, 0:8] = logf[0:8]
    smt_ref[0, 8:16] = beta[8:16]
    smt_ref[0, 16:24] = g[16:24]

    rawr = _dot_nt(hb, wsm_ref[...])
    logf, beta, g = _gate_fns(rawr, addr_ref[...], alogr_ref[...])
    col = _iota((1, LANES), 1)
    smr_ref[...] = jnp.where(col < 8, logf, jnp.where(col < 16, beta, g))


def _inproj(x, g, wmain, wq, wkv, wsm, addc, alogc, addr, alogr, *, nb, tm, decode):
    m, d = x.shape
    t = m // nb
    nt = t // tm
    row = lambda i: (i, 0)
    const = lambda i: (0, 0)
    tr = lambda i: (i // nt, 0, i % nt)
    outs = [jax.ShapeDtypeStruct((m, CONV_A_WIDTH), F32), jax.ShapeDtypeStruct((m, 3 * HW), F32),
            jax.ShapeDtypeStruct((m, HW), F32)]
    out_specs = [pl.BlockSpec((tm, CONV_A_WIDTH), row), pl.BlockSpec((tm, 3 * HW), row), pl.BlockSpec((tm, HW), row)]
    if decode:
        outs += [jax.ShapeDtypeStruct((m, HW), F32)]
        out_specs += [pl.BlockSpec((tm, HW), row)]
    else:
        outs += [jax.ShapeDtypeStruct((m, HW), BF16), jax.ShapeDtypeStruct((nb, N_HEADS * LANES, t), BF16),
                 jax.ShapeDtypeStruct((nb, N_HEADS * _VX_ROWS, t), BF16)]
        out_specs += [pl.BlockSpec((tm, HW), row), pl.BlockSpec((1, N_HEADS * LANES, tm), tr),
                      pl.BlockSpec((1, N_HEADS * _VX_ROWS, tm), tr)]
    outs += [jax.ShapeDtypeStruct((nb, HW, t), F32), jax.ShapeDtypeStruct((nb, HW, t), F32),
             jax.ShapeDtypeStruct((nb, 24, t), F32), jax.ShapeDtypeStruct((m, LANES), F32)]
    out_specs += [pl.BlockSpec((1, HW, tm), tr), pl.BlockSpec((1, HW, tm), tr),
                  pl.BlockSpec((1, 24, tm), tr), pl.BlockSpec((tm, LANES), row)]
    in_specs = [
        pl.BlockSpec((tm, d), row), pl.BlockSpec((1, d), const),
        pl.BlockSpec(wmain.shape, const), pl.BlockSpec(wq.shape, const), pl.BlockSpec(wkv.shape, const),
        pl.BlockSpec(wsm.shape, const),
        pl.BlockSpec((24, 1), const), pl.BlockSpec((24, 1), const),
        pl.BlockSpec((1, LANES), const), pl.BlockSpec((1, LANES), const),
    ]
    return pl.pallas_call(
        functools.partial(_inproj_kernel, decode=decode), grid=(m // tm,), in_specs=in_specs, out_specs=out_specs,
        out_shape=outs, compiler_params=_params(("arbitrary",)), name="inproj",
    )(x, g, wmain, wq, wkv, wsm, addc, alogc, addr, alogr)


_CA_PAD = 32
_CA_ROWS = 64


def _conv_a_kernel(u_ref, buf_ref, w_ref, b_ref, lg_ref, lb_ref, y_ref, nbuf_ref, ext_ref, *scratch, tt):
    ti = pl.program_id(1)

    @pl.when(ti == 0)
    def _():
        ext_ref[0:_CA_PAD] = buf_ref[0]

    ext_ref[_CA_PAD:_CA_PAD + tt] = u_ref[0]
    off = _CA_PAD - (CONV_A_KERNEL - 1)
    rows = min(_CA_ROWS, tt)
    aligned = tt % SUBLANES == 0
    if aligned:
        (sh_ref,) = scratch
        span = tt + _CA_PAD - SUBLANES
        for r in range(1, SUBLANES):
            sh_ref[r - 1] = ext_ref[r:r + span]
    for c in range(tt // rows):
        acc = jnp.zeros((rows, CONV_A_WIDTH), F32)
        for j in range(CONV_A_KERNEL):
            o = off + j + c * rows
            if aligned and o % SUBLANES:
                tap = sh_ref[o % SUBLANES - 1, o - o % SUBLANES:o - o % SUBLANES + rows, :]
            else:
                tap = ext_ref[o:o + rows, :]
            acc = acc + w_ref[j:j + 1, :] * tap
        ca = acc + b_ref[...]
        mu = jnp.mean(ca, axis=-1, keepdims=True)
        xc = ca - mu
        var = jnp.mean(xc * xc, axis=-1, keepdims=True)
        y_ref[0, c * rows:(c + 1) * rows, :] = _silu(xc * lax.rsqrt(var + LN_EPS) * lg_ref[...] + lb_ref[...])
    tail = ext_ref[tt:tt + _CA_PAD]
    nbuf_ref[0] = tail
    ext_ref[0:_CA_PAD] = tail


def _conv_a(u, buf, w, b, lg, lb, *, tt):
    nb, t, c = u.shape
    const = lambda bi, ti: (0, 0)
    return pl.pallas_call(
        functools.partial(_conv_a_kernel, tt=tt),
        grid=(nb, t // tt),
        in_specs=[pl.BlockSpec((1, tt, c), lambda bi, ti: (bi, ti, 0)),
                  pl.BlockSpec((1, _CA_PAD, c), lambda bi, ti: (bi, 0, 0)),
                  pl.BlockSpec((_CA_PAD, c), const), pl.BlockSpec((1, c), const),
                  pl.BlockSpec((1, c), const), pl.BlockSpec((1, c), const)],
        out_specs=[pl.BlockSpec((1, tt, c), lambda bi, ti: (bi, ti, 0)),
                   pl.BlockSpec((1, _CA_PAD, c), lambda bi, ti: (bi, 0, 0))],
        out_shape=[jax.ShapeDtypeStruct((nb, t, c), F32), jax.ShapeDtypeStruct((nb, _CA_PAD, c), F32)],
        scratch_shapes=[pltpu.VMEM((_CA_PAD + tt, c), F32)]
        + ([pltpu.VMEM((SUBLANES - 1, tt + _CA_PAD - SUBLANES, c), F32)] if tt % SUBLANES == 0 else []),
        compiler_params=_params(("arbitrary", "arbitrary")), name="conv_a",
    )(u, buf, w, b, lg, lb)


_CB_PAD = 8


def _pair_cols(x, p):
    return x[:, p * LANES:(p + 1) * LANES]


def _head_cols(sm, base, p, lane_lo):
    c0 = sm[:, base + 2 * p:base + 2 * p + 1]
    c1 = sm[:, base + 2 * p + 1:base + 2 * p + 2]
    return jnp.where(lane_lo, c0, c1)


def _block_diag(x, lane_lo):
    return jnp.concatenate([jnp.where(lane_lo, x, 0.0), jnp.where(lane_lo, 0.0, x)], axis=0)


def _delta_kernel(x_ref, buf_ref, w_ref, smr_ref, smt_ref, z_ref, gb_ref, s0_ref, gmat_ref, tril_ref, triu_ref,
                  y_ref, nbuf_ref, sout_ref, ext_ref, s_ref, *, tt):
    ti = pl.program_id(1)
    nchunk = tt // DELTA_CHUNK
    c = DELTA_CHUNK

    @pl.when(ti == 0)
    def _():
        ext_ref[0:_CB_PAD] = buf_ref[0]
        s_ref[...] = s0_ref[0]

    ext_ref[_CB_PAD:_CB_PAD + tt] = x_ref[0]
    off = _CB_PAD - (SHORT_CONV - 1)
    acc = w_ref[0:1, :] * ext_ref[off:off + tt, :]
    for j in range(1, SHORT_CONV):
        acc = acc + w_ref[j:j + 1, :] * ext_ref[off + j:off + j + tt, :]
    tail = ext_ref[tt:tt + _CB_PAD]
    nbuf_ref[0] = tail
    ext_ref[0:_CB_PAD] = tail
    cb = _silu(acc)

    gmat = gmat_ref[...]
    q = cb[:, 0:HW]
    k = cb[:, HW:2 * HW]
    v = cb[:, 2 * HW:3 * HW]
    qs = q * lax.rsqrt(_group_sum(q * q, gmat) + 1e-6) * (HEAD_DIM ** -0.5)
    kn = k * lax.rsqrt(_group_sum(k * k, gmat) + 1e-6)

    smr = smr_ref[0]
    smt = smt_ref[0]
    s_hi, s_mid, s_lo = _split3(smr)
    gc_col = _dot(tril_ref[...], s_hi) + _dot(tril_ref[...], s_mid) + _dot(tril_ref[...], s_lo)
    t_hi, t_mid, t_lo = _split3(smt[16:24])
    gc_row = _dot(t_hi, triu_ref[...]) + _dot(t_mid, triu_ref[...]) + _dot(t_lo, triu_ref[...])
    if tt < LANES:
        gc_row = jnp.concatenate([gc_row, jnp.zeros((8, LANES - tt), F32)], axis=1)

    lane = _iota((1, LANES), 1)
    lane_lo = lane < HEAD_DIM
    ri = _iota((c, LANES), 0)
    ci_ = _iota((c, LANES), 1)
    cj = jnp.where(ci_ < HEAD_DIM, ci_, ci_ - HEAD_DIM)
    causal = ri >= cj
    strict = ri > cj
    eye2 = jnp.where(ri == cj, 1.0, 0.0)
    r128 = _iota((LANES, LANES), 0)
    c128 = _iota((LANES, LANES), 1)
    bdmask = (r128 < HEAD_DIM) == (c128 < HEAD_DIM)

    combos = [(ci, p) for ci in range(nchunk) for p in range(N_PAIRS)]
    pre = []
    for ci, p in combos:
        r0, r1 = ci * c, (ci + 1) * c
        qs_p = _pair_cols(qs, p)[r0:r1]
        kn_p = _pair_cols(kn, p)[r0:r1]
        v_p = _pair_cols(v, p)[r0:r1]
        beta = _head_cols(smr[r0:r1], 8, p, lane_lo)
        gi = _head_cols(gc_col[r0:r1], 16, p, lane_lo)
        blk = gc_row[:, (ci // 2) * LANES:(ci // 2 + 1) * LANES]
        rot = pltpu.roll(blk, HEAD_DIM, 1)
        if ci % 2 == 0:
            gj = jnp.where(lane_lo, blk[2 * p:2 * p + 1], rot[2 * p + 1:2 * p + 2])
        else:
            gj = jnp.where(lane_lo, rot[2 * p:2 * p + 1], blk[2 * p + 1:2 * p + 2])
        glast = gi[c - 1:c, :]
        eg = jnp.exp(gi)
        decay = jnp.exp(jnp.where(causal, gi - gj, NEG_INF))
        kb = kn_p * beta
        pre.append(dict(qs=qs_p, kn=kn_p, vb=v_p * beta, kb=kb, kbe=kb * eg, qe=qs_p * eg, eg_last=jnp.exp(glast),
                        kdec=kn_p * jnp.exp(glast - gi), decay=decay, kd=_block_diag(kn_p, lane_lo)))

    grams = [_dot_nt(jnp.concatenate([d["kb"], d["qs"]], axis=0), d["kd"]) for d in pre]
    lmats = [jnp.where(strict, g[0:c] * d["decay"], 0.0) for g, d in zip(grams, pre)]
    qks = [jnp.where(causal, g[c:2 * c] * d["decay"], 0.0) for g, d in zip(grams, pre)]

    def split_bd(m):
        hi, lo = _split_hi_lo(m)
        return hi, lo, _block_diag(hi, lane_lo), _block_diag(lo, lane_lo)

    def dot3(a_hi, a_lo, b_hi, b_lo):
        return _dot(a_hi, b_hi) + _dot(a_lo, b_hi) + _dot(a_hi, b_lo)

    xs = [eye2 - l for l in lmats]
    pws = [dot3(*split_bd(l)) for l in lmats]
    for level in range(4):
        psp = [split_bd(pw) for pw in pws]
        xsp = [_split_hi_lo(x) for x in xs]
        prods = [dot3(jnp.concatenate([ph, xh], axis=0), jnp.concatenate([plo, xl], axis=0), bh, bl)
                 for (ph, plo, bh, bl), (xh, xl) in zip(psp, xsp)]
        xs = [x + pr[c:2 * c] for x, pr in zip(xs, prods)]
        pws = [pr[0:c] for pr in prods]
    xs = [x + dot3(*_split_hi_lo(x), *split_bd(pw)[2:]) for pw, x in zip(pws, xs)]

    rhs_uw = [jnp.concatenate([_block_diag(d["vb"], lane_lo), _block_diag(d["kbe"], lane_lo)], axis=1) for d in pre]
    uws = [_dot(x, r) for x, r in zip(xs, rhs_uw)]
    kts = [d["kdec"].T for d in pre]
    kuws = [_dot(kt, uw) for kt, uw in zip(kts, uws)]
    kus = [jnp.where(bdmask, m[:, 0:LANES], 0.0) for m in kuws]
    kws = [jnp.where(bdmask, m[:, LANES:2 * LANES], 0.0) for m in kuws]
    rhs_q = [jnp.concatenate([_block_diag(uw[:, 0:LANES], lane_lo), _block_diag(uw[:, LANES:2 * LANES], lane_lo)],
                             axis=1) for uw in uws]
    qkuw = [_dot(qk, r) for qk, r in zip(qks, rhs_q)]
    o2s = [m[:, 0:LANES] for m in qkuw]
    q2s = [d["qe"] - m[:, LANES:2 * LANES] for d, m in zip(pre, qkuw)]

    states = [s_ref[p] for p in range(N_PAIRS)]
    o_chunks = []
    for ci in range(nchunk):
        o_pairs = []
        for p in range(N_PAIRS):
            i = ci * N_PAIRS + p
            s_bd = states[p]
            prod = _dot(jnp.concatenate([q2s[i], kws[i]], axis=0), s_bd)
            o_pairs.append(prod[0:c] + o2s[i])
            states[p] = s_bd * pre[i]["eg_last"] - prod[c:c + LANES] + kus[i]
        o_chunks.append(jnp.concatenate(o_pairs, axis=1))
    for p in range(N_PAIRS):
        s_ref[p] = states[p]
    o = jnp.concatenate(o_chunks, axis=0) if nchunk > 1 else o_chunks[0]
    ms = _group_sum(o * o, gmat) * (1.0 / HEAD_DIM)
    y_ref[0] = o * lax.rsqrt(ms + RMS_EPS) * gb_ref[...] * _silu(z_ref[0])
    sout_ref[0] = s_ref[...]


def _delta(x, buf, w, smr, smt, z, gb, s0, *, tt):
    nb, t, cw = x.shape
    const = lambda bi, ti: (0, 0)
    head = jnp.arange(HW) // HEAD_DIM
    gmat = (head[:, None] == head[None, :]).astype(BF16)
    chunk = jnp.arange(tt) // DELTA_CHUNK
    same = chunk[:, None] == chunk[None, :]
    pos = jnp.arange(tt)
    tril = (same & (pos[:, None] >= pos[None, :])).astype(BF16)
    triu = tril.T
    return pl.pallas_call(
        functools.partial(_delta_kernel, tt=tt),
        grid=(nb, t // tt),
        in_specs=[pl.BlockSpec((1, tt, cw), lambda bi, ti: (bi, ti, 0)),
                  pl.BlockSpec((1, _CB_PAD, cw), lambda bi, ti: (bi, 0, 0)),
                  pl.BlockSpec((SHORT_CONV, cw), const),
                  pl.BlockSpec((1, tt, LANES), lambda bi, ti: (bi, ti, 0)),
                  pl.BlockSpec((1, 24, tt), lambda bi, ti: (bi, 0, ti)),
                  pl.BlockSpec((1, tt, HW), lambda bi, ti: (bi, ti, 0)),
                  pl.BlockSpec((1, HW), const),
                  pl.BlockSpec((1, N_PAIRS, LANES, LANES), lambda bi, ti: (bi, 0, 0, 0)),
                  pl.BlockSpec((HW, HW), const), pl.BlockSpec((tt, tt), const), pl.BlockSpec((tt, tt), const)],
        out_specs=[pl.BlockSpec((1, tt, HW), lambda bi, ti: (bi, ti, 0)),
                   pl.BlockSpec((1, _CB_PAD, cw), lambda bi, ti: (bi, 0, 0)),
                   pl.BlockSpec((1, N_PAIRS, LANES, LANES), lambda bi, ti: (bi, 0, 0, 0))],
        out_shape=[jax.ShapeDtypeStruct((nb, t, HW), F32), jax.ShapeDtypeStruct((nb, _CB_PAD, cw), F32),
                   jax.ShapeDtypeStruct((nb, N_PAIRS, LANES, LANES), F32)],
        scratch_shapes=[pltpu.VMEM((_CB_PAD + tt, cw), F32), pltpu.VMEM((N_PAIRS, LANES, LANES), F32)],
        compiler_params=_params(("arbitrary", "arbitrary")), name="delta",
    )(x, buf, w, smr, smt, z, gb, s0, gmat, tril, triu)


_DEC_ROWS = 8


def _delta_decode_kernel(x_ref, buf_ref, w_ref, smr_ref, z_ref, gb_ref, s0_ref, gmat_ref, y_ref, sout_ref):
    xc = w_ref[SHORT_CONV - 1:SHORT_CONV, :] * x_ref[...]
    for j in range(SHORT_CONV - 1):
        xc = xc + w_ref[j:j + 1, :] * buf_ref[j]
    cb = _silu(xc)
    gmat = gmat_ref[...]
    q = cb[:, 0:HW]
    k = cb[:, HW:2 * HW]
    v = cb[:, 2 * HW:3 * HW]
    qs = q * lax.rsqrt(_group_sum(q * q, gmat) + 1e-6) * (HEAD_DIM ** -0.5)
    kn = k * lax.rsqrt(_group_sum(k * k, gmat) + 1e-6)
    qk = _group_sum(qs * kn, gmat)
    smr = smr_ref[...]
    lane_lo = _iota((1, LANES), 1) < HEAD_DIM
    row = _iota((_DEC_ROWS, LANES), 0)
    r128 = _iota((LANES, LANES), 0)
    c128 = _iota((LANES, LANES), 1)
    bdmask = (r128 < HEAD_DIM) == (c128 < HEAD_DIM)
    o_pairs = []
    for p in range(N_PAIRS):
        qs_p, kn_p, v_p, qk_p = (_pair_cols(t, p) for t in (qs, kn, v, qk))
        beta = _head_cols(smr, 8, p, lane_lo)
        a = jnp.exp(_head_cols(smr, 16, p, lane_lo))
        kn_t = kn_p.T
        o_p = jnp.zeros((_DEC_ROWS, LANES), F32)
        for r in range(_DEC_ROWS):
            s_bd = s0_ref[r, p]
            kq = jnp.where(row == 0, kn_p[r:r + 1], jnp.where(row == 1, qs_p[r:r + 1], 0.0))
            prod = _dot(kq, s_bd)
            a_r = a[r:r + 1]
            v_new = beta[r:r + 1] * (v_p[r:r + 1] - a_r * prod[0:1])
            o_p = jnp.where(row == r, a_r * prod[1:2] + qk_p[r:r + 1] * v_new, o_p)
            sout_ref[r, p] = jnp.where(bdmask, s_bd * a_r + kn_t[:, r:r + 1] * v_new, 0.0)
        o_pairs.append(o_p)
    o = jnp.concatenate(o_pairs, axis=1)
    ms = _group_sum(o * o, gmat) * (1.0 / HEAD_DIM)
    y_ref[...] = o * lax.rsqrt(ms + RMS_EPS) * gb_ref[...] * _silu(z_ref[...])


def _delta_decode(x, buf, w, smr, z, gb, s0):
    bd, cw = x.shape
    head = jnp.arange(HW) // HEAD_DIM
    gmat = (head[:, None] == head[None, :]).astype(BF16)
    rows = lambda i: (i, 0)
    const = lambda i: (0, 0)
    return pl.pallas_call(
        _delta_decode_kernel, grid=(bd // _DEC_ROWS,),
        in_specs=[pl.BlockSpec((_DEC_ROWS, cw), rows),
                  pl.BlockSpec((SHORT_CONV - 1, _DEC_ROWS, cw), lambda i: (0, i, 0)),
                  pl.BlockSpec((SHORT_CONV, cw), const), pl.BlockSpec((_DEC_ROWS, LANES), rows),
                  pl.BlockSpec((_DEC_ROWS, HW), rows), pl.BlockSpec((1, HW), const),
                  pl.BlockSpec((_DEC_ROWS, N_PAIRS, LANES, LANES), lambda i: (i, 0, 0, 0)),
                  pl.BlockSpec((HW, HW), const)],
        out_specs=[pl.BlockSpec((_DEC_ROWS, HW), rows),
                   pl.BlockSpec((_DEC_ROWS, N_PAIRS, LANES, LANES), lambda i: (i, 0, 0, 0))],
        out_shape=[jax.ShapeDtypeStruct((bd, HW), F32), jax.ShapeDtypeStruct((bd, N_PAIRS, LANES, LANES), F32)],
        compiler_params=_params(("arbitrary",)), name="delta_decode",
    )(x, buf, w, smr, z, gb, s0, gmat)


def _conv_a_decode_kernel(u_ref, st_ref, w_ref, b_ref, lg_ref, lb_ref, y_ref):
    acc = w_ref[CONV_A_KERNEL - 1:CONV_A_KERNEL, :] * u_ref[...]
    for j in range(CONV_A_KERNEL - 1):
        acc = acc + w_ref[j:j + 1, :] * st_ref[j]
    ca = acc + b_ref[...]
    mu = jnp.mean(ca, axis=-1, keepdims=True)
    xc = ca - mu
    var = jnp.mean(xc * xc, axis=-1, keepdims=True)
    y_ref[...] = _silu(xc * lax.rsqrt(var + LN_EPS) * lg_ref[...] + lb_ref[...])


def _conv_a_decode(u, st, w, b, lg, lb):
    bd, c = u.shape
    whole = lambda a: pl.BlockSpec(a.shape, lambda i: (0,) * a.ndim)
    return pl.pallas_call(
        _conv_a_decode_kernel, grid=(1,),
        in_specs=[whole(u), whole(st), whole(w), whole(b), whole(lg), whole(lb)],
        out_specs=whole(u), out_shape=jax.ShapeDtypeStruct((bd, c), F32),
        compiler_params=_params(("arbitrary",)), name="conv_a_decode",
    )(u, st, w, b, lg, lb)


def _fox_keys_kernel(smr_ref, kr_ref, tril_ref, pk_ref, pb_ref, kx_ref, carry_ref):
    ti = pl.program_id(1)

    @pl.when(ti == 0)
    def _():
        carry_ref[...] = jnp.zeros_like(carry_ref)

    l_hi, l_mid, l_lo = _split3(smr_ref[0])
    tril = tril_ref[...]
    cs = _dot(tril, l_hi) + _dot(tril, l_mid) + _dot(tril, l_lo) + carry_ref[0:1, :]
    tl = cs.shape[0]
    carry_ref[...] = jnp.broadcast_to(cs[tl - 1:tl, :], carry_ref.shape)
    hi, mid, lo = _split3(jnp.where(_iota((1, LANES), 1) < N_HEADS, cs * (-LOG2E), 0.0))
    kx = (_dot(kr_ref[0], pk_ref[...]) + _dot(hi, pb_ref[0]) + _dot(mid, pb_ref[1]) + _dot(lo, pb_ref[2]))
    kx_ref[0] = kx.astype(BF16)


def _fox_keys(smr, kr, *, tl):
    nb, t, _ = smr.shape
    pos = jnp.arange(tl)
    tril = (pos[:, None] >= pos[None, :]).astype(BF16)
    src = jnp.arange(HW)
    dst = (src // HEAD_DIM) * LANES + src % HEAD_DIM
    pk = (dst[:, None] == jnp.arange(N_HEADS * LANES)[None, :]).astype(BF16)
    col = jnp.arange(LANES)
    pb = jnp.stack([((col[:, None] < N_HEADS)
                     & (col[:, None] * LANES + HEAD_DIM + piece == jnp.arange(N_HEADS * LANES)[None, :])).astype(BF16)
                    for piece in range(_BIAS_ROWS)])
    const2 = lambda bi, ti: (0, 0)
    return pl.pallas_call(
        _fox_keys_kernel, grid=(nb, t // tl),
        in_specs=[pl.BlockSpec((1, tl, LANES), lambda bi, ti: (bi, ti, 0)),
                  pl.BlockSpec((1, tl, HW), lambda bi, ti: (bi, ti, 0)),
                  pl.BlockSpec((tl, tl), const2), pl.BlockSpec(pk.shape, const2),
                  pl.BlockSpec(pb.shape, lambda bi, ti: (0, 0, 0))],
        out_specs=pl.BlockSpec((1, tl, N_HEADS * LANES), lambda bi, ti: (bi, ti, 0)),
        out_shape=jax.ShapeDtypeStruct((nb, t, N_HEADS * LANES), BF16),
        scratch_shapes=[pltpu.VMEM((8, LANES), F32)],
        compiler_params=_params(("arbitrary", "arbitrary")), name="fox_keys",
    )(smr, kr, tril, pk, pb)


def _fox_kernel(qi_ref, ki_ref, kx_ref, qxt_ref, vx_ref, o_ref, m_ref, acc_ref, *, tq, tk):
    step = pl.program_id(1)
    qi = qi_ref[step]
    ki = ki_ref[step]

    @pl.when(ki == 0)
    def _():
        m_ref[...] = jnp.full_like(m_ref, NEG_INF)
        acc_ref[...] = jnp.zeros_like(acc_ref)

    def update(masked):
        if masked:
            keep = _iota((tk, tq), 0) <= _iota((tk, tq), 1)
        combos = [(h, qc) for h in range(N_HEADS) for qc in range(tq // _FOX_Q_BLOCK)]
        half = tk // _FOX_KEY_SPLIT
        ms = [m_ref[h:h + 1, qc * _FOX_Q_BLOCK:(qc + 1) * _FOX_Q_BLOCK] for h, qc in combos]
        accs = [acc_ref[h, :, qc * _FOX_Q_BLOCK:(qc + 1) * _FOX_Q_BLOCK] for h, qc in combos]
        for kh in range(_FOX_KEY_SPLIT):
            k0 = kh * half
            live = [i for i, (h, qc) in enumerate(combos) if not masked or (qc + 1) * _FOX_Q_BLOCK > k0]
            sts = [_dot(kx_ref[0, k0:k0 + half, combos[i][0] * LANES:(combos[i][0] + 1) * LANES],
                        qxt_ref[0, combos[i][0] * LANES:(combos[i][0] + 1) * LANES,
                                combos[i][1] * _FOX_Q_BLOCK:(combos[i][1] + 1) * _FOX_Q_BLOCK]) for i in live]
            if masked:
                sts = [jnp.where(keep[k0:k0 + half, combos[i][1] * _FOX_Q_BLOCK:(combos[i][1] + 1) * _FOX_Q_BLOCK],
                                 st, NEG_INF) for st, i in zip(sts, live)]
            m_news = [jnp.maximum(ms[i], jnp.max(st, axis=0, keepdims=True)) for i, st in zip(live, sts)]
            pts = [jnp.exp2(st - mn).astype(BF16) for st, mn in zip(sts, m_news)]
            for i, mn, pt in zip(live, m_news, pts):
                h = combos[i][0]
                accs[i] = (jnp.exp2(ms[i] - mn) * accs[i]
                           + _dot(vx_ref[0, h * _VX_ROWS:(h + 1) * _VX_ROWS, k0:k0 + half], pt))
                ms[i] = mn
        for (h, qc), mn, acc in zip(combos, ms, accs):
            acc_ref[h, :, qc * _FOX_Q_BLOCK:(qc + 1) * _FOX_Q_BLOCK] = acc
            m_ref[h:h + 1, qc * _FOX_Q_BLOCK:(qc + 1) * _FOX_Q_BLOCK] = mn

    @pl.when(ki < qi)
    def _():
        update(False)

    @pl.when(ki == qi)
    def _():
        update(True)
        for h in range(N_HEADS):
            acc = acc_ref[h]
            o_ref[0, h * HEAD_DIM:(h + 1) * HEAD_DIM, :] = acc[0:HEAD_DIM] / acc[HEAD_DIM:HEAD_DIM + 1]


def _fox_prompt(kx, qxt, vx, *, tq):
    nb, t, _ = kx.shape
    nq = t // tq
    qi_tab = jnp.asarray([qi for qi in range(nq) for _ in range(qi + 1)], jnp.int32)
    ki_tab = jnp.asarray([ki for qi in range(nq) for ki in range(qi + 1)], jnp.int32)
    grid_spec = pltpu.PrefetchScalarGridSpec(
        num_scalar_prefetch=2, grid=(nb, int(qi_tab.shape[0])),
        in_specs=[pl.BlockSpec((1, tq, N_HEADS * LANES), lambda b, s, qt, kt: (b, kt[s], 0)),
                  pl.BlockSpec((1, N_HEADS * LANES, tq), lambda b, s, qt, kt: (b, 0, qt[s])),
                  pl.BlockSpec((1, N_HEADS * _VX_ROWS, tq), lambda b, s, qt, kt: (b, 0, kt[s]))],
        out_specs=pl.BlockSpec((1, HW, tq), lambda b, s, qt, kt: (b, 0, qt[s])),
        scratch_shapes=[pltpu.VMEM((8, tq), F32), pltpu.VMEM((N_HEADS, _VX_ROWS, tq), F32)],
    )
    return pl.pallas_call(
        functools.partial(_fox_kernel, tq=tq, tk=tq), grid_spec=grid_spec,
        out_shape=jax.ShapeDtypeStruct((nb, HW, t), F32),
        compiler_params=_params(("arbitrary", "arbitrary")), name="fox_prompt",
    )(qi_tab, ki_tab, kx, qxt, vx)


def _page_cumsum_kernel(x_ref, triu_ref, o_ref):
    hi, mid, lo = _split3(x_ref[...])
    triu = triu_ref[...]
    o_ref[...] = _dot(hi, triu) + _dot(mid, triu) + _dot(lo, triu)


def _page_cumsum(lf, *, rows):
    r = lf.shape[0]
    pos = jnp.arange(PAGE_SIZE)
    triu = (pos[:, None] <= pos[None, :]).astype(BF16)
    return pl.pallas_call(
        _page_cumsum_kernel, grid=(r // rows,),
        in_specs=[pl.BlockSpec((rows, PAGE_SIZE), lambda i: (i, 0)),
                  pl.BlockSpec((PAGE_SIZE, PAGE_SIZE), lambda i: (0, 0))],
        out_specs=pl.BlockSpec((rows, PAGE_SIZE), lambda i: (i, 0)),
        out_shape=jax.ShapeDtypeStruct((r, PAGE_SIZE), F32),
        compiler_params=_params(("arbitrary",)), name="page_cumsum",
    )(lf, triu)


def _fox_decode_kernel(pt_ref, qx_ref, kn_ref, vn_ref, lfn_ref, kt_hbm, vt_hbm, lcs_hbm, o_ref,
                       kbuf, vbuf, lbuf, sem, qbd_ref, m_ref, l_ref, acc_ref, carry_ref, *, layer, n_steps, group):
    b = pl.program_id(0)
    j = pl.program_id(1)
    step = b * n_steps + j
    slot = lax.rem(step, 2)
    total = pl.num_programs(0) * n_steps

    def page_copies(bb, jj, to_slot, lookup):
        out = []
        for g in range(group):
            page = pt_ref[bb, jj * group + g] if lookup else 0
            out.append(pltpu.make_async_copy(kt_hbm.at[layer, page], kbuf.at[to_slot, g], sem.at[0, to_slot]))
            out.append(pltpu.make_async_copy(vt_hbm.at[layer, page], vbuf.at[to_slot, g], sem.at[1, to_slot]))
            out.append(pltpu.make_async_copy(lcs_hbm.at[layer, page], lbuf.at[to_slot, g], sem.at[2, to_slot]))
        return out

    @pl.when(step == 0)
    def _():
        for cp in page_copies(0, 0, 0, True):
            cp.start()

    @pl.when(step + 1 < total)
    def _():
        last = j == n_steps - 1
        for cp in page_copies(jnp.where(last, b + 1, b), jnp.where(last, 0, j + 1), 1 - slot, True):
            cp.start()

    for cp in page_copies(0, 0, slot, False):
        cp.wait()
    kt_refs = [kbuf.at[slot, g] for g in range(group)]
    vt_refs = [vbuf.at[slot, g] for g in range(group)]
    lcs_refs = [lbuf.at[slot, g] for g in range(group)]

    @pl.when(j == 0)
    def _():
        row = _iota((8, HW), 0)
        col = _iota((8, HW), 1)
        own = (col >= row * HEAD_DIM) & (col < (row + 1) * HEAD_DIM)
        qbd_ref[...] = jnp.where(own, qx_ref[0], 0.0)
        m_ref[...] = jnp.full_like(m_ref, NEG_INF)
        l_ref[...] = jnp.zeros_like(l_ref)
        acc_ref[...] = jnp.zeros_like(acc_ref)
        carry_ref[...] = jnp.zeros_like(carry_ref)

    qbd = qbd_ref[...]
    lcs = [lcs_refs[g][...] for g in range(group)]
    totals = [jnp.broadcast_to(x[:, PAGE_SIZE - 1:PAGE_SIZE], x.shape) for x in lcs]
    qk = [_dot(qbd, kt_refs[g][...]) for g in range(group)]
    c_run = carry_ref[...]
    scores = []
    for g in range(group):
        scores.append(qk[g] - (lcs[g] + c_run))
        c_run = c_run + totals[g]
    c_end = c_run[:, 0:1]
    s = jnp.concatenate(scores, axis=1)
    m_prev = m_ref[:, 0:1]
    m_new = jnp.maximum(m_prev, jnp.max(s, axis=-1, keepdims=True))
    alpha = jnp.exp(m_prev - m_new)
    pr = jnp.exp(s - m_new)
    l_new = alpha * l_ref[:, 0:1] + jnp.sum(pr, axis=-1, keepdims=True)
    pv = _dot_nt(pr[:, 0:PAGE_SIZE], vt_refs[0][...])
    for g in range(1, group):
        pv = pv + _dot_nt(pr[:, g * PAGE_SIZE:(g + 1) * PAGE_SIZE], vt_refs[g][...])
    acc_new = alpha * acc_ref[...] + pv
    m_ref[...] = jnp.broadcast_to(m_new, m_ref.shape)
    l_ref[...] = jnp.broadcast_to(l_new, l_ref.shape)
    acc_ref[...] = acc_new
    carry_ref[...] = c_run

    @pl.when(j == n_steps - 1)
    def _():
        s_n = jnp.sum(qbd_ref[...] * kn_ref[0], axis=-1, keepdims=True) - (c_end + lfn_ref[0][:, 0:1])
        m_f = jnp.maximum(m_new, s_n)
        a_f = jnp.exp(m_new - m_f)
        p_n = jnp.exp(s_n - m_f)
        l_f = a_f * l_new + p_n
        o8 = (a_f * acc_new + p_n * vn_ref[0]) / l_f
        row = _iota((8, HW), 0)
        col = _iota((8, HW), 1)
        own = (col >= row * HEAD_DIM) & (col < (row + 1) * HEAD_DIM)
        o_ref[0] = jnp.sum(jnp.where(own, o8, 0.0), axis=0, keepdims=True)


def _fox_decode(page_table, layer, qx, cache_kt, cache_vt, cache_lcs, k_new, v_new, lf_new, *, group):
    bd = qx.shape[0]
    n_pages = page_table.shape[1]
    n_steps = n_pages // group
    per_b = lambda b, j, pt: (b, 0, 0)
    hbm = pl.BlockSpec(memory_space=pl.ANY)
    grid_spec = pltpu.PrefetchScalarGridSpec(
        num_scalar_prefetch=1, grid=(bd, n_steps),
        in_specs=[pl.BlockSpec((1, 1, HW), per_b), pl.BlockSpec((1, 1, HW), per_b), pl.BlockSpec((1, 1, HW), per_b),
                  pl.BlockSpec((1, 8, LANES), per_b), hbm, hbm, hbm],
        out_specs=pl.BlockSpec((1, 1, HW), per_b),
        scratch_shapes=[pltpu.VMEM((2, group, HW, PAGE_SIZE), F32), pltpu.VMEM((2, group, HW, PAGE_SIZE), F32),
                        pltpu.VMEM((2, group, 8, PAGE_SIZE), F32), pltpu.SemaphoreType.DMA((3, 2)),
                        pltpu.VMEM((8, HW), F32), pltpu.VMEM((8, LANES), F32), pltpu.VMEM((8, LANES), F32),
                        pltpu.VMEM((8, HW), F32), pltpu.VMEM((8, LANES), F32)],
    )
    return pl.pallas_call(
        functools.partial(_fox_decode_kernel, layer=layer, n_steps=n_steps, group=group), grid_spec=grid_spec,
        out_shape=jax.ShapeDtypeStruct((bd, 1, HW), F32),
        compiler_params=_params(("arbitrary", "arbitrary")), name="fox_decode",
    )(page_table, qx, k_new, v_new, lf_new, cache_kt, cache_vt, cache_lcs)


_FF_CHUNK = 704


def _outffn_kernel(x_ref, ya_ref, yb_ref, yc_ref, wo_ref, wg_ref, wu_ref, wd_ref, gpm_ref, gpf_ref, gqf_ref, o_ref,
                   *, yc_transposed):
    a0 = CONV_A_WIDTH
    a1 = a0 + HW
    yc = yc_ref[0].T if yc_transposed else yc_ref[...]
    mix = (_dot(ya_ref[...].astype(BF16), wo_ref[0:a0, :]) + _dot(yb_ref[...].astype(BF16), wo_ref[a0:a1, :])
           + _dot(yc.astype(BF16), wo_ref[a1:a1 + HW, :]))
    x1 = x_ref[...] + _rms(mix, gpm_ref[...])
    hb = _rms(x1, gpf_ref[...]).astype(BF16)
    d_ff = wg_ref.shape[1]
    f = jnp.zeros_like(x1)
    for c0 in range(0, d_ff, _FF_CHUNK):
        gate = _dot(hb, wg_ref[:, c0:c0 + _FF_CHUNK])
        up = _dot(hb, wu_ref[:, c0:c0 + _FF_CHUNK])
        f = f + _dot((_silu(gate) * up).astype(BF16), wd_ref[c0:c0 + _FF_CHUNK, :])
    o_ref[...] = x1 + _rms(f, gqf_ref[...])


def _outffn(x, ya, yb, yc, wo, wg, wu, wd, gpm, gpf, gqf, *, tm):
    m, d = x.shape
    row = lambda i: (i, 0)
    const = lambda i: (0, 0)
    whole = lambda a: pl.BlockSpec(a.shape, const, pipeline_mode=pl.Buffered(1))
    yc_transposed = yc.ndim == 3
    if yc_transposed:
        nt = yc.shape[2] // tm
        yc_spec = pl.BlockSpec((1, HW, tm), lambda i: (i // nt, 0, i % nt))
    else:
        yc_spec = pl.BlockSpec((tm, HW), row)
    return pl.pallas_call(
        functools.partial(_outffn_kernel, yc_transposed=yc_transposed), grid=(m // tm,),
        in_specs=[pl.BlockSpec((tm, d), row), pl.BlockSpec((tm, CONV_A_WIDTH), row),
                  pl.BlockSpec((tm, HW), row), yc_spec,
                  whole(wo), whole(wg), whole(wu), whole(wd),
                  pl.BlockSpec((1, d), const), pl.BlockSpec((1, d), const), pl.BlockSpec((1, d), const)],
        out_specs=pl.BlockSpec((tm, d), row),
        out_shape=jax.ShapeDtypeStruct((m, d), F32),
        compiler_params=_params(("arbitrary",)), name="outffn",
    )(x, ya, yb, yc, wo, wg, wu, wd, gpm, gpf, gqf)


def _layer_weights(l, w_in, conv_a_w, conv_a_b, ln_a_g, ln_a_b, conv_b_w, a_log, dt_bias, norm_b_g, f_bias,
                   w_out, g_pre_mix, g_post_mix, g_pre_ffn, g_post_ffn, w_gate, w_up, w_down):
    wt = w_in[l].T
    wmain = jnp.concatenate([wt[0:2048], wt[2444:2828]], axis=0).astype(BF16)
    wq = wt[2060:2444].astype(BF16)
    wkv = wt[2444:3212].astype(BF16)
    z2 = jnp.zeros((2, wt.shape[1]), F32)
    wsm = jnp.concatenate([wt[3212:3218], z2, wt[2048:2054], z2, wt[2054:2060], z2,
                           jnp.zeros((LANES - 24, wt.shape[1]), F32)], axis=0).astype(BF16)
    z2v = jnp.zeros((2,), F32)
    z8v = jnp.zeros((8,), F32)
    add24 = jnp.concatenate([f_bias[l], z2v, z8v, dt_bias[l], z2v])
    alog24 = jnp.concatenate([z8v, z8v, a_log[l], z2v])
    pad_row = lambda v: jnp.concatenate([v, jnp.zeros((LANES - 24,), F32)])[None, :]
    return dict(
        g_pre_mix=g_pre_mix[l][None, :], wmain=wmain, wq=wq, wkv=wkv, wsm=wsm,
        addc=add24[:, None], alogc=alog24[:, None], addr=pad_row(add24), alogr=pad_row(alog24),
        conv_a_w=jnp.concatenate([conv_a_w[l], jnp.zeros((1, CONV_A_WIDTH), F32)], axis=0),
        conv_a_b=conv_a_b[l][None, :], ln_a_g=ln_a_g[l][None, :], ln_a_b=ln_a_b[l][None, :],
        conv_b_w=conv_b_w[l], norm_b_g=jnp.tile(norm_b_g[l], N_HEADS)[None, :],
        w_out=w_out[l].astype(BF16), w_gate=w_gate[l].astype(BF16), w_up=w_up[l].astype(BF16),
        w_down=w_down[l].astype(BF16),
        g_post_mix=g_post_mix[l][None, :], g_pre_ffn=g_pre_ffn[l][None, :], g_post_ffn=g_post_ffn[l][None, :],
    )


def _state_to_pairs(s):
    nb = s.shape[0]
    s = s.reshape(nb, N_PAIRS, 2, HEAD_DIM, HEAD_DIM)
    z = jnp.zeros_like(s[:, :, 0])
    top = jnp.concatenate([s[:, :, 0], z], axis=-1)
    bot = jnp.concatenate([z, s[:, :, 1]], axis=-1)
    return jnp.concatenate([top, bot], axis=-2)


def _pairs_to_state(sp):
    nb = sp.shape[0]
    s0 = sp[:, :, :HEAD_DIM, :HEAD_DIM]
    s1 = sp[:, :, HEAD_DIM:, HEAD_DIM:]
    return jnp.stack([s0, s1], axis=2).reshape(nb, N_HEADS, HEAD_DIM, HEAD_DIM)


def _pad_rows_front(buf, rows):
    nb, r, c = buf.shape
    return jnp.concatenate([jnp.zeros((nb, rows - r, c), buf.dtype), buf], axis=1)


def _prompt_layer(x, p, *, tm, tq, tt_a, tt_b):
    nb, t, d = x.shape
    xf = x.reshape(nb * t, d)
    u, qkvb, zb, kr, qxt, vx, kt, vt, smt, smr = _inproj(
        xf, p["g_pre_mix"], p["wmain"], p["wq"], p["wkv"], p["wsm"], p["addc"], p["alogc"], p["addr"], p["alogr"],
        nb=nb, tm=tm, decode=False)
    ya, nbuf_a = _conv_a(u.reshape(nb, t, -1), jnp.zeros((nb, _CA_PAD, CONV_A_WIDTH), F32), p["conv_a_w"],
                         p["conv_a_b"], p["ln_a_g"], p["ln_a_b"], tt=tt_a)
    yb, nbuf_b, s_new = _delta(qkvb.reshape(nb, t, -1), jnp.zeros((nb, _CB_PAD, 3 * HW), F32), p["conv_b_w"],
                               smr.reshape(nb, t, LANES), smt, zb.reshape(nb, t, HW), p["norm_b_g"],
                               jnp.zeros((nb, N_PAIRS, LANES, LANES), F32), tt=tt_b)
    kx = _fox_keys(smr.reshape(nb, t, LANES), kr.reshape(nb, t, HW), tl=tq)
    yc = _fox_prompt(kx, qxt, vx, tq=tq)
    y = _outffn(xf, ya.reshape(nb * t, -1), yb.reshape(nb * t, -1), yc, p["w_out"],
                p["w_gate"], p["w_up"], p["w_down"], p["g_post_mix"], p["g_pre_ffn"], p["g_post_ffn"], tm=tm)
    k_out = kt.reshape(nb, N_HEADS, HEAD_DIM, t).transpose(0, 3, 1, 2)
    v_out = vt.reshape(nb, N_HEADS, HEAD_DIM, t).transpose(0, 3, 1, 2)
    logf_out = smt[:, 0:N_HEADS, :].transpose(0, 2, 1)
    states = (k_out, v_out, logf_out, nbuf_a[:, _CA_PAD - (CONV_A_KERNEL - 1):],
              nbuf_b[:, _CB_PAD - (SHORT_CONV - 1):], _pairs_to_state(s_new))
    return y.reshape(nb, t, d), states


def _sample_layer(x, p, layer, buf_a, buf_b, s_delta, cache_kt, cache_vt, cache_lcs, page_table, *, group):
    bd, _, d = x.shape
    xf = x.reshape(bd, d)
    u, qkvb, zb, qx, kt, vt, smt, smr = _inproj(
        xf, p["g_pre_mix"], p["wmain"], p["wq"], p["wkv"], p["wsm"], p["addc"], p["alogc"], p["addr"], p["alogr"],
        nb=1, tm=bd, decode=True)
    st_a = jnp.transpose(buf_a, (1, 0, 2))
    ya = _conv_a_decode(u, st_a, p["conv_a_w"], p["conv_a_b"], p["ln_a_g"], p["ln_a_b"])
    new_buf_a = jnp.concatenate([buf_a[:, 1:], u[:, None, :]], axis=1)
    st_b = jnp.transpose(buf_b, (1, 0, 2))
    yb, s_new = _delta_decode(qkvb, st_b, p["conv_b_w"], smr, zb, p["norm_b_g"], _state_to_pairs(s_delta))
    new_buf_b = jnp.concatenate([buf_b[:, 1:], qkvb[:, None, :]], axis=1)

    k_new = kt[0].T
    v_new = vt[0].T
    lf_new = smt[0, 0:8, :].T
    yc = _fox_decode(page_table, layer, qx[:, None, :], cache_kt, cache_vt, cache_lcs, k_new[:, None, :],
                     v_new[:, None, :], jnp.broadcast_to(lf_new[:, :, None], (bd, 8, LANES)), group=group)
    y = _outffn(xf, ya, yb, yc.reshape(bd, -1), p["w_out"], p["w_gate"], p["w_up"], p["w_down"],
                p["g_post_mix"], p["g_pre_ffn"], p["g_post_ffn"], tm=bd)
    states = (k_new.reshape(bd, 1, N_HEADS, HEAD_DIM), v_new.reshape(bd, 1, N_HEADS, HEAD_DIM),
              lf_new[:, None, 0:N_HEADS], new_buf_a, new_buf_b, _pairs_to_state(s_new))
    return y.reshape(bd, 1, d), states


def _forward(x_prompt, x_sample, cache_k, cache_v, cache_logf, page_table, state_conv_a, state_conv_b,
             state_delta, weights, *, tm, tq, tt_a, tt_b, group):
    depth = cache_k.shape[0]
    n_phys = cache_k.shape[1]
    ckt = jnp.transpose(cache_k, (0, 1, 3, 4, 2)).reshape(depth, n_phys, HW, PAGE_SIZE)
    cvt = jnp.transpose(cache_v, (0, 1, 3, 4, 2)).reshape(depth, n_phys, HW, PAGE_SIZE)
    clf = jnp.transpose(cache_logf, (0, 1, 3, 2))
    clf = jnp.concatenate([clf, jnp.zeros((depth, n_phys, 8 - N_HEADS, PAGE_SIZE), F32)], axis=2)
    n_rows = depth * n_phys * 8
    rows = max(r for r in range(8, min(n_rows, 2048) + 1, 8) if n_rows % r == 0)
    clcs = _page_cumsum(clf.reshape(n_rows, PAGE_SIZE), rows=rows).reshape(depth, n_phys, 8, PAGE_SIZE)
    xp, xs = x_prompt, x_sample
    prompt_states, sample_states = [], []
    for l in range(depth):
        p = _layer_weights(l, *weights)
        xp, sp = _prompt_layer(xp, p, tm=tm, tq=tq, tt_a=tt_a, tt_b=tt_b)
        prompt_states.append(sp)
        xs, ss = _sample_layer(xs, p, l, state_conv_a[l], state_conv_b[l], state_delta[l], ckt, cvt, clcs,
                               page_table, group=group)
        sample_states.append(ss)
    ps = [jnp.stack(t) for t in zip(*prompt_states)]
    ss = [jnp.stack(t) for t in zip(*sample_states)]
    return (xp, xs, *ps, *ss)


def kernel(x_prompt, x_sample, cache_k, cache_v, cache_logf, page_table, state_conv_a, state_conv_b, state_delta,
           w_in, conv_a_w, conv_a_b, ln_a_g, ln_a_b, conv_b_w, a_log, dt_bias, norm_b_g, f_bias, w_out, g_pre_mix,
           g_post_mix, g_pre_ffn, g_post_ffn, w_gate, w_up, w_down):
    weights = (w_in, conv_a_w, conv_a_b, ln_a_g, ln_a_b, conv_b_w, a_log, dt_bias, norm_b_g, f_bias, w_out,
               g_pre_mix, g_post_mix, g_pre_ffn, g_post_ffn, w_gate, w_up, w_down)
    return _forward(x_prompt, x_sample, cache_k, cache_v, cache_logf, page_table, state_conv_a, state_conv_b,
                    state_delta, weights, tm=512, tq=512, tt_a=512, tt_b=256, group=16)
```

```python
import functools

import jax
import jax.numpy as jnp
from jax import lax
from jax.experimental import pallas as pl
from jax.experimental.pallas import tpu as pltpu

F32 = jnp.float32
BF16 = jnp.bfloat16

HEAD_DIM = 64
N_HEADS = 6
N_PAIRS = N_HEADS // 2
HW = N_HEADS * HEAD_DIM
CONV_A_WIDTH = 256
CONV_A_KERNEL = 31
SHORT_CONV = 4
DELTA_CHUNK = 64
PAGE_SIZE = 128
LANES = 128
SUBLANES = 8
RMS_EPS = 1e-6
LN_EPS = 1e-5
VMEM_LIMIT_BYTES = 56 * 1024 * 1024
NEG_INF = float("-inf")
LOG2E = 1.4426950408889634

_NT = (((1,), (1,)), ((), ()))


def _dot(a, b):
    return jnp.dot(a, b, preferred_element_type=F32)


def _dot_nt(a, b):
    return lax.dot_general(a, b, _NT, preferred_element_type=F32)


def _sigmoid(x):
    return 1.0 / (1.0 + jnp.exp(-x))


def _softplus(x):
    return jnp.maximum(x, 0.0) + jnp.log(1.0 + jnp.exp(-jnp.abs(x)))


def _silu(x):
    return x * _sigmoid(x)


def _rms(x, g):
    return x * lax.rsqrt(jnp.mean(x * x, axis=-1, keepdims=True) + RMS_EPS) * g


def _params(sem):
    return pltpu.CompilerParams(dimension_semantics=sem, vmem_limit_bytes=VMEM_LIMIT_BYTES)


def _iota(shape, dim):
    return lax.broadcasted_iota(jnp.int32, shape, dim)


def _split_hi_lo(x):
    hi = x.astype(BF16)
    lo = (x - hi.astype(F32)).astype(BF16)
    return hi, lo


def _split3(x):
    hi = x.astype(BF16)
    r = x - hi.astype(F32)
    mid = r.astype(BF16)
    lo = (r - mid.astype(F32)).astype(BF16)
    return hi, mid, lo


def _group_sum(x2, gmat):
    hi, lo = _split_hi_lo(x2)
    return _dot(hi, gmat) + _dot(lo, gmat)


_MAIN_SPLITS = (0, 256, 512, 1664, 2048, 2432)


def _gate_fns(raw, add, a_log):
    x = raw + add
    logf = -_softplus(-x)
    beta = _sigmoid(x)
    g = -jnp.exp(a_log) * _softplus(x)
    return logf, beta, g


_VX_ROWS = 80
_BIAS_ROWS = 3
_FOX_KEY_SPLIT = 2
_FOX_Q_BLOCK = 128


def _inproj_kernel(x_ref, g_ref, wmain_ref, wq_ref, wkv_ref, wsm_ref, addc_ref, alogc_ref, addr_ref, alogr_ref,
                   *refs, decode, nt):
    hb = _rms(x_ref[...], g_ref[...]).astype(BF16)
    tm = hb.shape[0]
    if decode:
        u_ref, qkvb_ref, zb_ref, q_ref, kt_ref, vt_ref, smt_ref, smr_ref = refs
    else:
        (tril_ref, place_ref, u_ref, qkvb_ref, zb_ref, kx_ref, qxt_ref, vx_ref, kt_ref, vt_ref, smt_ref, smr_ref,
         carry_ref) = refs

    def proj(i):
        return _dot_nt(hb, wmain_ref[_MAIN_SPLITS[i]:_MAIN_SPLITS[i + 1], :])

    u_ref[...] = proj(0) * _sigmoid(proj(1))
    qkvb_ref[...] = proj(2)
    zb_ref[...] = proj(3)
    kv = _dot_nt(wkv_ref[...], hb)

    rawr = _dot_nt(hb, wsm_ref[...])
    logf, beta, g = _gate_fns(rawr, addr_ref[...], alogr_ref[...])
    col = _iota((1, LANES), 1)
    smr = jnp.where(col < 8, logf, jnp.where(col < 16, beta, g))
    smr_ref[...] = smr

    if decode:
        q_ref[...] = _dot_nt(hb, wq_ref[...]) * (HEAD_DIM ** -0.5)
    else:
        @pl.when(lax.rem(pl.program_id(0), nt) == 0)
        def _():
            carry_ref[...] = jnp.zeros_like(carry_ref)

        l_hi, l_mid, l_lo = _split3(smr)
        tril = tril_ref[...]
        cs = _dot(tril, l_hi) + _dot(tril, l_mid) + _dot(tril, l_lo) + carry_ref[0:1, :]
        carry_ref[...] = jnp.broadcast_to(cs[tm - 1:tm, :], carry_ref.shape)
        hi, mid, lo = _split3(jnp.where(col < N_HEADS, cs * (-LOG2E), 0.0))
        src = jnp.concatenate([proj(4).astype(BF16), hi, mid, lo], axis=1)
        kx_ref[...] = _dot(src, place_ref[...]).astype(BF16)
        qt = _dot_nt(wq_ref[...], hb) * (HEAD_DIM ** -0.5 * LOG2E)
        ones_rows = jnp.where(_iota((LANES - HEAD_DIM, tm), 0) < _BIAS_ROWS, 1.0, 0.0).astype(BF16)
        for h in range(N_HEADS):
            qxt_ref[0, h * LANES:h * LANES + HEAD_DIM, :] = qt[h * HEAD_DIM:(h + 1) * HEAD_DIM].astype(BF16)
            qxt_ref[0, h * LANES + HEAD_DIM:(h + 1) * LANES, :] = ones_rows
            vx_ref[0, h * _VX_ROWS:h * _VX_ROWS + HEAD_DIM, :] = kv[HW + h * HEAD_DIM:HW + (h + 1) * HEAD_DIM].astype(BF16)
            vx_ref[0, h * _VX_ROWS + HEAD_DIM:(h + 1) * _VX_ROWS, :] = jnp.ones((_VX_ROWS - HEAD_DIM, tm), BF16)
    kt_ref[0] = kv[:HW]
    vt_ref[0] = kv[HW:]

    rawt = _dot_nt(wsm_ref[0:24, :], hb)
    logf, beta, g = _gate_fns(rawt, addc_ref[...], alogc_ref[...])
    smt_ref[0, 0:8] = logf[0:8]
    smt_ref[0, 8:16] = beta[8:16]
    smt_ref[0, 16:24] = g[16:24]


def _inproj(x, g, wmain, wq, wkv, wsm, addc, alogc, addr, alogr, *, nb, tm, decode):
    m, d = x.shape
    t = m // nb
    nt = t // tm
    row = lambda i: (i, 0)
    const = lambda i: (0, 0)
    tr = lambda i: (i // nt, 0, i % nt)
    outs = [jax.ShapeDtypeStruct((m, CONV_A_WIDTH), F32), jax.ShapeDtypeStruct((m, 3 * HW), F32),
            jax.ShapeDtypeStruct((m, HW), F32)]
    out_specs = [pl.BlockSpec((tm, CONV_A_WIDTH), row), pl.BlockSpec((tm, 3 * HW), row), pl.BlockSpec((tm, HW), row)]
    extra_in, extra_specs, scratch = [], [], []
    if decode:
        outs += [jax.ShapeDtypeStruct((m, HW), F32)]
        out_specs += [pl.BlockSpec((tm, HW), row)]
    else:
        pos = jnp.arange(tm)
        tril = (pos[:, None] >= pos[None, :]).astype(BF16)
        src = jnp.arange(HW)
        dst = (src // HEAD_DIM) * LANES + src % HEAD_DIM
        lanes_x = jnp.arange(N_HEADS * LANES)
        pk = (dst[:, None] == lanes_x[None, :]).astype(BF16)
        col = jnp.arange(LANES)
        pb = jnp.stack([((col[:, None] < N_HEADS)
                         & (col[:, None] * LANES + HEAD_DIM + piece == lanes_x[None, :])).astype(BF16)
                        for piece in range(_BIAS_ROWS)])
        place = jnp.concatenate([pk, pb[0], pb[1], pb[2]], axis=0)
        extra_in = [tril, place]
        extra_specs = [pl.BlockSpec(tril.shape, const), pl.BlockSpec(place.shape, const)]
        scratch = [pltpu.VMEM((8, LANES), F32)]
        outs += [jax.ShapeDtypeStruct((m, N_HEADS * LANES), BF16),
                 jax.ShapeDtypeStruct((nb, N_HEADS * LANES, t), BF16),
                 jax.ShapeDtypeStruct((nb, N_HEADS * _VX_ROWS, t), BF16)]
        out_specs += [pl.BlockSpec((tm, N_HEADS * LANES), row), pl.BlockSpec((1, N_HEADS * LANES, tm), tr),
                      pl.BlockSpec((1, N_HEADS * _VX_ROWS, tm), tr)]
    outs += [jax.ShapeDtypeStruct((nb, HW, t), F32), jax.ShapeDtypeStruct((nb, HW, t), F32),
             jax.ShapeDtypeStruct((nb, 24, t), F32), jax.ShapeDtypeStruct((m, LANES), F32)]
    out_specs += [pl.BlockSpec((1, HW, tm), tr), pl.BlockSpec((1, HW, tm), tr),
                  pl.BlockSpec((1, 24, tm), tr), pl.BlockSpec((tm, LANES), row)]
    in_specs = [
        pl.BlockSpec((tm, d), row), pl.BlockSpec((1, d), const),
        pl.BlockSpec(wmain.shape, const), pl.BlockSpec(wq.shape, const), pl.BlockSpec(wkv.shape, const),
        pl.BlockSpec(wsm.shape, const),
        pl.BlockSpec((24, 1), const), pl.BlockSpec((24, 1), const),
        pl.BlockSpec((1, LANES), const), pl.BlockSpec((1, LANES), const),
    ] + extra_specs
    return pl.pallas_call(
        functools.partial(_inproj_kernel, decode=decode, nt=nt), grid=(m // tm,), in_specs=in_specs,
        out_specs=out_specs, out_shape=outs, scratch_shapes=scratch,
        compiler_params=_params(("arbitrary",)), name="inproj",
    )(x, g, wmain, wq, wkv, wsm, addc, alogc, addr, alogr, *extra_in)


_CA_PAD = 32
_CA_ROWS = 64


def _conv_a_kernel(u_ref, buf_ref, w_ref, b_ref, lg_ref, lb_ref, y_ref, nbuf_ref, ext_ref, *scratch, tt):
    ti = pl.program_id(1)

    @pl.when(ti == 0)
    def _():
        ext_ref[0:_CA_PAD] = buf_ref[0]

    ext_ref[_CA_PAD:_CA_PAD + tt] = u_ref[0]
    off = _CA_PAD - (CONV_A_KERNEL - 1)
    rows = min(_CA_ROWS, tt)
    aligned = tt % SUBLANES == 0
    if aligned:
        (sh_ref,) = scratch
        span = tt + _CA_PAD - SUBLANES
        for r in range(1, SUBLANES):
            sh_ref[r - 1] = ext_ref[r:r + span]
    for c in range(tt // rows):
        acc = jnp.zeros((rows, CONV_A_WIDTH), F32)
        for j in range(CONV_A_KERNEL):
            o = off + j + c * rows
            if aligned and o % SUBLANES:
                tap = sh_ref[o % SUBLANES - 1, o - o % SUBLANES:o - o % SUBLANES + rows, :]
            else:
                tap = ext_ref[o:o + rows, :]
            acc = acc + w_ref[j:j + 1, :] * tap
        ca = acc + b_ref[...]
        mu = jnp.mean(ca, axis=-1, keepdims=True)
        xc = ca - mu
        var = jnp.mean(xc * xc, axis=-1, keepdims=True)
        y_ref[0, c * rows:(c + 1) * rows, :] = _silu(xc * lax.rsqrt(var + LN_EPS) * lg_ref[...] + lb_ref[...])
    tail = ext_ref[tt:tt + _CA_PAD]
    nbuf_ref[0] = tail
    ext_ref[0:_CA_PAD] = tail


def _conv_a(u, buf, w, b, lg, lb, *, tt):
    nb, t, c = u.shape
    const = lambda bi, ti: (0, 0)
    return pl.pallas_call(
        functools.partial(_conv_a_kernel, tt=tt),
        grid=(nb, t // tt),
        in_specs=[pl.BlockSpec((1, tt, c), lambda bi, ti: (bi, ti, 0)),
                  pl.BlockSpec((1, _CA_PAD, c), lambda bi, ti: (bi, 0, 0)),
                  pl.BlockSpec((_CA_PAD, c), const), pl.BlockSpec((1, c), const),
                  pl.BlockSpec((1, c), const), pl.BlockSpec((1, c), const)],
        out_specs=[pl.BlockSpec((1, tt, c), lambda bi, ti: (bi, ti, 0)),
                   pl.BlockSpec((1, _CA_PAD, c), lambda bi, ti: (bi, 0, 0))],
        out_shape=[jax.ShapeDtypeStruct((nb, t, c), F32), jax.ShapeDtypeStruct((nb, _CA_PAD, c), F32)],
        scratch_shapes=[pltpu.VMEM((_CA_PAD + tt, c), F32)]
        + ([pltpu.VMEM((SUBLANES - 1, tt + _CA_PAD - SUBLANES, c), F32)] if tt % SUBLANES == 0 else []),
        compiler_params=_params(("arbitrary", "arbitrary")), name="conv_a",
    )(u, buf, w, b, lg, lb)


_CB_PAD = 8


def _pair_cols(x, p):
    return x[:, p * LANES:(p + 1) * LANES]


def _head_cols(sm, base, p, lane_lo):
    c0 = sm[:, base + 2 * p:base + 2 * p + 1]
    c1 = sm[:, base + 2 * p + 1:base + 2 * p + 2]
    return jnp.where(lane_lo, c0, c1)


def _block_diag(x, lane_lo):
    return jnp.concatenate([jnp.where(lane_lo, x, 0.0), jnp.where(lane_lo, 0.0, x)], axis=0)


def _delta_kernel(x_ref, buf_ref, w_ref, smr_ref, smt_ref, z_ref, gb_ref, s0_ref, gmat_ref, tril_ref, triu_ref,
                  y_ref, nbuf_ref, sout_ref, ext_ref, s_ref, *, tt):
    ti = pl.program_id(1)
    nchunk = tt // DELTA_CHUNK
    c = DELTA_CHUNK

    @pl.when(ti == 0)
    def _():
        ext_ref[0:_CB_PAD] = buf_ref[0]
        s_ref[...] = s0_ref[0]

    ext_ref[_CB_PAD:_CB_PAD + tt] = x_ref[0]
    off = _CB_PAD - (SHORT_CONV - 1)
    acc = w_ref[0:1, :] * ext_ref[off:off + tt, :]
    for j in range(1, SHORT_CONV):
        acc = acc + w_ref[j:j + 1, :] * ext_ref[off + j:off + j + tt, :]
    tail = ext_ref[tt:tt + _CB_PAD]
    nbuf_ref[0] = tail
    ext_ref[0:_CB_PAD] = tail
    cb = _silu(acc)

    gmat = gmat_ref[...]
    q = cb[:, 0:HW]
    k = cb[:, HW:2 * HW]
    v = cb[:, 2 * HW:3 * HW]
    qs = q * lax.rsqrt(_group_sum(q * q, gmat) + 1e-6) * (HEAD_DIM ** -0.5)
    kn = k * lax.rsqrt(_group_sum(k * k, gmat) + 1e-6)

    smr = smr_ref[0]
    smt = smt_ref[0]
    s_hi, s_mid, s_lo = _split3(smr)
    gc_col = _dot(tril_ref[...], s_hi) + _dot(tril_ref[...], s_mid) + _dot(tril_ref[...], s_lo)
    t_hi, t_mid, t_lo = _split3(smt[16:24])
    gc_row = _dot(t_hi, triu_ref[...]) + _dot(t_mid, triu_ref[...]) + _dot(t_lo, triu_ref[...])
    if tt < LANES:
        gc_row = jnp.concatenate([gc_row, jnp.zeros((8, LANES - tt), F32)], axis=1)

    lane = _iota((1, LANES), 1)
    lane_lo = lane < HEAD_DIM
    ri = _iota((c, LANES), 0)
    ci_ = _iota((c, LANES), 1)
    cj = jnp.where(ci_ < HEAD_DIM, ci_, ci_ - HEAD_DIM)
    causal = ri >= cj
    strict = ri > cj
    eye2 = jnp.where(ri == cj, 1.0, 0.0)
    r128 = _iota((LANES, LANES), 0)
    c128 = _iota((LANES, LANES), 1)
    bdmask = (r128 < HEAD_DIM) == (c128 < HEAD_DIM)

    combos = [(ci, p) for ci in range(nchunk) for p in range(N_PAIRS)]
    pre = []
    for ci, p in combos:
        r0, r1 = ci * c, (ci + 1) * c
        qs_p = _pair_cols(qs, p)[r0:r1]
        kn_p = _pair_cols(kn, p)[r0:r1]
        v_p = _pair_cols(v, p)[r0:r1]
        beta = _head_cols(smr[r0:r1], 8, p, lane_lo)
        gi = _head_cols(gc_col[r0:r1], 16, p, lane_lo)
        blk = gc_row[:, (ci // 2) * LANES:(ci // 2 + 1) * LANES]
        rot = pltpu.roll(blk, HEAD_DIM, 1)
        if ci % 2 == 0:
            gj = jnp.where(lane_lo, blk[2 * p:2 * p + 1], rot[2 * p + 1:2 * p + 2])
        else:
            gj = jnp.where(lane_lo, rot[2 * p:2 * p + 1], blk[2 * p + 1:2 * p + 2])
        glast = gi[c - 1:c, :]
        eg = jnp.exp(gi)
        decay = jnp.exp(jnp.where(causal, gi - gj, NEG_INF))
        kb = kn_p * beta
        pre.append(dict(qs=qs_p, kn=kn_p, vb=v_p * beta, kb=kb, kbe=kb * eg, qe=qs_p * eg, eg_last=jnp.exp(glast),
                        kdec=kn_p * jnp.exp(glast - gi), decay=decay, kd=_block_diag(kn_p, lane_lo)))

    grams = [_dot_nt(jnp.concatenate([d["kb"], d["qs"]], axis=0), d["kd"]) for d in pre]
    lmats = [jnp.where(strict, g[0:c] * d["decay"], 0.0) for g, d in zip(grams, pre)]
    qks = [jnp.where(causal, g[c:2 * c] * d["decay"], 0.0) for g, d in zip(grams, pre)]

    def split_bd(m):
        hi, lo = _split_hi_lo(m)
        return hi, lo, _block_diag(hi, lane_lo), _block_diag(lo, lane_lo)

    def dot3(a_hi, a_lo, b_hi, b_lo):
        return _dot(a_hi, b_hi) + _dot(a_lo, b_hi) + _dot(a_hi, b_lo)

    xs = [eye2 - l for l in lmats]
    pws = [dot3(*split_bd(l)) for l in lmats]
    for level in range(4):
        psp = [split_bd(pw) for pw in pws]
        xsp = [_split_hi_lo(x) for x in xs]
        prods = [dot3(jnp.concatenate([ph, xh], axis=0), jnp.concatenate([plo, xl], axis=0), bh, bl)
                 for (ph, plo, bh, bl), (xh, xl) in zip(psp, xsp)]
        xs = [x + pr[c:2 * c] for x, pr in zip(xs, prods)]
        pws = [pr[0:c] for pr in prods]
    xs = [x + dot3(*_split_hi_lo(x), *split_bd(pw)[2:]) for pw, x in zip(pws, xs)]

    rhs_uw = [jnp.concatenate([_block_diag(d["vb"], lane_lo), _block_diag(d["kbe"], lane_lo)], axis=1) for d in pre]
    uws = [_dot(x, r) for x, r in zip(xs, rhs_uw)]
    kts = [d["kdec"].T for d in pre]
    kuws = [_dot(kt, uw) for kt, uw in zip(kts, uws)]
    kus = [jnp.where(bdmask, m[:, 0:LANES], 0.0) for m in kuws]
    kws = [jnp.where(bdmask, m[:, LANES:2 * LANES], 0.0) for m in kuws]
    rhs_q = [jnp.concatenate([_block_diag(uw[:, 0:LANES], lane_lo), _block_diag(uw[:, LANES:2 * LANES], lane_lo)],
                             axis=1) for uw in uws]
    qkuw = [_dot(qk, r) for qk, r in zip(qks, rhs_q)]
    o2s = [m[:, 0:LANES] for m in qkuw]
    q2s = [d["qe"] - m[:, LANES:2 * LANES] for d, m in zip(pre, qkuw)]

    states = [s_ref[p] for p in range(N_PAIRS)]
    o_chunks = []
    for ci in range(nchunk):
        o_pairs = []
        for p in range(N_PAIRS):
            i = ci * N_PAIRS + p
            s_bd = states[p]
            prod = _dot(jnp.concatenate([q2s[i], kws[i]], axis=0), s_bd)
            o_pairs.append(prod[0:c] + o2s[i])
            states[p] = s_bd * pre[i]["eg_last"] - prod[c:c + LANES] + kus[i]
        o_chunks.append(jnp.concatenate(o_pairs, axis=1))
    for p in range(N_PAIRS):
        s_ref[p] = states[p]
    o = jnp.concatenate(o_chunks, axis=0) if nchunk > 1 else o_chunks[0]
    ms = _group_sum(o * o, gmat) * (1.0 / HEAD_DIM)
    y_ref[0] = o * lax.rsqrt(ms + RMS_EPS) * gb_ref[...] * _silu(z_ref[0])
    sout_ref[0] = s_ref[...]


def _delta(x, buf, w, smr, smt, z, gb, s0, *, tt):
    nb, t, cw = x.shape
    const = lambda bi, ti: (0, 0)
    head = jnp.arange(HW) // HEAD_DIM
    gmat = (head[:, None] == head[None, :]).astype(BF16)
    chunk = jnp.arange(tt) // DELTA_CHUNK
    same = chunk[:, None] == chunk[None, :]
    pos = jnp.arange(tt)
    tril = (same & (pos[:, None] >= pos[None, :])).astype(BF16)
    triu = tril.T
    return pl.pallas_call(
        functools.partial(_delta_kernel, tt=tt),
        grid=(nb, t // tt),
        in_specs=[pl.BlockSpec((1, tt, cw), lambda bi, ti: (bi, ti, 0)),
                  pl.BlockSpec((1, _CB_PAD, cw), lambda bi, ti: (bi, 0, 0)),
                  pl.BlockSpec((SHORT_CONV, cw), const),
                  pl.BlockSpec((1, tt, LANES), lambda bi, ti: (bi, ti, 0)),
                  pl.BlockSpec((1, 24, tt), lambda bi, ti: (bi, 0, ti)),
                  pl.BlockSpec((1, tt, HW), lambda bi, ti: (bi, ti, 0)),
                  pl.BlockSpec((1, HW), const),
                  pl.BlockSpec((1, N_PAIRS, LANES, LANES), lambda bi, ti: (bi, 0, 0, 0)),
                  pl.BlockSpec((HW, HW), const), pl.BlockSpec((tt, tt), const), pl.BlockSpec((tt, tt), const)],
        out_specs=[pl.BlockSpec((1, tt, HW), lambda bi, ti: (bi, ti, 0)),
                   pl.BlockSpec((1, _CB_PAD, cw), lambda bi, ti: (bi, 0, 0)),
                   pl.BlockSpec((1, N_PAIRS, LANES, LANES), lambda bi, ti: (bi, 0, 0, 0))],
        out_shape=[jax.ShapeDtypeStruct((nb, t, HW), F32), jax.ShapeDtypeStruct((nb, _CB_PAD, cw), F32),
                   jax.ShapeDtypeStruct((nb, N_PAIRS, LANES, LANES), F32)],
        scratch_shapes=[pltpu.VMEM((_CB_PAD + tt, cw), F32), pltpu.VMEM((N_PAIRS, LANES, LANES), F32)],
        compiler_params=_params(("arbitrary", "arbitrary")), name="delta",
    )(x, buf, w, smr, smt, z, gb, s0, gmat, tril, triu)


_DEC_ROWS = 8


def _delta_decode_kernel(x_ref, buf_ref, w_ref, smr_ref, z_ref, gb_ref, s0_ref, gmat_ref, y_ref, sout_ref):
    xc = w_ref[SHORT_CONV - 1:SHORT_CONV, :] * x_ref[...]
    for j in range(SHORT_CONV - 1):
        xc = xc + w_ref[j:j + 1, :] * buf_ref[j]
    cb = _silu(xc)
    gmat = gmat_ref[...]
    q = cb[:, 0:HW]
    k = cb[:, HW:2 * HW]
    v = cb[:, 2 * HW:3 * HW]
    qs = q * lax.rsqrt(_group_sum(q * q, gmat) + 1e-6) * (HEAD_DIM ** -0.5)
    kn = k * lax.rsqrt(_group_sum(k * k, gmat) + 1e-6)
    qk = _group_sum(qs * kn, gmat)
    smr = smr_ref[...]
    lane_lo = _iota((1, LANES), 1) < HEAD_DIM
    row = _iota((_DEC_ROWS, LANES), 0)
    r128 = _iota((LANES, LANES), 0)
    c128 = _iota((LANES, LANES), 1)
    bdmask = (r128 < HEAD_DIM) == (c128 < HEAD_DIM)
    o_pairs = []
    for p in range(N_PAIRS):
        qs_p, kn_p, v_p, qk_p = (_pair_cols(t, p) for t in (qs, kn, v, qk))
        beta = _head_cols(smr, 8, p, lane_lo)
        a = jnp.exp(_head_cols(smr, 16, p, lane_lo))
        kn_t = kn_p.T
        o_p = jnp.zeros((_DEC_ROWS, LANES), F32)
        for r in range(_DEC_ROWS):
            s_bd = s0_ref[r, p]
            kq = jnp.where(row == 0, kn_p[r:r + 1], jnp.where(row == 1, qs_p[r:r + 1], 0.0))
            prod = _dot(kq, s_bd)
            a_r = a[r:r + 1]
            v_new = beta[r:r + 1] * (v_p[r:r + 1] - a_r * prod[0:1])
            o_p = jnp.where(row == r, a_r * prod[1:2] + qk_p[r:r + 1] * v_new, o_p)
            sout_ref[r, p] = jnp.where(bdmask, s_bd * a_r + kn_t[:, r:r + 1] * v_new, 0.0)
        o_pairs.append(o_p)
    o = jnp.concatenate(o_pairs, axis=1)
    ms = _group_sum(o * o, gmat) * (1.0 / HEAD_DIM)
    y_ref[...] = o * lax.rsqrt(ms + RMS_EPS) * gb_ref[...] * _silu(z_ref[...])


def _delta_decode(x, buf, w, smr, z, gb, s0):
    bd, cw = x.shape
    head = jnp.arange(HW) // HEAD_DIM
    gmat = (head[:, None] == head[None, :]).astype(BF16)
    rows = lambda i: (i, 0)
    const = lambda i: (0, 0)
    return pl.pallas_call(
        _delta_decode_kernel, grid=(bd // _DEC_ROWS,),
        in_specs=[pl.BlockSpec((_DEC_ROWS, cw), rows),
                  pl.BlockSpec((SHORT_CONV - 1, _DEC_ROWS, cw), lambda i: (0, i, 0)),
                  pl.BlockSpec((SHORT_CONV, cw), const), pl.BlockSpec((_DEC_ROWS, LANES), rows),
                  pl.BlockSpec((_DEC_ROWS, HW), rows), pl.BlockSpec((1, HW), const),
                  pl.BlockSpec((_DEC_ROWS, N_PAIRS, LANES, LANES), lambda i: (i, 0, 0, 0)),
                  pl.BlockSpec((HW, HW), const)],
        out_specs=[pl.BlockSpec((_DEC_ROWS, HW), rows),
                   pl.BlockSpec((_DEC_ROWS, N_PAIRS, LANES, LANES), lambda i: (i, 0, 0, 0))],
        out_shape=[jax.ShapeDtypeStruct((bd, HW), F32), jax.ShapeDtypeStruct((bd, N_PAIRS, LANES, LANES), F32)],
        compiler_params=_params(("arbitrary",)), name="delta_decode",
    )(x, buf, w, smr, z, gb, s0, gmat)


def _conv_a_decode_kernel(u_ref, st_ref, w_ref, b_ref, lg_ref, lb_ref, y_ref):
    acc = w_ref[CONV_A_KERNEL - 1:CONV_A_KERNEL, :] * u_ref[...]
    for j in range(CONV_A_KERNEL - 1):
        acc = acc + w_ref[j:j + 1, :] * st_ref[j]
    ca = acc + b_ref[...]
    mu = jnp.mean(ca, axis=-1, keepdims=True)
    xc = ca - mu
    var = jnp.mean(xc * xc, axis=-1, keepdims=True)
    y_ref[...] = _silu(xc * lax.rsqrt(var + LN_EPS) * lg_ref[...] + lb_ref[...])


def _conv_a_decode(u, st, w, b, lg, lb):
    bd, c = u.shape
    whole = lambda a: pl.BlockSpec(a.shape, lambda i: (0,) * a.ndim)
    return pl.pallas_call(
        _conv_a_decode_kernel, grid=(1,),
        in_specs=[whole(u), whole(st), whole(w), whole(b), whole(lg), whole(lb)],
        out_specs=whole(u), out_shape=jax.ShapeDtypeStruct((bd, c), F32),
        compiler_params=_params(("arbitrary",)), name="conv_a_decode",
    )(u, st, w, b, lg, lb)


def _fox_kernel(qi_ref, ki_ref, kx_ref, qxt_ref, vx_ref, o_ref, m_ref, acc_ref, *, tq, tk):
    step = pl.program_id(1)
    qi = qi_ref[step]
    ki = ki_ref[step]

    @pl.when(ki == 0)
    def _():
        m_ref[...] = jnp.full_like(m_ref, NEG_INF)
        acc_ref[...] = jnp.zeros_like(acc_ref)

    def update(masked):
        if masked:
            keep = _iota((tk, tq), 0) <= _iota((tk, tq), 1)
        combos = [(h, qc) for h in range(N_HEADS) for qc in range(tq // _FOX_Q_BLOCK)]
        half = tk // _FOX_KEY_SPLIT
        ms = [m_ref[h:h + 1, qc * _FOX_Q_BLOCK:(qc + 1) * _FOX_Q_BLOCK] for h, qc in combos]
        accs = [acc_ref[h, :, qc * _FOX_Q_BLOCK:(qc + 1) * _FOX_Q_BLOCK] for h, qc in combos]
        for kh in range(_FOX_KEY_SPLIT):
            k0 = kh * half
            live = [i for i, (h, qc) in enumerate(combos) if not masked or (qc + 1) * _FOX_Q_BLOCK > k0]
            sts = [_dot(kx_ref[0, k0:k0 + half, combos[i][0] * LANES:(combos[i][0] + 1) * LANES],
                        qxt_ref[0, combos[i][0] * LANES:(combos[i][0] + 1) * LANES,
                                combos[i][1] * _FOX_Q_BLOCK:(combos[i][1] + 1) * _FOX_Q_BLOCK]) for i in live]
            if masked:
                sts = [jnp.where(keep[k0:k0 + half, combos[i][1] * _FOX_Q_BLOCK:(combos[i][1] + 1) * _FOX_Q_BLOCK],
                                 st, NEG_INF) for st, i in zip(sts, live)]
            m_news = [jnp.maximum(ms[i], jnp.max(st, axis=0, keepdims=True)) for i, st in zip(live, sts)]
            pts = [jnp.exp2(st - mn).astype(BF16) for st, mn in zip(sts, m_news)]
            for i, mn, pt in zip(live, m_news, pts):
                h = combos[i][0]
                accs[i] = (jnp.exp2(ms[i] - mn) * accs[i]
                           + _dot(vx_ref[0, h * _VX_ROWS:(h + 1) * _VX_ROWS, k0:k0 + half], pt))
                ms[i] = mn
        for (h, qc), mn, acc in zip(combos, ms, accs):
            acc_ref[h, :, qc * _FOX_Q_BLOCK:(qc + 1) * _FOX_Q_BLOCK] = acc
            m_ref[h:h + 1, qc * _FOX_Q_BLOCK:(qc + 1) * _FOX_Q_BLOCK] = mn

    @pl.when(ki < qi)
    def _():
        update(False)

    @pl.when(ki == qi)
    def _():
        update(True)
        for h in range(N_HEADS):
            acc = acc_ref[h]
            o_ref[0, h * HEAD_DIM:(h + 1) * HEAD_DIM, :] = acc[0:HEAD_DIM] / acc[HEAD_DIM:HEAD_DIM + 1]


def _fox_prompt(kx, qxt, vx, *, tq):
    nb, t, _ = kx.shape
    nq = t // tq
    qi_tab = jnp.asarray([qi for qi in range(nq) for _ in range(qi + 1)], jnp.int32)
    ki_tab = jnp.asarray([ki for qi in range(nq) for ki in range(qi + 1)], jnp.int32)
    grid_spec = pltpu.PrefetchScalarGridSpec(
        num_scalar_prefetch=2, grid=(nb, int(qi_tab.shape[0])),
        in_specs=[pl.BlockSpec((1, tq, N_HEADS * LANES), lambda b, s, qt, kt: (b, kt[s], 0)),
                  pl.BlockSpec((1, N_HEADS * LANES, tq), lambda b, s, qt, kt: (b, 0, qt[s])),
                  pl.BlockSpec((1, N_HEADS * _VX_ROWS, tq), lambda b, s, qt, kt: (b, 0, kt[s]))],
        out_specs=pl.BlockSpec((1, HW, tq), lambda b, s, qt, kt: (b, 0, qt[s])),
        scratch_shapes=[pltpu.VMEM((8, tq), F32), pltpu.VMEM((N_HEADS, _VX_ROWS, tq), F32)],
    )
    return pl.pallas_call(
        functools.partial(_fox_kernel, tq=tq, tk=tq), grid_spec=grid_spec,
        out_shape=jax.ShapeDtypeStruct((nb, HW, t), F32),
        compiler_params=_params(("arbitrary", "arbitrary")), name="fox_prompt",
    )(qi_tab, ki_tab, kx, qxt, vx)


def _page_cumsum_kernel(x_ref, triu_ref, o_ref):
    hi, mid, lo = _split3(x_ref[...])
    triu = triu_ref[...]
    o_ref[...] = _dot(hi, triu) + _dot(mid, triu) + _dot(lo, triu)


def _page_cumsum(lf, *, rows):
    r = lf.shape[0]
    pos = jnp.arange(PAGE_SIZE)
    triu = (pos[:, None] <= pos[None, :]).astype(BF16)
    return pl.pallas_call(
        _page_cumsum_kernel, grid=(r // rows,),
        in_specs=[pl.BlockSpec((rows, PAGE_SIZE), lambda i: (i, 0)),
                  pl.BlockSpec((PAGE_SIZE, PAGE_SIZE), lambda i: (0, 0))],
        out_specs=pl.BlockSpec((rows, PAGE_SIZE), lambda i: (i, 0)),
        out_shape=jax.ShapeDtypeStruct((r, PAGE_SIZE), F32),
        compiler_params=_params(("arbitrary",)), name="page_cumsum",
    )(lf, triu)


def _fox_decode_kernel(pt_ref, qx_ref, kn_ref, vn_ref, lfn_ref, kt_hbm, vt_hbm, lcs_hbm, o_ref,
                       kbuf, vbuf, lbuf, sem, qbd_ref, m_ref, l_ref, acc_ref, carry_ref, *, layer, n_steps, group):
    b = pl.program_id(0)
    j = pl.program_id(1)
    step = b * n_steps + j
    slot = lax.rem(step, 2)
    total = pl.num_programs(0) * n_steps

    def page_copies(bb, jj, to_slot, lookup):
        out = []
        for g in range(group):
            page = pt_ref[bb, jj * group + g] if lookup else 0
            out.append(pltpu.make_async_copy(kt_hbm.at[layer, page], kbuf.at[to_slot, g], sem.at[0, to_slot]))
            out.append(pltpu.make_async_copy(vt_hbm.at[layer, page], vbuf.at[to_slot, g], sem.at[1, to_slot]))
            out.append(pltpu.make_async_copy(lcs_hbm.at[layer, page], lbuf.at[to_slot, g], sem.at[2, to_slot]))
        return out

    @pl.when(step == 0)
    def _():
        for cp in page_copies(0, 0, 0, True):
            cp.start()

    @pl.when(step + 1 < total)
    def _():
        last = j == n_steps - 1
        for cp in page_copies(jnp.where(last, b + 1, b), jnp.where(last, 0, j + 1), 1 - slot, True):
            cp.start()

    for cp in page_copies(0, 0, slot, False):
        cp.wait()
    kt_refs = [kbuf.at[slot, g] for g in range(group)]
    vt_refs = [vbuf.at[slot, g] for g in range(group)]
    lcs_refs = [lbuf.at[slot, g] for g in range(group)]

    @pl.when(j == 0)
    def _():
        row = _iota((8, HW), 0)
        col = _iota((8, HW), 1)
        own = (col >= row * HEAD_DIM) & (col < (row + 1) * HEAD_DIM)
        qbd_ref[...] = jnp.where(own, qx_ref[0], 0.0)
        m_ref[...] = jnp.full_like(m_ref, NEG_INF)
        l_ref[...] = jnp.zeros_like(l_ref)
        acc_ref[...] = jnp.zeros_like(acc_ref)
        carry_ref[...] = jnp.zeros_like(carry_ref)

    qbd = qbd_ref[...]
    lcs = [lcs_refs[g][...] for g in range(group)]
    totals = [jnp.broadcast_to(x[:, PAGE_SIZE - 1:PAGE_SIZE], x.shape) for x in lcs]
    qk = [_dot(qbd, kt_refs[g][...]) for g in range(group)]
    c_run = carry_ref[...]
    scores = []
    for g in range(group):
        scores.append(qk[g] - (lcs[g] + c_run))
        c_run = c_run + totals[g]
    c_end = c_run[:, 0:1]
    s = jnp.concatenate(scores, axis=1)
    m_prev = m_ref[:, 0:1]
    m_new = jnp.maximum(m_prev, jnp.max(s, axis=-1, keepdims=True))
    alpha = jnp.exp(m_prev - m_new)
    pr = jnp.exp(s - m_new)
    l_new = alpha * l_ref[:, 0:1] + jnp.sum(pr, axis=-1, keepdims=True)
    pv = _dot_nt(pr[:, 0:PAGE_SIZE], vt_refs[0][...])
    for g in range(1, group):
        pv = pv + _dot_nt(pr[:, g * PAGE_SIZE:(g + 1) * PAGE_SIZE], vt_refs[g][...])
    acc_new = alpha * acc_ref[...] + pv
    m_ref[...] = jnp.broadcast_to(m_new, m_ref.shape)
    l_ref[...] = jnp.broadcast_to(l_new, l_ref.shape)
    acc_ref[...] = acc_new
    carry_ref[...] = c_run

    @pl.when(j == n_steps - 1)
    def _():
        s_n = jnp.sum(qbd_ref[...] * kn_ref[0], axis=-1, keepdims=True) - (c_end + lfn_ref[0][:, 0:1])
        m_f = jnp.maximum(m_new, s_n)
        a_f = jnp.exp(m_new - m_f)
        p_n = jnp.exp(s_n - m_f)
        l_f = a_f * l_new + p_n
        o8 = (a_f * acc_new + p_n * vn_ref[0]) / l_f
        row = _iota((8, HW), 0)
        col = _iota((8, HW), 1)
        own = (col >= row * HEAD_DIM) & (col < (row + 1) * HEAD_DIM)
        o_ref[0] = jnp.sum(jnp.where(own, o8, 0.0), axis=0, keepdims=True)


def _fox_decode(page_table, layer, qx, cache_kt, cache_vt, cache_lcs, k_new, v_new, lf_new, *, group):
    bd = qx.shape[0]
    n_pages = page_table.shape[1]
    n_steps = n_pages // group
    per_b = lambda b, j, pt: (b, 0, 0)
    hbm = pl.BlockSpec(memory_space=pl.ANY)
    grid_spec = pltpu.PrefetchScalarGridSpec(
        num_scalar_prefetch=1, grid=(bd, n_steps),
        in_specs=[pl.BlockSpec((1, 1, HW), per_b), pl.BlockSpec((1, 1, HW), per_b), pl.BlockSpec((1, 1, HW), per_b),
                  pl.BlockSpec((1, 8, LANES), per_b), hbm, hbm, hbm],
        out_specs=pl.BlockSpec((1, 1, HW), per_b),
        scratch_shapes=[pltpu.VMEM((2, group, HW, PAGE_SIZE), F32), pltpu.VMEM((2, group, HW, PAGE_SIZE), F32),
                        pltpu.VMEM((2, group, 8, PAGE_SIZE), F32), pltpu.SemaphoreType.DMA((3, 2)),
                        pltpu.VMEM((8, HW), F32), pltpu.VMEM((8, LANES), F32), pltpu.VMEM((8, LANES), F32),
                        pltpu.VMEM((8, HW), F32), pltpu.VMEM((8, LANES), F32)],
    )
    return pl.pallas_call(
        functools.partial(_fox_decode_kernel, layer=layer, n_steps=n_steps, group=group), grid_spec=grid_spec,
        out_shape=jax.ShapeDtypeStruct((bd, 1, HW), F32),
        compiler_params=_params(("arbitrary", "arbitrary")), name="fox_decode",
    )(page_table, qx, k_new, v_new, lf_new, cache_kt, cache_vt, cache_lcs)


_FF_CHUNK = 704


def _outffn_kernel(x_ref, ya_ref, yb_ref, yc_ref, wo_ref, wg_ref, wu_ref, wd_ref, gpm_ref, gpf_ref, gqf_ref, o_ref,
                   *, yc_transposed):
    a0 = CONV_A_WIDTH
    a1 = a0 + HW
    yc = yc_ref[0].T if yc_transposed else yc_ref[...]
    mix = (_dot(ya_ref[...].astype(BF16), wo_ref[0:a0, :]) + _dot(yb_ref[...].astype(BF16), wo_ref[a0:a1, :])
           + _dot(yc.astype(BF16), wo_ref[a1:a1 + HW, :]))
    x1 = x_ref[...] + _rms(mix, gpm_ref[...])
    hb = _rms(x1, gpf_ref[...]).astype(BF16)
    d_ff = wg_ref.shape[1]
    f = jnp.zeros_like(x1)
    for c0 in range(0, d_ff, _FF_CHUNK):
        gate = _dot(hb, wg_ref[:, c0:c0 + _FF_CHUNK])
        up = _dot(hb, wu_ref[:, c0:c0 + _FF_CHUNK])
        f = f + _dot((_silu(gate) * up).astype(BF16), wd_ref[c0:c0 + _FF_CHUNK, :])
    o_ref[...] = x1 + _rms(f, gqf_ref[...])


def _outffn(x, ya, yb, yc, wo, wg, wu, wd, gpm, gpf, gqf, *, tm):
    m, d = x.shape
    row = lambda i: (i, 0)
    const = lambda i: (0, 0)
    whole = lambda a: pl.BlockSpec(a.shape, const, pipeline_mode=pl.Buffered(1))
    yc_transposed = yc.ndim == 3
    if yc_transposed:
        nt = yc.shape[2] // tm
        yc_spec = pl.BlockSpec((1, HW, tm), lambda i: (i // nt, 0, i % nt))
    else:
        yc_spec = pl.BlockSpec((tm, HW), row)
    return pl.pallas_call(
        functools.partial(_outffn_kernel, yc_transposed=yc_transposed), grid=(m // tm,),
        in_specs=[pl.BlockSpec((tm, d), row), pl.BlockSpec((tm, CONV_A_WIDTH), row),
                  pl.BlockSpec((tm, HW), row), yc_spec,
                  whole(wo), whole(wg), whole(wu), whole(wd),
                  pl.BlockSpec((1, d), const), pl.BlockSpec((1, d), const), pl.BlockSpec((1, d), const)],
        out_specs=pl.BlockSpec((tm, d), row),
        out_shape=jax.ShapeDtypeStruct((m, d), F32),
        compiler_params=_params(("arbitrary",)), name="outffn",
    )(x, ya, yb, yc, wo, wg, wu, wd, gpm, gpf, gqf)


def _layer_weights(l, w_in, conv_a_w, conv_a_b, ln_a_g, ln_a_b, conv_b_w, a_log, dt_bias, norm_b_g, f_bias,
                   w_out, g_pre_mix, g_post_mix, g_pre_ffn, g_post_ffn, w_gate, w_up, w_down):
    wt = w_in[l].T
    wmain = jnp.concatenate([wt[0:2048], wt[2444:2828]], axis=0).astype(BF16)
    wq = wt[2060:2444].astype(BF16)
    wkv = wt[2444:3212].astype(BF16)
    z2 = jnp.zeros((2, wt.shape[1]), F32)
    wsm = jnp.concatenate([wt[3212:3218], z2, wt[2048:2054], z2, wt[2054:2060], z2,
                           jnp.zeros((LANES - 24, wt.shape[1]), F32)], axis=0).astype(BF16)
    z2v = jnp.zeros((2,), F32)
    z8v = jnp.zeros((8,), F32)
    add24 = jnp.concatenate([f_bias[l], z2v, z8v, dt_bias[l], z2v])
    alog24 = jnp.concatenate([z8v, z8v, a_log[l], z2v])
    pad_row = lambda v: jnp.concatenate([v, jnp.zeros((LANES - 24,), F32)])[None, :]
    return dict(
        g_pre_mix=g_pre_mix[l][None, :], wmain=wmain, wq=wq, wkv=wkv, wsm=wsm,
        addc=add24[:, None], alogc=alog24[:, None], addr=pad_row(add24), alogr=pad_row(alog24),
        conv_a_w=jnp.concatenate([conv_a_w[l], jnp.zeros((1, CONV_A_WIDTH), F32)], axis=0),
        conv_a_b=conv_a_b[l][None, :], ln_a_g=ln_a_g[l][None, :], ln_a_b=ln_a_b[l][None, :],
        conv_b_w=conv_b_w[l], norm_b_g=jnp.tile(norm_b_g[l], N_HEADS)[None, :],
        w_out=w_out[l].astype(BF16), w_gate=w_gate[l].astype(BF16), w_up=w_up[l].astype(BF16),
        w_down=w_down[l].astype(BF16),
        g_post_mix=g_post_mix[l][None, :], g_pre_ffn=g_pre_ffn[l][None, :], g_post_ffn=g_post_ffn[l][None, :],
    )


def _state_to_pairs(s):
    nb = s.shape[0]
    s = s.reshape(nb, N_PAIRS, 2, HEAD_DIM, HEAD_DIM)
    z = jnp.zeros_like(s[:, :, 0])
    top = jnp.concatenate([s[:, :, 0], z], axis=-1)
    bot = jnp.concatenate([z, s[:, :, 1]], axis=-1)
    return jnp.concatenate([top, bot], axis=-2)


def _pairs_to_state(sp):
    nb = sp.shape[0]
    s0 = sp[:, :, :HEAD_DIM, :HEAD_DIM]
    s1 = sp[:, :, HEAD_DIM:, HEAD_DIM:]
    return jnp.stack([s0, s1], axis=2).reshape(nb, N_HEADS, HEAD_DIM, HEAD_DIM)


def _pad_rows_front(buf, rows):
    nb, r, c = buf.shape
    return jnp.concatenate([jnp.zeros((nb, rows - r, c), buf.dtype), buf], axis=1)


def _prompt_layer(x, p, *, tm, tq, tt_a, tt_b):
    nb, t, d = x.shape
    xf = x.reshape(nb * t, d)
    u, qkvb, zb, kx, qxt, vx, kt, vt, smt, smr = _inproj(
        xf, p["g_pre_mix"], p["wmain"], p["wq"], p["wkv"], p["wsm"], p["addc"], p["alogc"], p["addr"], p["alogr"],
        nb=nb, tm=tm, decode=False)
    ya, nbuf_a = _conv_a(u.reshape(nb, t, -1), jnp.zeros((nb, _CA_PAD, CONV_A_WIDTH), F32), p["conv_a_w"],
                         p["conv_a_b"], p["ln_a_g"], p["ln_a_b"], tt=tt_a)
    yb, nbuf_b, s_new = _delta(qkvb.reshape(nb, t, -1), jnp.zeros((nb, _CB_PAD, 3 * HW), F32), p["conv_b_w"],
                               smr.reshape(nb, t, LANES), smt, zb.reshape(nb, t, HW), p["norm_b_g"],
                               jnp.zeros((nb, N_PAIRS, LANES, LANES), F32), tt=tt_b)
    yc = _fox_prompt(kx.reshape(nb, t, N_HEADS * LANES), qxt, vx, tq=tq)
    y = _outffn(xf, ya.reshape(nb * t, -1), yb.reshape(nb * t, -1), yc, p["w_out"],
                p["w_gate"], p["w_up"], p["w_down"], p["g_post_mix"], p["g_pre_ffn"], p["g_post_ffn"], tm=tm)
    k_out = kt.reshape(nb, N_HEADS, HEAD_DIM, t).transpose(0, 3, 1, 2)
    v_out = vt.reshape(nb, N_HEADS, HEAD_DIM, t).transpose(0, 3, 1, 2)
    logf_out = smt[:, 0:N_HEADS, :].transpose(0, 2, 1)
    states = (k_out, v_out, logf_out, nbuf_a[:, _CA_PAD - (CONV_A_KERNEL - 1):],
              nbuf_b[:, _CB_PAD - (SHORT_CONV - 1):], _pairs_to_state(s_new))
    return y.reshape(nb, t, d), states


def _sample_layer(x, p, layer, buf_a, buf_b, s_delta, cache_kt, cache_vt, cache_lcs, page_table, *, group):
    bd, _, d = x.shape
    xf = x.reshape(bd, d)
    u, qkvb, zb, qx, kt, vt, smt, smr = _inproj(
        xf, p["g_pre_mix"], p["wmain"], p["wq"], p["wkv"], p["wsm"], p["addc"], p["alogc"], p["addr"], p["alogr"],
        nb=1, tm=bd, decode=True)
    st_a = jnp.transpose(buf_a, (1, 0, 2))
    ya = _conv_a_decode(u, st_a, p["conv_a_w"], p["conv_a_b"], p["ln_a_g"], p["ln_a_b"])
    new_buf_a = jnp.concatenate([buf_a[:, 1:], u[:, None, :]], axis=1)
    st_b = jnp.transpose(buf_b, (1, 0, 2))
    yb, s_new = _delta_decode(qkvb, st_b, p["conv_b_w"], smr, zb, p["norm_b_g"], _state_to_pairs(s_delta))
    new_buf_b = jnp.concatenate([buf_b[:, 1:], qkvb[:, None, :]], axis=1)

    k_new = kt[0].T
    v_new = vt[0].T
    lf_new = smt[0, 0:8, :].T
    yc = _fox_decode(page_table, layer, qx[:, None, :], cache_kt, cache_vt, cache_lcs, k_new[:, None, :],
                     v_new[:, None, :], jnp.broadcast_to(lf_new[:, :, None], (bd, 8, LANES)), group=group)
    y = _outffn(xf, ya, yb, yc.reshape(bd, -1), p["w_out"], p["w_gate"], p["w_up"], p["w_down"],
                p["g_post_mix"], p["g_pre_ffn"], p["g_post_ffn"], tm=bd)
    states = (k_new.reshape(bd, 1, N_HEADS, HEAD_DIM), v_new.reshape(bd, 1, N_HEADS, HEAD_DIM),
              lf_new[:, None, 0:N_HEADS], new_buf_a, new_buf_b, _pairs_to_state(s_new))
    return y.reshape(bd, 1, d), states


def _forward(x_prompt, x_sample, cache_k, cache_v, cache_logf, page_table, state_conv_a, state_conv_b,
             state_delta, weights, *, tm, tq, tt_a, tt_b, group):
    depth = cache_k.shape[0]
    n_phys = cache_k.shape[1]
    ckt = jnp.transpose(cache_k, (0, 1, 3, 4, 2)).reshape(depth, n_phys, HW, PAGE_SIZE)
    cvt = jnp.transpose(cache_v, (0, 1, 3, 4, 2)).reshape(depth, n_phys, HW, PAGE_SIZE)
    clf = jnp.transpose(cache_logf, (0, 1, 3, 2))
    clf = jnp.concatenate([clf, jnp.zeros((depth, n_phys, 8 - N_HEADS, PAGE_SIZE), F32)], axis=2)
    n_rows = depth * n_phys * 8
    rows = max(r for r in range(8, min(n_rows, 2048) + 1, 8) if n_rows % r == 0)
    clcs = _page_cumsum(clf.reshape(n_rows, PAGE_SIZE), rows=rows).reshape(depth, n_phys, 8, PAGE_SIZE)
    xp, xs = x_prompt, x_sample
    prompt_states, sample_states = [], []
    for l in range(depth):
        p = _layer_weights(l, *weights)
        xp, sp = _prompt_layer(xp, p, tm=tm, tq=tq, tt_a=tt_a, tt_b=tt_b)
        prompt_states.append(sp)
        xs, ss = _sample_layer(xs, p, l, state_conv_a[l], state_conv_b[l], state_delta[l], ckt, cvt, clcs,
                               page_table, group=group)
        sample_states.append(ss)
    ps = [jnp.stack(t) for t in zip(*prompt_states)]
    ss = [jnp.stack(t) for t in zip(*sample_states)]
    return (xp, xs, *ps, *ss)


def kernel(x_prompt, x_sample, cache_k, cache_v, cache_logf, page_table, state_conv_a, state_conv_b, state_delta,
           w_in, conv_a_w, conv_a_b, ln_a_g, ln_a_b, conv_b_w, a_log, dt_bias, norm_b_g, f_bias, w_out, g_pre_mix,
           g_post_mix, g_pre_ffn, g_post_ffn, w_gate, w_up, w_down):
    weights = (w_in, conv_a_w, conv_a_b, ln_a_g, ln_a_b, conv_b_w, a_log, dt_bias, norm_b_g, f_bias, w_out,
               g_pre_mix, g_post_mix, g_pre_ffn, g_post_ffn, w_gate, w_up, w_down)
    return _forward(x_prompt, x_sample, cache_k, cache_v, cache_logf, page_table, state_conv_a, state_conv_b,
                    state_delta, weights, tm=512, tq=512, tt_a=512, tt_b=256, group=16)
```

```python
import functools

import jax
import jax.numpy as jnp
from jax import lax
from jax.experimental import pallas as pl
from jax.experimental.pallas import tpu as pltpu

F32 = jnp.float32
BF16 = jnp.bfloat16

HEAD_DIM = 64
N_HEADS = 6
N_PAIRS = N_HEADS // 2
HW = N_HEADS * HEAD_DIM
CONV_A_WIDTH = 256
CONV_A_KERNEL = 31
SHORT_CONV = 4
DELTA_CHUNK = 64
PAGE_SIZE = 128
LANES = 128
SUBLANES = 8
RMS_EPS = 1e-6
LN_EPS = 1e-5
VMEM_LIMIT_BYTES = 56 * 1024 * 1024
NEG_INF = float("-inf")
LOG2E = 1.4426950408889634

_NT = (((1,), (1,)), ((), ()))


def _dot(a, b):
    return jnp.dot(a, b, preferred_element_type=F32)


def _dot_nt(a, b):
    return lax.dot_general(a, b, _NT, preferred_element_type=F32)


def _sigmoid(x):
    return 1.0 / (1.0 + jnp.exp(-x))


def _softplus(x):
    return jnp.maximum(x, 0.0) + jnp.log(1.0 + jnp.exp(-jnp.abs(x)))


def _silu(x):
    return x * _sigmoid(x)


def _rms(x, g):
    return x * lax.rsqrt(jnp.mean(x * x, axis=-1, keepdims=True) + RMS_EPS) * g


def _params(sem):
    return pltpu.CompilerParams(dimension_semantics=sem, vmem_limit_bytes=VMEM_LIMIT_BYTES)


def _iota(shape, dim):
    return lax.broadcasted_iota(jnp.int32, shape, dim)


def _split_hi_lo(x):
    hi = x.astype(BF16)
    lo = (x - hi.astype(F32)).astype(BF16)
    return hi, lo


def _split3(x):
    hi = x.astype(BF16)
    r = x - hi.astype(F32)
    mid = r.astype(BF16)
    lo = (r - mid.astype(F32)).astype(BF16)
    return hi, mid, lo


def _group_sum(x2, gmat):
    hi, lo = _split_hi_lo(x2)
    return _dot(hi, gmat) + _dot(lo, gmat)


_MAIN_SPLITS = (0, 256, 512, 1664, 2048, 2432)


def _gate_fns(raw, add, a_log):
    x = raw + add
    logf = -_softplus(-x)
    beta = _sigmoid(x)
    g = -jnp.exp(a_log) * _softplus(x)
    return logf, beta, g


_VX_ROWS = 80
_BIAS_ROWS = 3
_FOX_KEY_BLOCK = 256
_FOX_Q_BLOCK = 128


def _inproj_kernel(x_ref, g_ref, wmain_ref, wq_ref, wkv_ref, wsm_ref, addc_ref, alogc_ref, addr_ref, alogr_ref,
                   *refs, decode, nt):
    hb = _rms(x_ref[...], g_ref[...]).astype(BF16)
    tm = hb.shape[0]
    if decode:
        u_ref, qkvb_ref, zb_ref, q_ref, kt_ref, vt_ref, smt_ref, smr_ref = refs
    else:
        (tril_ref, place_ref, u_ref, qkvb_ref, zb_ref, kx_ref, qxt_ref, vx_ref, kt_ref, vt_ref, smt_ref, smr_ref,
         carry_ref) = refs

    def proj(i):
        return _dot_nt(hb, wmain_ref[_MAIN_SPLITS[i]:_MAIN_SPLITS[i + 1], :])

    u_ref[...] = proj(0) * _sigmoid(proj(1))
    qkvb_ref[...] = proj(2)
    zb_ref[...] = proj(3)
    kv = _dot_nt(wkv_ref[...], hb)

    rawr = _dot_nt(hb, wsm_ref[...])
    logf, beta, g = _gate_fns(rawr, addr_ref[...], alogr_ref[...])
    col = _iota((1, LANES), 1)
    smr = jnp.where(col < 8, logf, jnp.where(col < 16, beta, g))
    smr_ref[...] = smr

    if decode:
        q_ref[...] = _dot_nt(hb, wq_ref[...]) * (HEAD_DIM ** -0.5)
    else:
        @pl.when(lax.rem(pl.program_id(0), nt) == 0)
        def _():
            carry_ref[...] = jnp.zeros_like(carry_ref)

        l_hi, l_mid, l_lo = _split3(smr)
        tril = tril_ref[...]
        cs = _dot(tril, l_hi) + _dot(tril, l_mid) + _dot(tril, l_lo) + carry_ref[0:1, :]
        carry_ref[...] = jnp.broadcast_to(cs[tm - 1:tm, :], carry_ref.shape)
        hi, mid, lo = _split3(jnp.where(col < N_HEADS, cs * (-LOG2E), 0.0))
        src = jnp.concatenate([proj(4).astype(BF16), hi, mid, lo], axis=1)
        kx_ref[...] = _dot(src, place_ref[...]).astype(BF16)
        qt = _dot_nt(wq_ref[...], hb) * (HEAD_DIM ** -0.5 * LOG2E)
        ones_rows = jnp.where(_iota((LANES - HEAD_DIM, tm), 0) < _BIAS_ROWS, 1.0, 0.0).astype(BF16)
        for h in range(N_HEADS):
            qxt_ref[0, h * LANES:h * LANES + HEAD_DIM, :] = qt[h * HEAD_DIM:(h + 1) * HEAD_DIM].astype(BF16)
            qxt_ref[0, h * LANES + HEAD_DIM:(h + 1) * LANES, :] = ones_rows
            vx_ref[0, h * _VX_ROWS:h * _VX_ROWS + HEAD_DIM, :] = kv[HW + h * HEAD_DIM:HW + (h + 1) * HEAD_DIM].astype(BF16)
            vx_ref[0, h * _VX_ROWS + HEAD_DIM:(h + 1) * _VX_ROWS, :] = jnp.ones((_VX_ROWS - HEAD_DIM, tm), BF16)
    kt_ref[0] = kv[:HW]
    vt_ref[0] = kv[HW:]

    rawt = _dot_nt(wsm_ref[0:24, :], hb)
    logf, beta, g = _gate_fns(rawt, addc_ref[...], alogc_ref[...])
    smt_ref[0, 0:8] = logf[0:8]
    smt_ref[0, 8:16] = beta[8:16]
    smt_ref[0, 16:24] = g[16:24]


def _inproj(x, g, wmain, wq, wkv, wsm, addc, alogc, addr, alogr, *, nb, tm, decode):
    m, d = x.shape
    t = m // nb
    nt = t // tm
    row = lambda i: (i, 0)
    const = lambda i: (0, 0)
    tr = lambda i: (i // nt, 0, i % nt)
    outs = [jax.ShapeDtypeStruct((m, CONV_A_WIDTH), F32), jax.ShapeDtypeStruct((m, 3 * HW), F32),
            jax.ShapeDtypeStruct((m, HW), F32)]
    out_specs = [pl.BlockSpec((tm, CONV_A_WIDTH), row), pl.BlockSpec((tm, 3 * HW), row), pl.BlockSpec((tm, HW), row)]
    extra_in, extra_specs, scratch = [], [], []
    if decode:
        outs += [jax.ShapeDtypeStruct((m, HW), F32)]
        out_specs += [pl.BlockSpec((tm, HW), row)]
    else:
        pos = jnp.arange(tm)
        tril = (pos[:, None] >= pos[None, :]).astype(BF16)
        src = jnp.arange(HW)
        dst = (src // HEAD_DIM) * LANES + src % HEAD_DIM
        lanes_x = jnp.arange(N_HEADS * LANES)
        pk = (dst[:, None] == lanes_x[None, :]).astype(BF16)
        col = jnp.arange(LANES)
        pb = jnp.stack([((col[:, None] < N_HEADS)
                         & (col[:, None] * LANES + HEAD_DIM + piece == lanes_x[None, :])).astype(BF16)
                        for piece in range(_BIAS_ROWS)])
        place = jnp.concatenate([pk, pb[0], pb[1], pb[2]], axis=0)
        extra_in = [tril, place]
        extra_specs = [pl.BlockSpec(tril.shape, const), pl.BlockSpec(place.shape, const)]
        scratch = [pltpu.VMEM((8, LANES), F32)]
        outs += [jax.ShapeDtypeStruct((m, N_HEADS * LANES), BF16),
                 jax.ShapeDtypeStruct((nb, N_HEADS * LANES, t), BF16),
                 jax.ShapeDtypeStruct((nb, N_HEADS * _VX_ROWS, t), BF16)]
        out_specs += [pl.BlockSpec((tm, N_HEADS * LANES), row), pl.BlockSpec((1, N_HEADS * LANES, tm), tr),
                      pl.BlockSpec((1, N_HEADS * _VX_ROWS, tm), tr)]
    outs += [jax.ShapeDtypeStruct((nb, HW, t), F32), jax.ShapeDtypeStruct((nb, HW, t), F32),
             jax.ShapeDtypeStruct((nb, 24, t), F32), jax.ShapeDtypeStruct((m, LANES), F32)]
    out_specs += [pl.BlockSpec((1, HW, tm), tr), pl.BlockSpec((1, HW, tm), tr),
                  pl.BlockSpec((1, 24, tm), tr), pl.BlockSpec((tm, LANES), row)]
    in_specs = [
        pl.BlockSpec((tm, d), row), pl.BlockSpec((1, d), const),
        pl.BlockSpec(wmain.shape, const), pl.BlockSpec(wq.shape, const), pl.BlockSpec(wkv.shape, const),
        pl.BlockSpec(wsm.shape, const),
        pl.BlockSpec((24, 1), const), pl.BlockSpec((24, 1), const),
        pl.BlockSpec((1, LANES), const), pl.BlockSpec((1, LANES), const),
    ] + extra_specs
    return pl.pallas_call(
        functools.partial(_inproj_kernel, decode=decode, nt=nt), grid=(m // tm,), in_specs=in_specs,
        out_specs=out_specs, out_shape=outs, scratch_shapes=scratch,
        compiler_params=_params(("arbitrary",)), name="inproj",
    )(x, g, wmain, wq, wkv, wsm, addc, alogc, addr, alogr, *extra_in)


_CA_PAD = 32
_CA_ROWS = 64


def _conv_a_kernel(u_ref, buf_ref, w_ref, b_ref, lg_ref, lb_ref, y_ref, nbuf_ref, ext_ref, *scratch, tt):
    ti = pl.program_id(1)

    @pl.when(ti == 0)
    def _():
        ext_ref[0:_CA_PAD] = buf_ref[0]

    ext_ref[_CA_PAD:_CA_PAD + tt] = u_ref[0]
    off = _CA_PAD - (CONV_A_KERNEL - 1)
    rows = min(_CA_ROWS, tt)
    aligned = tt % SUBLANES == 0
    if aligned:
        (sh_ref,) = scratch
        span = tt + _CA_PAD - SUBLANES
        for r in range(1, SUBLANES):
            sh_ref[r - 1] = ext_ref[r:r + span]
    for c in range(tt // rows):
        acc = jnp.zeros((rows, CONV_A_WIDTH), F32)
        for j in range(CONV_A_KERNEL):
            o = off + j + c * rows
            if aligned and o % SUBLANES:
                tap = sh_ref[o % SUBLANES - 1, o - o % SUBLANES:o - o % SUBLANES + rows, :]
            else:
                tap = ext_ref[o:o + rows, :]
            acc = acc + w_ref[j:j + 1, :] * tap
        ca = acc + b_ref[...]
        mu = jnp.mean(ca, axis=-1, keepdims=True)
        xc = ca - mu
        var = jnp.mean(xc * xc, axis=-1, keepdims=True)
        y_ref[0, c * rows:(c + 1) * rows, :] = _silu(xc * lax.rsqrt(var + LN_EPS) * lg_ref[...] + lb_ref[...])
    tail = ext_ref[tt:tt + _CA_PAD]
    nbuf_ref[0] = tail
    ext_ref[0:_CA_PAD] = tail


def _conv_a(u, buf, w, b, lg, lb, *, tt):
    nb, t, c = u.shape
    const = lambda bi, ti: (0, 0)
    return pl.pallas_call(
        functools.partial(_conv_a_kernel, tt=tt),
        grid=(nb, t // tt),
        in_specs=[pl.BlockSpec((1, tt, c), lambda bi, ti: (bi, ti, 0)),
                  pl.BlockSpec((1, _CA_PAD, c), lambda bi, ti: (bi, 0, 0)),
                  pl.BlockSpec((_CA_PAD, c), const), pl.BlockSpec((1, c), const),
                  pl.BlockSpec((1, c), const), pl.BlockSpec((1, c), const)],
        out_specs=[pl.BlockSpec((1, tt, c), lambda bi, ti: (bi, ti, 0)),
                   pl.BlockSpec((1, _CA_PAD, c), lambda bi, ti: (bi, 0, 0))],
        out_shape=[jax.ShapeDtypeStruct((nb, t, c), F32), jax.ShapeDtypeStruct((nb, _CA_PAD, c), F32)],
        scratch_shapes=[pltpu.VMEM((_CA_PAD + tt, c), F32)]
        + ([pltpu.VMEM((SUBLANES - 1, tt + _CA_PAD - SUBLANES, c), F32)] if tt % SUBLANES == 0 else []),
        compiler_params=_params(("arbitrary", "arbitrary")), name="conv_a",
    )(u, buf, w, b, lg, lb)


_CB_PAD = 8


def _pair_cols(x, p):
    return x[:, p * LANES:(p + 1) * LANES]


def _head_cols(sm, base, p, lane_lo):
    c0 = sm[:, base + 2 * p:base + 2 * p + 1]
    c1 = sm[:, base + 2 * p + 1:base + 2 * p + 2]
    return jnp.where(lane_lo, c0, c1)


def _block_diag(x, lane_lo):
    return jnp.concatenate([jnp.where(lane_lo, x, 0.0), jnp.where(lane_lo, 0.0, x)], axis=0)


def _delta_kernel(x_ref, buf_ref, w_ref, smr_ref, smt_ref, z_ref, gb_ref, s0_ref, gmat_ref, tril_ref, triu_ref,
                  y_ref, nbuf_ref, sout_ref, ext_ref, s_ref, *, tt):
    ti = pl.program_id(1)
    nchunk = tt // DELTA_CHUNK
    c = DELTA_CHUNK

    @pl.when(ti == 0)
    def _():
        ext_ref[0:_CB_PAD] = buf_ref[0]
        s_ref[...] = s0_ref[0]

    ext_ref[_CB_PAD:_CB_PAD + tt] = x_ref[0]
    off = _CB_PAD - (SHORT_CONV - 1)
    acc = w_ref[0:1, :] * ext_ref[off:off + tt, :]
    for j in range(1, SHORT_CONV):
        acc = acc + w_ref[j:j + 1, :] * ext_ref[off + j:off + j + tt, :]
    tail = ext_ref[tt:tt + _CB_PAD]
    nbuf_ref[0] = tail
    ext_ref[0:_CB_PAD] = tail
    cb = _silu(acc)

    gmat = gmat_ref[...]
    q = cb[:, 0:HW]
    k = cb[:, HW:2 * HW]
    v = cb[:, 2 * HW:3 * HW]
    qs = q * lax.rsqrt(_group_sum(q * q, gmat) + 1e-6) * (HEAD_DIM ** -0.5)
    kn = k * lax.rsqrt(_group_sum(k * k, gmat) + 1e-6)

    smr = smr_ref[0]
    smt = smt_ref[0]
    s_hi, s_mid, s_lo = _split3(smr)
    gc_col = _dot(tril_ref[...], s_hi) + _dot(tril_ref[...], s_mid) + _dot(tril_ref[...], s_lo)
    t_hi, t_mid, t_lo = _split3(smt[16:24])
    gc_row = _dot(t_hi, triu_ref[...]) + _dot(t_mid, triu_ref[...]) + _dot(t_lo, triu_ref[...])
    if tt < LANES:
        gc_row = jnp.concatenate([gc_row, jnp.zeros((8, LANES - tt), F32)], axis=1)

    lane = _iota((1, LANES), 1)
    lane_lo = lane < HEAD_DIM
    ri = _iota((c, LANES), 0)
    ci_ = _iota((c, LANES), 1)
    cj = jnp.where(ci_ < HEAD_DIM, ci_, ci_ - HEAD_DIM)
    causal = ri >= cj
    strict = ri > cj
    eye2 = jnp.where(ri == cj, 1.0, 0.0)
    r128 = _iota((LANES, LANES), 0)
    c128 = _iota((LANES, LANES), 1)
    bdmask = (r128 < HEAD_DIM) == (c128 < HEAD_DIM)

    combos = [(ci, p) for ci in range(nchunk) for p in range(N_PAIRS)]
    pre = []
    for ci, p in combos:
        r0, r1 = ci * c, (ci + 1) * c
        qs_p = _pair_cols(qs, p)[r0:r1]
        kn_p = _pair_cols(kn, p)[r0:r1]
        v_p = _pair_cols(v, p)[r0:r1]
        beta = _head_cols(smr[r0:r1], 8, p, lane_lo)
        gi = _head_cols(gc_col[r0:r1], 16, p, lane_lo)
        blk = gc_row[:, (ci // 2) * LANES:(ci // 2 + 1) * LANES]
        rot = pltpu.roll(blk, HEAD_DIM, 1)
        if ci % 2 == 0:
            gj = jnp.where(lane_lo, blk[2 * p:2 * p + 1], rot[2 * p + 1:2 * p + 2])
        else:
            gj = jnp.where(lane_lo, rot[2 * p:2 * p + 1], blk[2 * p + 1:2 * p + 2])
        glast = gi[c - 1:c, :]
        eg = jnp.exp(gi)
        decay = jnp.exp(jnp.where(causal, gi - gj, NEG_INF))
        kb = kn_p * beta
        pre.append(dict(qs=qs_p, kn=kn_p, vb=v_p * beta, kb=kb, kbe=kb * eg, qe=qs_p * eg, eg_last=jnp.exp(glast),
                        kdec=kn_p * jnp.exp(glast - gi), decay=decay, kd=_block_diag(kn_p, lane_lo)))

    grams = [_dot_nt(jnp.concatenate([d["kb"], d["qs"]], axis=0), d["kd"]) for d in pre]
    lmats = [jnp.where(strict, g[0:c] * d["decay"], 0.0) for g, d in zip(grams, pre)]
    qks = [jnp.where(causal, g[c:2 * c] * d["decay"], 0.0) for g, d in zip(grams, pre)]

    def split_bd(m):
        hi, lo = _split_hi_lo(m)
        return hi, lo, _block_diag(hi, lane_lo), _block_diag(lo, lane_lo)

    def dot3(a_hi, a_lo, b_hi, b_lo):
        return _dot(a_hi, b_hi) + _dot(a_lo, b_hi) + _dot(a_hi, b_lo)

    xs = [eye2 - l for l in lmats]
    pws = [dot3(*split_bd(l)) for l in lmats]
    for level in range(4):
        psp = [split_bd(pw) for pw in pws]
        xsp = [_split_hi_lo(x) for x in xs]
        prods = [dot3(jnp.concatenate([ph, xh], axis=0), jnp.concatenate([plo, xl], axis=0), bh, bl)
                 for (ph, plo, bh, bl), (xh, xl) in zip(psp, xsp)]
        xs = [x + pr[c:2 * c] for x, pr in zip(xs, prods)]
        pws = [pr[0:c] for pr in prods]
    xs = [x + dot3(*_split_hi_lo(x), *split_bd(pw)[2:]) for pw, x in zip(pws, xs)]

    rhs_uw = [jnp.concatenate([_block_diag(d["vb"], lane_lo), _block_diag(d["kbe"], lane_lo)], axis=1) for d in pre]
    uws = [_dot(x, r) for x, r in zip(xs, rhs_uw)]
    kts = [d["kdec"].T for d in pre]
    kuws = [_dot(kt, uw) for kt, uw in zip(kts, uws)]
    kus = [jnp.where(bdmask, m[:, 0:LANES], 0.0) for m in kuws]
    kws = [jnp.where(bdmask, m[:, LANES:2 * LANES], 0.0) for m in kuws]
    rhs_q = [jnp.concatenate([_block_diag(uw[:, 0:LANES], lane_lo), _block_diag(uw[:, LANES:2 * LANES], lane_lo)],
                             axis=1) for uw in uws]
    qkuw = [_dot(qk, r) for qk, r in zip(qks, rhs_q)]
    o2s = [m[:, 0:LANES] for m in qkuw]
    q2s = [d["qe"] - m[:, LANES:2 * LANES] for d, m in zip(pre, qkuw)]

    states = [s_ref[p] for p in range(N_PAIRS)]
    o_chunks = []
    for ci in range(nchunk):
        o_pairs = []
        for p in range(N_PAIRS):
            i = ci * N_PAIRS + p
            s_bd = states[p]
            prod = _dot(jnp.concatenate([q2s[i], kws[i]], axis=0), s_bd)
            o_pairs.append(prod[0:c] + o2s[i])
            states[p] = s_bd * pre[i]["eg_last"] - prod[c:c + LANES] + kus[i]
        o_chunks.append(jnp.concatenate(o_pairs, axis=1))
    for p in range(N_PAIRS):
        s_ref[p] = states[p]
    o = jnp.concatenate(o_chunks, axis=0) if nchunk > 1 else o_chunks[0]
    ms = _group_sum(o * o, gmat) * (1.0 / HEAD_DIM)
    y_ref[0] = o * lax.rsqrt(ms + RMS_EPS) * gb_ref[...] * _silu(z_ref[0])
    sout_ref[0] = s_ref[...]


def _delta(x, buf, w, smr, smt, z, gb, s0, *, tt):
    nb, t, cw = x.shape
    const = lambda bi, ti: (0, 0)
    head = jnp.arange(HW) // HEAD_DIM
    gmat = (head[:, None] == head[None, :]).astype(BF16)
    chunk = jnp.arange(tt) // DELTA_CHUNK
    same = chunk[:, None] == chunk[None, :]
    pos = jnp.arange(tt)
    tril = (same & (pos[:, None] >= pos[None, :])).astype(BF16)
    triu = tril.T
    return pl.pallas_call(
        functools.partial(_delta_kernel, tt=tt),
        grid=(nb, t // tt),
        in_specs=[pl.BlockSpec((1, tt, cw), lambda bi, ti: (bi, ti, 0)),
                  pl.BlockSpec((1, _CB_PAD, cw), lambda bi, ti: (bi, 0, 0)),
                  pl.BlockSpec((SHORT_CONV, cw), const),
                  pl.BlockSpec((1, tt, LANES), lambda bi, ti: (bi, ti, 0)),
                  pl.BlockSpec((1, 24, tt), lambda bi, ti: (bi, 0, ti)),
                  pl.BlockSpec((1, tt, HW), lambda bi, ti: (bi, ti, 0)),
                  pl.BlockSpec((1, HW), const),
                  pl.BlockSpec((1, N_PAIRS, LANES, LANES), lambda bi, ti: (bi, 0, 0, 0)),
                  pl.BlockSpec((HW, HW), const), pl.BlockSpec((tt, tt), const), pl.BlockSpec((tt, tt), const)],
        out_specs=[pl.BlockSpec((1, tt, HW), lambda bi, ti: (bi, ti, 0)),
                   pl.BlockSpec((1, _CB_PAD, cw), lambda bi, ti: (bi, 0, 0)),
                   pl.BlockSpec((1, N_PAIRS, LANES, LANES), lambda bi, ti: (bi, 0, 0, 0))],
        out_shape=[jax.ShapeDtypeStruct((nb, t, HW), F32), jax.ShapeDtypeStruct((nb, _CB_PAD, cw), F32),
                   jax.ShapeDtypeStruct((nb, N_PAIRS, LANES, LANES), F32)],
        scratch_shapes=[pltpu.VMEM((_CB_PAD + tt, cw), F32), pltpu.VMEM((N_PAIRS, LANES, LANES), F32)],
        compiler_params=_params(("arbitrary", "arbitrary")), name="delta",
    )(x, buf, w, smr, smt, z, gb, s0, gmat, tril, triu)


_DEC_ROWS = 8


def _delta_decode_kernel(x_ref, buf_ref, w_ref, smr_ref, z_ref, gb_ref, s0_ref, gmat_ref, y_ref, sout_ref):
    xc = w_ref[SHORT_CONV - 1:SHORT_CONV, :] * x_ref[...]
    for j in range(SHORT_CONV - 1):
        xc = xc + w_ref[j:j + 1, :] * buf_ref[j]
    cb = _silu(xc)
    gmat = gmat_ref[...]
    q = cb[:, 0:HW]
    k = cb[:, HW:2 * HW]
    v = cb[:, 2 * HW:3 * HW]
    qs = q * lax.rsqrt(_group_sum(q * q, gmat) + 1e-6) * (HEAD_DIM ** -0.5)
    kn = k * lax.rsqrt(_group_sum(k * k, gmat) + 1e-6)
    qk = _group_sum(qs * kn, gmat)
    smr = smr_ref[...]
    lane_lo = _iota((1, LANES), 1) < HEAD_DIM
    row = _iota((_DEC_ROWS, LANES), 0)
    r128 = _iota((LANES, LANES), 0)
    c128 = _iota((LANES, LANES), 1)
    bdmask = (r128 < HEAD_DIM) == (c128 < HEAD_DIM)
    o_pairs = []
    for p in range(N_PAIRS):
        qs_p, kn_p, v_p, qk_p = (_pair_cols(t, p) for t in (qs, kn, v, qk))
        beta = _head_cols(smr, 8, p, lane_lo)
        a = jnp.exp(_head_cols(smr, 16, p, lane_lo))
        kn_t = kn_p.T
        o_p = jnp.zeros((_DEC_ROWS, LANES), F32)
        for r in range(_DEC_ROWS):
            s_bd = s0_ref[r, p]
            kq = jnp.where(row == 0, kn_p[r:r + 1], jnp.where(row == 1, qs_p[r:r + 1], 0.0))
            prod = _dot(kq, s_bd)
            a_r = a[r:r + 1]
            v_new = beta[r:r + 1] * (v_p[r:r + 1] - a_r * prod[0:1])
            o_p = jnp.where(row == r, a_r * prod[1:2] + qk_p[r:r + 1] * v_new, o_p)
            sout_ref[r, p] = jnp.where(bdmask, s_bd * a_r + kn_t[:, r:r + 1] * v_new, 0.0)
        o_pairs.append(o_p)
    o = jnp.concatenate(o_pairs, axis=1)
    ms = _group_sum(o * o, gmat) * (1.0 / HEAD_DIM)
    y_ref[...] = o * lax.rsqrt(ms + RMS_EPS) * gb_ref[...] * _silu(z_ref[...])


def _delta_decode(x, buf, w, smr, z, gb, s0):
    bd, cw = x.shape
    head = jnp.arange(HW) // HEAD_DIM
    gmat = (head[:, None] == head[None, :]).astype(BF16)
    rows = lambda i: (i, 0)
    const = lambda i: (0, 0)
    return pl.pallas_call(
        _delta_decode_kernel, grid=(bd // _DEC_ROWS,),
        in_specs=[pl.BlockSpec((_DEC_ROWS, cw), rows),
                  pl.BlockSpec((SHORT_CONV - 1, _DEC_ROWS, cw), lambda i: (0, i, 0)),
                  pl.BlockSpec((SHORT_CONV, cw), const), pl.BlockSpec((_DEC_ROWS, LANES), rows),
                  pl.BlockSpec((_DEC_ROWS, HW), rows), pl.BlockSpec((1, HW), const),
                  pl.BlockSpec((_DEC_ROWS, N_PAIRS, LANES, LANES), lambda i: (i, 0, 0, 0)),
                  pl.BlockSpec((HW, HW), const)],
        out_specs=[pl.BlockSpec((_DEC_ROWS, HW), rows),
                   pl.BlockSpec((_DEC_ROWS, N_PAIRS, LANES, LANES), lambda i: (i, 0, 0, 0))],
        out_shape=[jax.ShapeDtypeStruct((bd, HW), F32), jax.ShapeDtypeStruct((bd, N_PAIRS, LANES, LANES), F32)],
        compiler_params=_params(("arbitrary",)), name="delta_decode",
    )(x, buf, w, smr, z, gb, s0, gmat)


def _conv_a_decode_kernel(u_ref, st_ref, w_ref, b_ref, lg_ref, lb_ref, y_ref):
    acc = w_ref[CONV_A_KERNEL - 1:CONV_A_KERNEL, :] * u_ref[...]
    for j in range(CONV_A_KERNEL - 1):
        acc = acc + w_ref[j:j + 1, :] * st_ref[j]
    ca = acc + b_ref[...]
    mu = jnp.mean(ca, axis=-1, keepdims=True)
    xc = ca - mu
    var = jnp.mean(xc * xc, axis=-1, keepdims=True)
    y_ref[...] = _silu(xc * lax.rsqrt(var + LN_EPS) * lg_ref[...] + lb_ref[...])


def _conv_a_decode(u, st, w, b, lg, lb):
    bd, c = u.shape
    whole = lambda a: pl.BlockSpec(a.shape, lambda i: (0,) * a.ndim)
    return pl.pallas_call(
        _conv_a_decode_kernel, grid=(1,),
        in_specs=[whole(u), whole(st), whole(w), whole(b), whole(lg), whole(lb)],
        out_specs=whole(u), out_shape=jax.ShapeDtypeStruct((bd, c), F32),
        compiler_params=_params(("arbitrary",)), name="conv_a_decode",
    )(u, st, w, b, lg, lb)


def _fox_kernel(qi_ref, ki_ref, kx_ref, qxt_ref, vx_ref, o_ref, m_ref, acc_ref, *, tq, tk):
    step = pl.program_id(1)
    qi = qi_ref[step]
    ki = ki_ref[step]

    @pl.when(ki == 0)
    def _():
        m_ref[...] = jnp.full_like(m_ref, NEG_INF)
        acc_ref[...] = jnp.zeros_like(acc_ref)

    def update(masked):
        if masked:
            keep = _iota((tk, tq), 0) <= _iota((tk, tq), 1)
        combos = [(h, qc) for h in range(N_HEADS) for qc in range(tq // _FOX_Q_BLOCK)]
        half = min(_FOX_KEY_BLOCK, tk)
        ms = [m_ref[h:h + 1, qc * _FOX_Q_BLOCK:(qc + 1) * _FOX_Q_BLOCK] for h, qc in combos]
        accs = [acc_ref[h, :, qc * _FOX_Q_BLOCK:(qc + 1) * _FOX_Q_BLOCK] for h, qc in combos]
        for kh in range(tk // half):
            k0 = kh * half
            live = [i for i, (h, qc) in enumerate(combos) if not masked or (qc + 1) * _FOX_Q_BLOCK > k0]
            sts = [_dot(kx_ref[0, k0:k0 + half, combos[i][0] * LANES:(combos[i][0] + 1) * LANES],
                        qxt_ref[0, combos[i][0] * LANES:(combos[i][0] + 1) * LANES,
                                combos[i][1] * _FOX_Q_BLOCK:(combos[i][1] + 1) * _FOX_Q_BLOCK]) for i in live]
            if masked:
                sts = [jnp.where(keep[k0:k0 + half, combos[i][1] * _FOX_Q_BLOCK:(combos[i][1] + 1) * _FOX_Q_BLOCK],
                                 st, NEG_INF) for st, i in zip(sts, live)]
            m_news = [jnp.maximum(ms[i], jnp.max(st, axis=0, keepdims=True)) for i, st in zip(live, sts)]
            pts = [jnp.exp2(st - mn).astype(BF16) for st, mn in zip(sts, m_news)]
            for i, mn, pt in zip(live, m_news, pts):
                h = combos[i][0]
                accs[i] = (jnp.exp2(ms[i] - mn) * accs[i]
                           + _dot(vx_ref[0, h * _VX_ROWS:(h + 1) * _VX_ROWS, k0:k0 + half], pt))
                ms[i] = mn
        for (h, qc), mn, acc in zip(combos, ms, accs):
            acc_ref[h, :, qc * _FOX_Q_BLOCK:(qc + 1) * _FOX_Q_BLOCK] = acc
            m_ref[h:h + 1, qc * _FOX_Q_BLOCK:(qc + 1) * _FOX_Q_BLOCK] = mn

    @pl.when(ki < qi)
    def _():
        update(False)

    @pl.when(ki == qi)
    def _():
        update(True)
        for h in range(N_HEADS):
            acc = acc_ref[h]
            o_ref[0, h * HEAD_DIM:(h + 1) * HEAD_DIM, :] = acc[0:HEAD_DIM] / acc[HEAD_DIM:HEAD_DIM + 1]


def _fox_prompt(kx, qxt, vx, *, tq):
    nb, t, _ = kx.shape
    nq = t // tq
    qi_tab = jnp.asarray([qi for qi in range(nq) for _ in range(qi + 1)], jnp.int32)
    ki_tab = jnp.asarray([ki for qi in range(nq) for ki in range(qi + 1)], jnp.int32)
    grid_spec = pltpu.PrefetchScalarGridSpec(
        num_scalar_prefetch=2, grid=(nb, int(qi_tab.shape[0])),
        in_specs=[pl.BlockSpec((1, tq, N_HEADS * LANES), lambda b, s, qt, kt: (b, kt[s], 0)),
                  pl.BlockSpec((1, N_HEADS * LANES, tq), lambda b, s, qt, kt: (b, 0, qt[s])),
                  pl.BlockSpec((1, N_HEADS * _VX_ROWS, tq), lambda b, s, qt, kt: (b, 0, kt[s]))],
        out_specs=pl.BlockSpec((1, HW, tq), lambda b, s, qt, kt: (b, 0, qt[s])),
        scratch_shapes=[pltpu.VMEM((8, tq), F32), pltpu.VMEM((N_HEADS, _VX_ROWS, tq), F32)],
    )
    return pl.pallas_call(
        functools.partial(_fox_kernel, tq=tq, tk=tq), grid_spec=grid_spec,
        out_shape=jax.ShapeDtypeStruct((nb, HW, t), F32),
        compiler_params=_params(("arbitrary", "arbitrary")), name="fox_prompt",
    )(qi_tab, ki_tab, kx, qxt, vx)


def _page_cumsum_kernel(x_ref, triu_ref, o_ref):
    hi, mid, lo = _split3(x_ref[...])
    triu = triu_ref[...]
    o_ref[...] = _dot(hi, triu) + _dot(mid, triu) + _dot(lo, triu)


def _page_cumsum(lf, *, rows):
    r = lf.shape[0]
    pos = jnp.arange(PAGE_SIZE)
    triu = (pos[:, None] <= pos[None, :]).astype(BF16)
    return pl.pallas_call(
        _page_cumsum_kernel, grid=(r // rows,),
        in_specs=[pl.BlockSpec((rows, PAGE_SIZE), lambda i: (i, 0)),
                  pl.BlockSpec((PAGE_SIZE, PAGE_SIZE), lambda i: (0, 0))],
        out_specs=pl.BlockSpec((rows, PAGE_SIZE), lambda i: (i, 0)),
        out_shape=jax.ShapeDtypeStruct((r, PAGE_SIZE), F32),
        compiler_params=_params(("arbitrary",)), name="page_cumsum",
    )(lf, triu)


def _fox_decode_kernel(pt_ref, qx_ref, kn_ref, vn_ref, lfn_ref, kt_hbm, vt_hbm, lcs_hbm, o_ref,
                       kbuf, vbuf, lbuf, sem, qbd_ref, m_ref, l_ref, acc_ref, carry_ref, *, layer, n_steps, group):
    b = pl.program_id(0)
    j = pl.program_id(1)
    step = b * n_steps + j
    slot = lax.rem(step, 2)
    total = pl.num_programs(0) * n_steps

    def page_copies(bb, jj, to_slot, lookup):
        out = []
        for g in range(group):
            page = pt_ref[bb, jj * group + g] if lookup else 0
            out.append(pltpu.make_async_copy(kt_hbm.at[layer, page], kbuf.at[to_slot, g], sem.at[0, to_slot]))
            out.append(pltpu.make_async_copy(vt_hbm.at[layer, page], vbuf.at[to_slot, g], sem.at[1, to_slot]))
            out.append(pltpu.make_async_copy(lcs_hbm.at[layer, page], lbuf.at[to_slot, g], sem.at[2, to_slot]))
        return out

    @pl.when(step == 0)
    def _():
        for cp in page_copies(0, 0, 0, True):
            cp.start()

    @pl.when(step + 1 < total)
    def _():
        last = j == n_steps - 1
        for cp in page_copies(jnp.where(last, b + 1, b), jnp.where(last, 0, j + 1), 1 - slot, True):
            cp.start()

    for cp in page_copies(0, 0, slot, False):
        cp.wait()
    kt_refs = [kbuf.at[slot, g] for g in range(group)]
    vt_refs = [vbuf.at[slot, g] for g in range(group)]
    lcs_refs = [lbuf.at[slot, g] for g in range(group)]

    @pl.when(j == 0)
    def _():
        row = _iota((8, HW), 0)
        col = _iota((8, HW), 1)
        own = (col >= row * HEAD_DIM) & (col < (row + 1) * HEAD_DIM)
        qbd_ref[...] = jnp.where(own, qx_ref[0], 0.0)
        m_ref[...] = jnp.full_like(m_ref, NEG_INF)
        l_ref[...] = jnp.zeros_like(l_ref)
        acc_ref[...] = jnp.zeros_like(acc_ref)
        carry_ref[...] = jnp.zeros_like(carry_ref)

    qbd = qbd_ref[...]
    lcs = [lcs_refs[g][...] for g in range(group)]
    totals = [jnp.broadcast_to(x[:, PAGE_SIZE - 1:PAGE_SIZE], x.shape) for x in lcs]
    qk = [_dot(qbd, kt_refs[g][...]) for g in range(group)]
    c_run = carry_ref[...]
    scores = []
    for g in range(group):
        scores.append(qk[g] - (lcs[g] + c_run))
        c_run = c_run + totals[g]
    c_end = c_run[:, 0:1]
    s = jnp.concatenate(scores, axis=1)
    m_prev = m_ref[:, 0:1]
    m_new = jnp.maximum(m_prev, jnp.max(s, axis=-1, keepdims=True))
    alpha = jnp.exp(m_prev - m_new)
    pr = jnp.exp(s - m_new)
    l_new = alpha * l_ref[:, 0:1] + jnp.sum(pr, axis=-1, keepdims=True)
    pv = _dot_nt(pr[:, 0:PAGE_SIZE], vt_refs[0][...])
    for g in range(1, group):
        pv = pv + _dot_nt(pr[:, g * PAGE_SIZE:(g + 1) * PAGE_SIZE], vt_refs[g][...])
    acc_new = alpha * acc_ref[...] + pv
    m_ref[...] = jnp.broadcast_to(m_new, m_ref.shape)
    l_ref[...] = jnp.broadcast_to(l_new, l_ref.shape)
    acc_ref[...] = acc_new
    carry_ref[...] = c_run

    @pl.when(j == n_steps - 1)
    def _():
        s_n = jnp.sum(qbd_ref[...] * kn_ref[0], axis=-1, keepdims=True) - (c_end + lfn_ref[0][:, 0:1])
        m_f = jnp.maximum(m_new, s_n)
        a_f = jnp.exp(m_new - m_f)
        p_n = jnp.exp(s_n - m_f)
        l_f = a_f * l_new + p_n
        o8 = (a_f * acc_new + p_n * vn_ref[0]) / l_f
        row = _iota((8, HW), 0)
        col = _iota((8, HW), 1)
        own = (col >= row * HEAD_DIM) & (col < (row + 1) * HEAD_DIM)
        o_ref[0] = jnp.sum(jnp.where(own, o8, 0.0), axis=0, keepdims=True)


def _fox_decode(page_table, layer, qx, cache_kt, cache_vt, cache_lcs, k_new, v_new, lf_new, *, group):
    bd = qx.shape[0]
    n_pages = page_table.shape[1]
    n_steps = n_pages // group
    per_b = lambda b, j, pt: (b, 0, 0)
    hbm = pl.BlockSpec(memory_space=pl.ANY)
    grid_spec = pltpu.PrefetchScalarGridSpec(
        num_scalar_prefetch=1, grid=(bd, n_steps),
        in_specs=[pl.BlockSpec((1, 1, HW), per_b), pl.BlockSpec((1, 1, HW), per_b), pl.BlockSpec((1, 1, HW), per_b),
                  pl.BlockSpec((1, 8, LANES), per_b), hbm, hbm, hbm],
        out_specs=pl.BlockSpec((1, 1, HW), per_b),
        scratch_shapes=[pltpu.VMEM((2, group, HW, PAGE_SIZE), F32), pltpu.VMEM((2, group, HW, PAGE_SIZE), F32),
                        pltpu.VMEM((2, group, 8, PAGE_SIZE), F32), pltpu.SemaphoreType.DMA((3, 2)),
                        pltpu.VMEM((8, HW), F32), pltpu.VMEM((8, LANES), F32), pltpu.VMEM((8, LANES), F32),
                        pltpu.VMEM((8, HW), F32), pltpu.VMEM((8, LANES), F32)],
    )
    return pl.pallas_call(
        functools.partial(_fox_decode_kernel, layer=layer, n_steps=n_steps, group=group), grid_spec=grid_spec,
        out_shape=jax.ShapeDtypeStruct((bd, 1, HW), F32),
        compiler_params=_params(("arbitrary", "arbitrary")), name="fox_decode",
    )(page_table, qx, k_new, v_new, lf_new, cache_kt, cache_vt, cache_lcs)


_FF_CHUNK = 704


def _outffn_kernel(x_ref, ya_ref, yb_ref, yc_ref, wo_ref, wg_ref, wu_ref, wd_ref, gpm_ref, gpf_ref, gqf_ref, o_ref,
                   *, yc_transposed):
    a0 = CONV_A_WIDTH
    a1 = a0 + HW
    yc = yc_ref[0].T if yc_transposed else yc_ref[...]
    mix = (_dot(ya_ref[...].astype(BF16), wo_ref[0:a0, :]) + _dot(yb_ref[...].astype(BF16), wo_ref[a0:a1, :])
           + _dot(yc.astype(BF16), wo_ref[a1:a1 + HW, :]))
    x1 = x_ref[...] + _rms(mix, gpm_ref[...])
    hb = _rms(x1, gpf_ref[...]).astype(BF16)
    d_ff = wg_ref.shape[1]
    f = jnp.zeros_like(x1)
    for c0 in range(0, d_ff, _FF_CHUNK):
        gate = _dot(hb, wg_ref[:, c0:c0 + _FF_CHUNK])
        up = _dot(hb, wu_ref[:, c0:c0 + _FF_CHUNK])
        f = f + _dot((_silu(gate) * up).astype(BF16), wd_ref[c0:c0 + _FF_CHUNK, :])
    o_ref[...] = x1 + _rms(f, gqf_ref[...])


def _outffn(x, ya, yb, yc, wo, wg, wu, wd, gpm, gpf, gqf, *, tm):
    m, d = x.shape
    row = lambda i: (i, 0)
    const = lambda i: (0, 0)
    whole = lambda a: pl.BlockSpec(a.shape, const, pipeline_mode=pl.Buffered(1))
    yc_transposed = yc.ndim == 3
    if yc_transposed:
        nt = yc.shape[2] // tm
        yc_spec = pl.BlockSpec((1, HW, tm), lambda i: (i // nt, 0, i % nt))
    else:
        yc_spec = pl.BlockSpec((tm, HW), row)
    return pl.pallas_call(
        functools.partial(_outffn_kernel, yc_transposed=yc_transposed), grid=(m // tm,),
        in_specs=[pl.BlockSpec((tm, d), row), pl.BlockSpec((tm, CONV_A_WIDTH), row),
                  pl.BlockSpec((tm, HW), row), yc_spec,
                  whole(wo), whole(wg), whole(wu), whole(wd),
                  pl.BlockSpec((1, d), const), pl.BlockSpec((1, d), const), pl.BlockSpec((1, d), const)],
        out_specs=pl.BlockSpec((tm, d), row),
        out_shape=jax.ShapeDtypeStruct((m, d), F32),
        compiler_params=_params(("arbitrary",)), name="outffn",
    )(x, ya, yb, yc, wo, wg, wu, wd, gpm, gpf, gqf)


def _layer_weights(l, w_in, conv_a_w, conv_a_b, ln_a_g, ln_a_b, conv_b_w, a_log, dt_bias, norm_b_g, f_bias,
                   w_out, g_pre_mix, g_post_mix, g_pre_ffn, g_post_ffn, w_gate, w_up, w_down):
    wt = w_in[l].T
    wmain = jnp.concatenate([wt[0:2048], wt[2444:2828]], axis=0).astype(BF16)
    wq = wt[2060:2444].astype(BF16)
    wkv = wt[2444:3212].astype(BF16)
    z2 = jnp.zeros((2, wt.shape[1]), F32)
    wsm = jnp.concatenate([wt[3212:3218], z2, wt[2048:2054], z2, wt[2054:2060], z2,
                           jnp.zeros((LANES - 24, wt.shape[1]), F32)], axis=0).astype(BF16)
    z2v = jnp.zeros((2,), F32)
    z8v = jnp.zeros((8,), F32)
    add24 = jnp.concatenate([f_bias[l], z2v, z8v, dt_bias[l], z2v])
    alog24 = jnp.concatenate([z8v, z8v, a_log[l], z2v])
    pad_row = lambda v: jnp.concatenate([v, jnp.zeros((LANES - 24,), F32)])[None, :]
    return dict(
        g_pre_mix=g_pre_mix[l][None, :], wmain=wmain, wq=wq, wkv=wkv, wsm=wsm,
        addc=add24[:, None], alogc=alog24[:, None], addr=pad_row(add24), alogr=pad_row(alog24),
        conv_a_w=jnp.concatenate([conv_a_w[l], jnp.zeros((1, CONV_A_WIDTH), F32)], axis=0),
        conv_a_b=conv_a_b[l][None, :], ln_a_g=ln_a_g[l][None, :], ln_a_b=ln_a_b[l][None, :],
        conv_b_w=conv_b_w[l], norm_b_g=jnp.tile(norm_b_g[l], N_HEADS)[None, :],
        w_out=w_out[l].astype(BF16), w_gate=w_gate[l].astype(BF16), w_up=w_up[l].astype(BF16),
        w_down=w_down[l].astype(BF16),
        g_post_mix=g_post_mix[l][None, :], g_pre_ffn=g_pre_ffn[l][None, :], g_post_ffn=g_post_ffn[l][None, :],
    )


def _state_to_pairs(s):
    nb = s.shape[0]
    s = s.reshape(nb, N_PAIRS, 2, HEAD_DIM, HEAD_DIM)
    z = jnp.zeros_like(s[:, :, 0])
    top = jnp.concatenate([s[:, :, 0], z], axis=-1)
    bot = jnp.concatenate([z, s[:, :, 1]], axis=-1)
    return jnp.concatenate([top, bot], axis=-2)


def _pairs_to_state(sp):
    nb = sp.shape[0]
    s0 = sp[:, :, :HEAD_DIM, :HEAD_DIM]
    s1 = sp[:, :, HEAD_DIM:, HEAD_DIM:]
    return jnp.stack([s0, s1], axis=2).reshape(nb, N_HEADS, HEAD_DIM, HEAD_DIM)


def _pad_rows_front(buf, rows):
    nb, r, c = buf.shape
    return jnp.concatenate([jnp.zeros((nb, rows - r, c), buf.dtype), buf], axis=1)


def _prompt_layer(x, p, *, tm, tq, tt_a, tt_b):
    nb, t, d = x.shape
    xf = x.reshape(nb * t, d)
    u, qkvb, zb, kx, qxt, vx, kt, vt, smt, smr = _inproj(
        xf, p["g_pre_mix"], p["wmain"], p["wq"], p["wkv"], p["wsm"], p["addc"], p["alogc"], p["addr"], p["alogr"],
        nb=nb, tm=tm, decode=False)
    ya, nbuf_a = _conv_a(u.reshape(nb, t, -1), jnp.zeros((nb, _CA_PAD, CONV_A_WIDTH), F32), p["conv_a_w"],
                         p["conv_a_b"], p["ln_a_g"], p["ln_a_b"], tt=tt_a)
    yb, nbuf_b, s_new = _delta(qkvb.reshape(nb, t, -1), jnp.zeros((nb, _CB_PAD, 3 * HW), F32), p["conv_b_w"],
                               smr.reshape(nb, t, LANES), smt, zb.reshape(nb, t, HW), p["norm_b_g"],
                               jnp.zeros((nb, N_PAIRS, LANES, LANES), F32), tt=tt_b)
    yc = _fox_prompt(kx.reshape(nb, t, N_HEADS * LANES), qxt, vx, tq=tq)
    y = _outffn(xf, ya.reshape(nb * t, -1), yb.reshape(nb * t, -1), yc, p["w_out"],
                p["w_gate"], p["w_up"], p["w_down"], p["g_post_mix"], p["g_pre_ffn"], p["g_post_ffn"], tm=tm)
    k_out = kt.reshape(nb, N_HEADS, HEAD_DIM, t).transpose(0, 3, 1, 2)
    v_out = vt.reshape(nb, N_HEADS, HEAD_DIM, t).transpose(0, 3, 1, 2)
    logf_out = smt[:, 0:N_HEADS, :].transpose(0, 2, 1)
    states = (k_out, v_out, logf_out, nbuf_a[:, _CA_PAD - (CONV_A_KERNEL - 1):],
              nbuf_b[:, _CB_PAD - (SHORT_CONV - 1):], _pairs_to_state(s_new))
    return y.reshape(nb, t, d), states


def _sample_layer(x, p, layer, buf_a, buf_b, s_delta, cache_kt, cache_vt, cache_lcs, page_table, *, group):
    bd, _, d = x.shape
    xf = x.reshape(bd, d)
    u, qkvb, zb, qx, kt, vt, smt, smr = _inproj(
        xf, p["g_pre_mix"], p["wmain"], p["wq"], p["wkv"], p["wsm"], p["addc"], p["alogc"], p["addr"], p["alogr"],
        nb=1, tm=bd, decode=True)
    st_a = jnp.transpose(buf_a, (1, 0, 2))
    ya = _conv_a_decode(u, st_a, p["conv_a_w"], p["conv_a_b"], p["ln_a_g"], p["ln_a_b"])
    new_buf_a = jnp.concatenate([buf_a[:, 1:], u[:, None, :]], axis=1)
    st_b = jnp.transpose(buf_b, (1, 0, 2))
    yb, s_new = _delta_decode(qkvb, st_b, p["conv_b_w"], smr, zb, p["norm_b_g"], _state_to_pairs(s_delta))
    new_buf_b = jnp.concatenate([buf_b[:, 1:], qkvb[:, None, :]], axis=1)

    k_new = kt[0].T
    v_new = vt[0].T
    lf_new = smt[0, 0:8, :].T
    yc = _fox_decode(page_table, layer, qx[:, None, :], cache_kt, cache_vt, cache_lcs, k_new[:, None, :],
                     v_new[:, None, :], jnp.broadcast_to(lf_new[:, :, None], (bd, 8, LANES)), group=group)
    y = _outffn(xf, ya, yb, yc.reshape(bd, -1), p["w_out"], p["w_gate"], p["w_up"], p["w_down"],
                p["g_post_mix"], p["g_pre_ffn"], p["g_post_ffn"], tm=bd)
    states = (k_new.reshape(bd, 1, N_HEADS, HEAD_DIM), v_new.reshape(bd, 1, N_HEADS, HEAD_DIM),
              lf_new[:, None, 0:N_HEADS], new_buf_a, new_buf_b, _pairs_to_state(s_new))
    return y.reshape(bd, 1, d), states


def _forward(x_prompt, x_sample, cache_k, cache_v, cache_logf, page_table, state_conv_a, state_conv_b,
             state_delta, weights, *, tm, tq, tt_a, tt_b, group):
    depth = cache_k.shape[0]
    n_phys = cache_k.shape[1]
    ckt = jnp.transpose(cache_k, (0, 1, 3, 4, 2)).reshape(depth, n_phys, HW, PAGE_SIZE)
    cvt = jnp.transpose(cache_v, (0, 1, 3, 4, 2)).reshape(depth, n_phys, HW, PAGE_SIZE)
    clf = jnp.transpose(cache_logf, (0, 1, 3, 2))
    clf = jnp.concatenate([clf, jnp.zeros((depth, n_phys, 8 - N_HEADS, PAGE_SIZE), F32)], axis=2)
    n_rows = depth * n_phys * 8
    rows = max(r for r in range(8, min(n_rows, 2048) + 1, 8) if n_rows % r == 0)
    clcs = _page_cumsum(clf.reshape(n_rows, PAGE_SIZE), rows=rows).reshape(depth, n_phys, 8, PAGE_SIZE)
    xp, xs = x_prompt, x_sample
    prompt_states, sample_states = [], []
    for l in range(depth):
        p = _layer_weights(l, *weights)
        xp, sp = _prompt_layer(xp, p, tm=tm, tq=tq, tt_a=tt_a, tt_b=tt_b)
        prompt_states.append(sp)
        xs, ss = _sample_layer(xs, p, l, state_conv_a[l], state_conv_b[l], state_delta[l], ckt, cvt, clcs,
                               page_table, group=group)
        sample_states.append(ss)
    ps = [jnp.stack(t) for t in zip(*prompt_states)]
    ss = [jnp.stack(t) for t in zip(*sample_states)]
    return (xp, xs, *ps, *ss)


def kernel(x_prompt, x_sample, cache_k, cache_v, cache_logf, page_table, state_conv_a, state_conv_b, state_delta,
           w_in, conv_a_w, conv_a_b, ln_a_g, ln_a_b, conv_b_w, a_log, dt_bias, norm_b_g, f_bias, w_out, g_pre_mix,
           g_post_mix, g_pre_ffn, g_post_ffn, w_gate, w_up, w_down):
    weights = (w_in, conv_a_w, conv_a_b, ln_a_g, ln_a_b, conv_b_w, a_log, dt_bias, norm_b_g, f_bias, w_out,
               g_pre_mix, g_post_mix, g_pre_ffn, g_post_ffn, w_gate, w_up, w_down)
    return _forward(x_prompt, x_sample, cache_k, cache_v, cache_logf, page_table, state_conv_a, state_conv_b,
                    state_delta, weights, tm=512, tq=1024, tt_a=512, tt_b=256, group=16)
```
